```python
import math
import jax, jax.numpy as jnp
from jax import lax
import numpy as np

D_MODEL = 1024
BATCH = 16
SEQ = 4096
DEPTH = 4

CTX_LEN = 256
GRID_W = 64
RMS_EPS = 1e-6
ROPE_BASE = 10000.0

GDN_HEADS = 4
GDN_DK = 128
GDN_DV = 128
GDN_QK_W = GDN_HEADS * GDN_DK
GDN_V_W = GDN_HEADS * GDN_DV
CONV_W = 5
CHUNK = 64
N_DIR = 2

FNET_GROUPS = 4
FNET_GROUP_CH = 64
FNET_W = FNET_GROUPS * FNET_GROUP_CH

NA_HEADS = 4
NA_DH = 64
NA_W = NA_HEADS * NA_DH
WIN_R = 8
WIN_C = 16

N_BRANCH = 3

N_EXPERTS = 16
EC_CAPACITY_FACTOR = 2
D_EXPERT = 1024

IN_SPLITS = (GDN_QK_W, GDN_QK_W, GDN_V_W, GDN_V_W, N_DIR * GDN_HEADS, N_DIR * GDN_HEADS,
             FNET_W, NA_W, NA_W, NA_W, N_BRANCH * D_MODEL)
N_IN = sum(IN_SPLITS)

kernel_name = 'hybrid_gdn_fnet_natten_ec_moe_dit'


def split_cols(p):
    idx = np.cumsum(IN_SPLITS)[:-1].tolist()
    return jnp.split(p, idx, axis=-1)


def rmsnorm(x, w):
    xf = x.astype(jnp.float32)
    y = xf * lax.rsqrt(jnp.mean(xf * xf, axis=-1, keepdims=True) + RMS_EPS)
    return (y * w.astype(jnp.float32)).astype(x.dtype)


def l2norm(x):
    xf = x.astype(jnp.float32)
    return (xf * lax.rsqrt(jnp.sum(xf * xf, axis=-1, keepdims=True) + RMS_EPS)).astype(x.dtype)


def axial_rope(x):
    b_, t_, h_, dh = x.shape
    nf = dh // 4
    inv = ROPE_BASE ** (-jnp.arange(nf, dtype=jnp.float32) / nf)
    t = jnp.arange(t_)
    pos = jnp.stack([t // GRID_W, t % GRID_W], axis=-1).astype(jnp.float32)
    ang = pos[:, :, None] * inv
    cos = jnp.cos(ang)[None, :, None]
    sin = jnp.sin(ang)[None, :, None]
    xr = x.astype(jnp.float32).reshape(b_, t_, h_, 2, 2, nf)
    x1, x2 = xr[..., 0, :], xr[..., 1, :]
    out = jnp.stack([x1 * cos - x2 * sin, x2 * cos + x1 * sin], axis=-2)
    return out.reshape(b_, t_, h_, dh).astype(x.dtype)


def short_conv(u, w):
    ch = u.shape[-1]
    return lax.conv_general_dilated(u, w[:, None, :].astype(u.dtype), window_strides=(1,),
                                    padding=[(CONV_W // 2, CONV_W // 2)],
                                    dimension_numbers=('NWC', 'WIO', 'NWC'), feature_group_count=ch)


def gdn_prep(q, k, v, a, b, conv_w, a_log, dt_bias, rotary):
    b_, t_, _ = q.shape
    qkv = jax.nn.silu(short_conv(jnp.concatenate([q, k, v], axis=-1), conv_w))
    q, k, v = jnp.split(qkv, [GDN_QK_W, 2 * GDN_QK_W], axis=-1)
    q = l2norm(q.reshape(b_, t_, GDN_HEADS, GDN_DK))
    k = l2norm(k.reshape(b_, t_, GDN_HEADS, GDN_DK))
    if rotary:
        q = axial_rope(q)
        k = axial_rope(k)
    q = q * (GDN_DK ** -0.5)
    v = v.reshape(b_, t_, GDN_HEADS, GDN_DV)
    a = a.reshape(b_, t_, N_DIR, GDN_HEADS).astype(jnp.float32)
    g = -jnp.exp(a_log.astype(jnp.float32)) * jax.nn.softplus(a + dt_bias.astype(jnp.float32))
    beta = jax.nn.sigmoid(b.reshape(b_, t_, N_DIR, GDN_HEADS).astype(jnp.float32))
    return q, k, v, g, beta


def gdn_chunked(q, k, v, g, beta, s0):
    f32 = jnp.float32
    b_, t_, h_, _ = q.shape
    dv = v.shape[-1]
    n = t_ // CHUNK

    def chunks(a):
        a = a.astype(f32).reshape(b_, n, CHUNK, h_, *a.shape[3:])
        return jnp.moveaxis(jnp.moveaxis(a, 1, 0), 3, 2)

    qc, kc, vc = chunks(q), chunks(k), chunks(v)
    gc = jnp.cumsum(chunks(g), axis=-1)
    bc = chunks(beta)
    kb = kc * bc[..., None]
    vb = vc * bc[..., None]
    tri = jnp.tril(jnp.ones((CHUNK, CHUNK), bool))
    strict = jnp.tril(jnp.ones((CHUNK, CHUNK), bool), -1)
    diff = gc[..., :, None] - gc[..., None, :]
    decay = jnp.where(tri, jnp.exp(jnp.where(tri, diff, 0.0)), 0.0)
    m = jnp.where(strict, jnp.einsum('nbhik,nbhjk->nbhij', kb, kc) * decay, 0.0)
    a_mat = m + jnp.eye(CHUNK, dtype=f32)
    rhs = jnp.concatenate([vb, kb * jnp.exp(gc)[..., None]], axis=-1)
    sol = lax.linalg.triangular_solve(a_mat, rhs, left_side=True, lower=True, unit_diagonal=True)
    u, w = sol[..., :dv], sol[..., dv:]
    qk = jnp.where(tri, jnp.einsum('nbhik,nbhjk->nbhij', qc, kc) * decay, 0.0)
    qg = qc * jnp.exp(gc)[..., None]
    kg = kc * jnp.exp(gc[..., -1:] - gc)[..., None]
    g_last = jnp.exp(gc[..., -1])

    def step(s, inp):
        qg_i, kg_i, u_i, w_i, qk_i, gl_i = inp
        v_new = u_i - jnp.einsum('bhck,bhkv->bhcv', w_i, s)
        o = jnp.einsum('bhck,bhkv->bhcv', qg_i, s) + jnp.einsum('bhij,bhjv->bhiv', qk_i, v_new)
        s = s * gl_i[..., None, None] + jnp.einsum('bhck,bhcv->bhkv', kg_i, v_new)
        return s, o

    s_fin, o = lax.scan(step, s0.astype(f32), (qg, kg, u, w, qk, g_last))
    o = jnp.swapaxes(jnp.moveaxis(o, 0, 1), 2, 3).reshape(b_, t_, h_, dv)
    return o.astype(v.dtype), s_fin


def gdn_bidir(q, k, v, g, beta, s0_fwd, s0_bwd):
    o_f, s_f = gdn_chunked(q, k, v, g[:, :, 0], beta[:, :, 0], s0_fwd)
    fl = lambda a: jnp.flip(a, axis=1)
    o_b, s_b = gdn_chunked(fl(q), fl(k), fl(v), fl(g[:, :, 1]), fl(beta[:, :, 1]), s0_bwd)
    return o_f + fl(o_b), s_f, s_b


def gdn_out(o, z, norm_w):
    b_, t_, _, _ = o.shape
    zh = z.reshape(b_, t_, GDN_HEADS, GDN_DV)
    return (rmsnorm(o, norm_w) * jax.nn.silu(zh)).reshape(b_, t_, GDN_V_W)


def fourier_mix(u):
    b_, t_, _ = u.shape
    ug = u.astype(jnp.float32).reshape(b_, t_, FNET_GROUPS, FNET_GROUP_CH)
    y = jnp.fft.fftn(ug, axes=(1, 3), norm='ortho').real
    return y.reshape(b_, t_, FNET_W).astype(u.dtype)


def na_latent(q, k, v, k_ctx, v_ctx, rpb):
    b_, t_, h_, dh = q.shape
    rows = t_ // GRID_W
    kr = min(WIN_R, rows)
    qg = q.reshape(b_, rows, GRID_W, h_, dh)
    kg = k.reshape(b_, rows, GRID_W, h_, dh)
    vg = v.reshape(b_, rows, GRID_W, h_, dh)
    r_ids = jnp.arange(rows)
    c_ids = jnp.arange(GRID_W)
    r_start = jnp.clip(r_ids - kr // 2, 0, rows - kr)
    c_start = jnp.clip(c_ids - WIN_C // 2, 0, GRID_W - WIN_C)
    col_idx = c_start[:, None] + jnp.arange(WIN_C)
    dc = col_idx - c_ids[:, None] + (WIN_C - 1)
    scale = dh ** -0.5
    n_loc = kr * WIN_C

    def row_block(r):
        rs = r_start[r]
        q_r = lax.dynamic_index_in_dim(qg, r, axis=1, keepdims=False)
        k_r = lax.dynamic_slice_in_dim(kg, rs, kr, axis=1)[:, :, col_idx]
        v_r = lax.dynamic_slice_in_dim(vg, rs, kr, axis=1)[:, :, col_idx]
        dr = rs + jnp.arange(kr) - r + (WIN_R - 1)
        bias = jnp.transpose(rpb[:, dr][:, :, dc], (0, 2, 1, 3))
        s_loc = jnp.einsum('bjhd,bajchd->bhjac', q_r, k_r).astype(jnp.float32) * scale + bias.astype(jnp.float32)
        s_ctx = jnp.einsum('bjhd,blhd->bhjl', q_r, k_ctx).astype(jnp.float32) * scale
        s = jnp.concatenate([s_loc.reshape(b_, h_, GRID_W, n_loc), s_ctx], axis=-1)
        p = jax.nn.softmax(s, axis=-1).astype(v.dtype)
        p_loc = p[..., :n_loc].reshape(b_, h_, GRID_W, kr, WIN_C)
        return (jnp.einsum('bhjac,bajchd->bjhd', p_loc, v_r)
                + jnp.einsum('bhjl,blhd->bjhd', p[..., n_loc:], v_ctx))

    o = lax.map(row_block, r_ids)
    return jnp.transpose(o, (1, 0, 2, 3, 4)).reshape(b_, t_, h_ * dh)


def na_context(q, k, v):
    b_, l_, h_, dh = q.shape
    s = jnp.einsum('blhd,bmhd->bhlm', q, k).astype(jnp.float32) * (dh ** -0.5)
    p = jax.nn.softmax(s, axis=-1).astype(v.dtype)
    return jnp.einsum('bhlm,bmhd->blhd', p, v).reshape(b_, l_, h_ * dh)


def merge(g_raw, oa, ob, oc, w_br_a, w_br_b, w_br_c, w_out):
    ga, gb, gc = jnp.split(jax.nn.sigmoid(g_raw), N_BRANCH, axis=-1)
    y = ga * (oa @ w_br_a) + gb * (ob @ w_br_b) + gc * (oc @ w_br_c)
    return y @ w_out


def mixer(h, hc, w_in, conv_w, a_log, dt_bias, gdn_norm_w, na_qn_w, na_kn_w, na_rpb,
          w_br_a, w_br_b, w_br_c, w_out, need_ctx):
    b_, t_, _ = h.shape
    lc = hc.shape[1]
    qa, ka, va, za, aa, ba, ub, qn, kn, vn, gr = split_cols(h @ w_in)
    qac, kac, vac, zac, aac, bac, ubc, qnc, knc, vnc, grc = split_cols(hc @ w_in)
    ql, kl, vl, gl, bl = gdn_prep(qa, ka, va, aa, ba, conv_w, a_log, dt_bias, True)
    qc_, kc_, vc_, gc_, bc_ = gdn_prep(qac, kac, vac, aac, bac, conv_w, a_log, dt_bias, False)
    zero = jnp.zeros((b_, GDN_HEADS, GDN_DK, GDN_DV), jnp.float32)
    o_ac, s_f, s_b = gdn_bidir(qc_, kc_, vc_, gc_, bc_, zero, zero)
    o_al, _, _ = gdn_bidir(ql, kl, vl, gl, bl, s_f, s_b)
    heads = lambda u: u.reshape(u.shape[0], u.shape[1], NA_HEADS, NA_DH)
    k_ctx = rmsnorm(heads(knc), na_kn_w)
    v_ctx = heads(vnc)
    o_cl = na_latent(rmsnorm(heads(qn), na_qn_w), rmsnorm(heads(kn), na_kn_w), heads(vn), k_ctx, v_ctx, na_rpb)
    y_lat = merge(gr, gdn_out(o_al, za, gdn_norm_w), fourier_mix(ub), o_cl, w_br_a, w_br_b, w_br_c, w_out)
    if not need_ctx:
        return y_lat, None
    o_cc = na_context(rmsnorm(heads(qnc), na_qn_w), k_ctx, v_ctx)
    y_ctx = merge(grc, gdn_out(o_ac, zac, gdn_norm_w), fourier_mix(ubc), o_cc, w_br_a, w_br_b, w_br_c, w_out)
    return y_lat, y_ctx


def expert_choice_ffn(h, w_router, w_gate, w_up, w_down):
    b_, t_, _ = h.shape
    cap = EC_CAPACITY_FACTOR * t_ // N_EXPERTS
    aff = jax.nn.softmax(jnp.einsum('btd,de->bte', h, w_router).astype(jnp.float32), axis=-1)
    top_aff, top_idx = lax.top_k(jnp.swapaxes(aff, 1, 2), cap)
    bidx = jnp.arange(b_)[:, None, None]
    xe = h[bidx, top_idx]
    hid = jax.nn.silu(jnp.einsum('becd,edf->becf', xe, w_gate)) * jnp.einsum('becd,edf->becf', xe, w_up)
    ye = jnp.einsum('becf,efd->becd', hid, w_down) * top_aff[..., None].astype(h.dtype)
    return jnp.zeros_like(h).at[bidx, top_idx].add(ye)


def setup_inputs(seed: int = 0) -> dict:
    key = jax.random.key(seed)
    ks = jax.random.split(key, 26)
    f32 = jnp.float32
    L, D = DEPTH, D_MODEL
    nrm = lambda k, shape, s: jax.random.normal(k, shape, f32) * s
    dt = jnp.exp(jax.random.uniform(ks[11], (L, N_DIR, GDN_HEADS), f32, math.log(1e-3), math.log(1e-1)))
    return {
        'x': nrm(ks[0], (BATCH, SEQ, D), 1.0),
        'c': nrm(ks[1], (BATCH, D), 1.0),
        'ctx': nrm(ks[2], (BATCH, CTX_LEN, D), 1.0),
        'c_ctx': nrm(ks[3], (D,), 1.0),
        'w_mod': nrm(ks[4], (L, D, 6 * D), 0.5 * D ** -0.5),
        'b_mod': nrm(ks[5], (L, 6 * D), 0.01),
        'norm1_w': 1.0 + nrm(ks[6], (L, D), 0.1),
        'norm2_w': 1.0 + nrm(ks[7], (L, D), 0.1),
        'w_in': nrm(ks[8], (L, D, N_IN), D ** -0.5),
        'conv_w': nrm(ks[9], (L, CONV_W, 2 * GDN_QK_W + GDN_V_W), CONV_W ** -0.5),
        'a_log': jnp.log(jax.random.uniform(ks[10], (L, N_DIR, GDN_HEADS), f32, 1.0, 16.0)),
        'dt_bias': dt + jnp.log(-jnp.expm1(-dt)),
        'gdn_norm_w': 1.0 + nrm(ks[12], (L, GDN_DV), 0.1),
        'na_qn_w': 1.0 + nrm(ks[13], (L, NA_DH), 0.1),
        'na_kn_w': 1.0 + nrm(ks[14], (L, NA_DH), 0.1),
        'na_rpb': nrm(ks[15], (L, NA_HEADS, 2 * WIN_R - 1, 2 * WIN_C - 1), 0.1),
        'w_br_a': nrm(ks[16], (L, GDN_V_W, D), GDN_V_W ** -0.5),
        'w_br_b': nrm(ks[17], (L, FNET_W, D), FNET_W ** -0.5),
        'w_br_c': nrm(ks[18], (L, NA_W, D), NA_W ** -0.5),
        'w_out': nrm(ks[19], (L, D, D), D ** -0.5),
        'w_router': nrm(ks[20], (L, D, N_EXPERTS), D ** -0.5),
        'w_e_gate': nrm(ks[21], (L, N_EXPERTS, D, D_EXPERT), D ** -0.5),
        'w_e_up': nrm(ks[22], (L, N_EXPERTS, D, D_EXPERT), D ** -0.5),
        'w_e_down': nrm(ks[23], (L, N_EXPERTS, D_EXPERT, D), D_EXPERT ** -0.5),
    }


def reference(x, c, ctx, c_ctx, w_mod, b_mod, norm1_w, norm2_w, w_in, conv_w, a_log, dt_bias, gdn_norm_w,
              na_qn_w, na_kn_w, na_rpb, w_br_a, w_br_b, w_br_c, w_out, w_router, w_e_gate, w_e_up, w_e_down):
    for i in range(DEPTH):
        last = i == DEPTH - 1
        mod = jax.nn.silu(c) @ w_mod[i] + b_mod[i]
        mod_c = jax.nn.silu(c_ctx) @ w_mod[i] + b_mod[i]
        sh1, sc1, gt1, sh2, sc2, gt2 = jnp.split(mod[:, None, :], 6, axis=-1)
        csh1, csc1, cgt1, csh2, csc2, cgt2 = jnp.split(mod_c[None, None, :], 6, axis=-1)
        h = rmsnorm(x, norm1_w[i]) * (1 + sc1) + sh1
        hc = rmsnorm(ctx, norm1_w[i]) * (1 + csc1) + csh1
        y, yc = mixer(h, hc, w_in[i], conv_w[i], a_log[i], dt_bias[i], gdn_norm_w[i], na_qn_w[i], na_kn_w[i],
                      na_rpb[i], w_br_a[i], w_br_b[i], w_br_c[i], w_out[i], not last)
        x = x + gt1 * y
        h2 = rmsnorm(x, norm2_w[i]) * (1 + sc2) + sh2
        x = x + gt2 * expert_choice_ffn(h2, w_router[i], w_e_gate[i], w_e_up[i], w_e_down[i])
        if not last:
            ctx = ctx + cgt1 * yc
            hc2 = rmsnorm(ctx, norm2_w[i]) * (1 + csc2) + csh2
            ctx = ctx + cgt2 * expert_choice_ffn(hc2, w_router[i], w_e_gate[i], w_e_up[i], w_e_down[i])
    return x
```

```python
import functools
import math

import jax
import jax.numpy as jnp
from jax import lax
from jax.experimental import pallas as pl
from jax.experimental.pallas import tpu as pltpu

D_MODEL = 1024
DEPTH = 4
GRID_W = 64
RMS_EPS = 1e-6
ROPE_BASE = 10000.0

GDN_HEADS = 4
GDN_DK = 128
GDN_DV = 128
GDN_QK_W = GDN_HEADS * GDN_DK
GDN_V_W = GDN_HEADS * GDN_DV
CONV_W = 5
CHUNK = 64
N_DIR = 2

FNET_GROUPS = 4
FNET_GROUP_CH = 64
FNET_W = FNET_GROUPS * FNET_GROUP_CH

NA_HEADS = 4
NA_DH = 64
NA_W = NA_HEADS * NA_DH
WIN_R = 8
WIN_C = 16

N_BRANCH = 3
N_EXPERTS = 16
EC_CAPACITY_FACTOR = 2
D_EXPERT = 1024

LANES = 128
BF16_SUBLANES = 16
VMEM_LIMIT_BIG = 56 * 1024 * 1024

TOK_BLK = 256
QKV_W = 2 * GDN_QK_W + GDN_V_W
NAQKV_W = 3 * NA_W
GATE_W = N_BRANCH * D_MODEL
AB_PAD = LANES
N_IN_PAD = QKV_W + GDN_V_W + FNET_W + NAQKV_W + GATE_W + AB_PAD

F32 = jnp.float32
BF16 = jnp.bfloat16
HI = lax.Precision.HIGHEST
NEG_BIG = -1e30

_NT = (((1,), (1,)), ((), ()))
_TN = (((0,), (0,)), ((), ()))


def _dot(a, b, prec=None):
    return jnp.dot(a, b, preferred_element_type=F32, precision=prec)


def _dg(a, b, dims, prec=None):
    return lax.dot_general(a, b, dims, preferred_element_type=F32, precision=prec)


def _silu(v):
    return v * jax.nn.sigmoid(v)


def _cparams(sem, vmem=None):
    return pltpu.CompilerParams(dimension_semantics=sem, vmem_limit_bytes=vmem)


def _mod_kernel(c_ref, w_ref, b_ref, o_ref):
    s = _silu(c_ref[...]).astype(BF16)
    o_ref[0] = _dot(s, w_ref[0]) + b_ref[0]


def _modulation(cs, w_mod, b_mod):
    L, D, N = w_mod.shape
    R = cs.shape[0]
    tn = 1536
    return pl.pallas_call(
        _mod_kernel,
        grid=(L, N // tn),
        in_specs=[
            pl.BlockSpec((R, D), lambda l, n: (0, 0)),
            pl.BlockSpec((1, D, tn), lambda l, n: (l, 0, n)),
            pl.BlockSpec((1, 1, tn), lambda l, n: (l, 0, n)),
        ],
        out_specs=pl.BlockSpec((1, R, tn), lambda l, n: (l, 0, n)),
        out_shape=jax.ShapeDtypeStruct((L, R, N), F32),
        compiler_params=_cparams(("parallel", "parallel")),
        name="modulation",
    )(cs, w_mod, b_mod)


def _rms_mod(x, nw, shift, scale):
    ms = jnp.mean(x * x, axis=-1, keepdims=True)
    return (x * lax.rsqrt(ms + RMS_EPS) * nw) * (1.0 + scale) + shift


def _proj_kernel(x_ref, mod_ref, nw_ref, w_ref, al_ref, dt_ref,
                 qkv_ref, z_ref, ub_ref, naqkv_ref, gr_ref, gb_ref):
    h = _rms_mod(x_ref[0], nw_ref[...], mod_ref[0:1, :], mod_ref[1:2, :]).astype(BF16)

    def mm(lo, width):
        return _dot(h, w_ref[:, lo:lo + width])

    col = 0
    for ref, width in ((qkv_ref, QKV_W), (z_ref, GDN_V_W), (ub_ref, FNET_W),
                       (naqkv_ref, NAQKV_W), (gr_ref, GATE_W)):
        step = 512 if width % 512 == 0 else 256
        for j in range(width // step):
            ref[0, :, j * step:(j + 1) * step] = mm(col + j * step, step).astype(ref.dtype)
        col += width
    ab = mm(col, AB_PAD)
    sp_in = ab + dt_ref[...]
    softplus = jnp.maximum(sp_in, 0.0) + jnp.log(1.0 + jnp.exp(-jnp.abs(sp_in)))
    g = -jnp.exp(al_ref[...]) * softplus
    lane = lax.broadcasted_iota(jnp.int32, ab.shape, 1)
    gb_ref[0] = jnp.where(lane < N_DIR * GDN_HEADS, g, jax.nn.sigmoid(ab))


def _input_proj(xs, modv, nw, w_in, al, dtb, n_lat_blk):
    B, Tt, D = xs.shape
    nblk = Tt // TOK_BLK
    tok = lambda w: pl.BlockSpec((1, TOK_BLK, w), lambda b, t: (b, t, 0))
    widths = (QKV_W, GDN_V_W, FNET_W, NAQKV_W, GATE_W)
    return pl.pallas_call(
        _proj_kernel,
        grid=(B, nblk),
        in_specs=[
            tok(D),
            pl.BlockSpec((None, None, 6, D), lambda b, t: (b, t // n_lat_blk, 0, 0)),
            pl.BlockSpec((1, D), lambda b, t: (0, 0)),
            pl.BlockSpec((D, N_IN_PAD), lambda b, t: (0, 0)),
            pl.BlockSpec((1, AB_PAD), lambda b, t: (0, 0)),
            pl.BlockSpec((1, AB_PAD), lambda b, t: (0, 0)),
        ],
        out_specs=[tok(w) for w in widths] + [tok(AB_PAD)],
        out_shape=[jax.ShapeDtypeStruct((B, Tt, w), BF16) for w in widths]
        + [jax.ShapeDtypeStruct((B, Tt, AB_PAD), F32)],
        compiler_params=_cparams(("parallel", "parallel"), VMEM_LIMIT_BIG),
        name="input_proj",
    )(xs, modv, nw, w_in, al, dtb)


def _gdn_prep_kernel(u_ref, cw_ref, cos_ref, sin_ref, o_ref, *, T, Tt):
    j = pl.program_id(1)
    w = cw_ref[...]
    is_qk = j < 2 * GDN_HEADS
    scale = jnp.where(j < GDN_HEADS, GDN_DK ** -0.5, 1.0).astype(F32)
    halo = BF16_SUBLANES
    lane = lax.broadcasted_iota(jnp.int32, (TOK_BLK, LANES), 1)
    rowi = lax.broadcasted_iota(jnp.int32, (TOK_BLK, LANES), 0)

    def body(c, carry):
        start = pl.multiple_of(c * TOK_BLK, TOK_BLK)
        ps = pl.multiple_of(jnp.maximum(start - halo, 0), halo)
        ns = pl.multiple_of(jnp.minimum(start + TOK_BLK, Tt - halo), halo)
        ext = jnp.concatenate([
            u_ref[0, pl.ds(ps, halo), :].astype(F32),
            u_ref[0, pl.ds(start, TOK_BLK), :].astype(F32),
            u_ref[0, pl.ds(ns, halo), :].astype(F32)], axis=0)
        row = start + rowi
        seg_lo = jnp.where(start < T, 0, T)
        seg_hi = jnp.where(start < T, T, Tt)
        acc = jnp.zeros((TOK_BLK, LANES), F32)
        for d in range(-(CONV_W // 2), CONV_W // 2 + 1):
            sh = ext[halo + d:halo + d + TOK_BLK, :]
            ok = (row + d >= seg_lo) & (row + d < seg_hi)
            acc = acc + jnp.where(ok, sh, 0.0) * w[d + CONV_W // 2:d + CONV_W // 2 + 1, :]
        y = _silu(acc)
        yn = y * lax.rsqrt(jnp.sum(y * y, axis=-1, keepdims=True) + RMS_EPS)
        partner = jnp.where((lane & 32) == 0, pltpu.roll(yn, LANES - 32, 1), pltpu.roll(yn, 32, 1))
        yr = (yn * cos_ref[pl.ds(start, TOK_BLK), :] + partner * sin_ref[pl.ds(start, TOK_BLK), :]) * scale
        o_ref[0, pl.ds(start, TOK_BLK), :] = jnp.where(is_qk, yr, y)
        return carry

    lax.fori_loop(0, Tt // TOK_BLK, body, 0)


def _gdn_prep(qkv, conv_w, cos_t, sin_t, T):
    B, Tt, W = qkv.shape
    return pl.pallas_call(
        functools.partial(_gdn_prep_kernel, T=T, Tt=Tt),
        grid=(B, W // LANES),
        in_specs=[
            pl.BlockSpec((1, Tt, LANES), lambda b, j: (b, 0, j)),
            pl.BlockSpec((CONV_W, LANES), lambda b, j: (0, j)),
            pl.BlockSpec((Tt, LANES), lambda b, j: (0, 0)),
            pl.BlockSpec((Tt, LANES), lambda b, j: (0, 0)),
        ],
        out_specs=pl.BlockSpec((1, Tt, LANES), lambda b, j: (b, 0, j)),
        out_shape=jax.ShapeDtypeStruct((B, Tt, W), F32),
        compiler_params=_cparams(("parallel", "parallel")),
        name="gdn_prep",
    )(qkv, conv_w, cos_t, sin_t)


def _unit_tri_inverse(m, eye, blk16, blk32):
    n1 = -jnp.where(blk16, m, 0.0)
    n2 = _dot(n1, n1, HI)
    n4 = _dot(n2, n2, HI)
    n8 = _dot(n4, n4, HI)
    p = eye + n1
    p = p + _dot(p, n2, HI)
    p = p + _dot(p, n4, HI)
    p = p + _dot(p, n8, HI)
    c1 = jnp.where(blk32 & jnp.logical_not(blk16), m, 0.0)
    p = p - _dot(_dot(p, c1, HI), p, HI)
    c2 = jnp.where(blk32, 0.0, m)
    return p - _dot(_dot(p, c2, HI), p, HI)


def _gdn_scan_kernel(qf_ref, kf_ref, vf_ref, gf_ref, qb_ref, kb_ref, vb_ref, gbk_ref,
                     of_ref, ob_ref, s_ref):
    @pl.when(pl.program_id(1) == 0)
    def _():
        s_ref[...] = jnp.zeros_like(s_ref)

    ii = lax.broadcasted_iota(jnp.int32, (CHUNK, CHUNK), 0)
    jj = lax.broadcasted_iota(jnp.int32, (CHUNK, CHUNK), 1)
    eye = (ii == jj).astype(F32)
    blk16 = (ii >> 4) == (jj >> 4)
    blk32 = (ii >> 5) == (jj >> 5)

    dirs = ((qf_ref, kf_ref, vf_ref, gf_ref, of_ref, ii >= jj, ii > jj, CHUNK - 1),
            (qb_ref, kb_ref, vb_ref, gbk_ref, ob_ref, ii <= jj, ii < jj, 0))
    for d, (q_ref, k_ref, v_ref, g_ref, o_ref, le, lt, last) in enumerate(dirs):
        gbv = g_ref[0]
        lef = le.astype(F32)
        gc_all = _dot(lef, gbv, HI)
        gc_rows = _dg(gbv, lef, (((0,), (1,)), ((), ())), HI)
        for h in range(GDN_HEADS):
            l = d * GDN_HEADS + h
            hs = slice(h * GDN_DK, (h + 1) * GDN_DK)
            q = q_ref[0, :, hs]
            k = k_ref[0, :, hs]
            v = v_ref[0, :, hs]
            kb16 = k.astype(BF16)
            kk = _dg(kb16, kb16, _NT)
            qk = _dg(q.astype(BF16), kb16, _NT)
            beta = gbv[:, N_DIR * GDN_HEADS + l:N_DIR * GDN_HEADS + l + 1]
            gc = gc_all[:, l:l + 1]
            gcr = gc_rows[l:l + 1, :]
            tot = gc_all[last:last + 1, l:l + 1]
            dec = jnp.where(le, jnp.exp(jnp.where(le, gc - gcr, 0.0)), 0.0)
            m = jnp.where(lt, beta * kk * dec, 0.0)
            tinv = _unit_tri_inverse(m, eye, blk16, blk32)
            eg = jnp.exp(gc)
            rhs = jnp.concatenate([v * beta, k * (beta * eg)], axis=1)
            sol = _dot(tinv, rhs, HI)
            u = sol[:, :GDN_DV]
            w = sol[:, GDN_DV:]
            qkm = jnp.where(le, qk * dec, 0.0)
            qg = q * eg
            kg = k * jnp.exp(tot - gc)
            s = s_ref[l]
            s16 = s.astype(BF16)
            v_new = u - _dot(w.astype(BF16), s16)
            vn16 = v_new.astype(BF16)
            o = _dot(qg.astype(BF16), s16) + _dot(qkm.astype(BF16), vn16)
            s_ref[l] = s * jnp.exp(tot) + _dg(kg.astype(BF16), vn16, _TN)
            o_ref[0, :, hs] = o


def _gdn_scan(qkvp, gb, T):
    B, Tt, _ = qkvp.shape
    n_lat = T // CHUNK
    n_all = Tt // CHUNK
    n_ctx = n_all - n_lat

    def fwd(c):
        return jnp.where(c < n_ctx, n_lat + c, c - n_ctx)

    def bwd(c):
        return n_all - 1 - c

    def spec(order, col, width):
        return pl.BlockSpec((1, CHUNK, width), lambda b, c: (b, order(c), col))

    hw = GDN_QK_W
    return pl.pallas_call(
        _gdn_scan_kernel,
        grid=(B, n_all),
        in_specs=[spec(fwd, 0, hw), spec(fwd, 1, hw), spec(fwd, 2, hw), spec(fwd, 0, LANES),
                  spec(bwd, 0, hw), spec(bwd, 1, hw), spec(bwd, 2, hw), spec(bwd, 0, LANES)],
        out_specs=[spec(fwd, 0, hw), spec(bwd, 0, hw)],
        out_shape=[jax.ShapeDtypeStruct((B, Tt, GDN_V_W), F32)] * 2,
        scratch_shapes=[pltpu.VMEM((N_DIR * GDN_HEADS, GDN_DK, GDN_DV), F32)],
        compiler_params=_cparams(("parallel", "arbitrary")),
        name="gdn_scan",
    )(qkvp, qkvp, qkvp, gb, qkvp, qkvp, qkvp, gb)


def _fnet_kernel(c_ref, s_ref, u_ref, cc_ref, sc_ref, *rest):
    o_ref = rest[-1]
    u = u_ref[0]
    p = _dot(c_ref[...], u).astype(BF16)
    q = _dot(s_ref[...], u).astype(BF16)
    o_ref[0] = (_dot(p, cc_ref[...]) - _dot(q, sc_ref[...])).astype(o_ref.dtype)


def _dft_mats(n):
    j = jnp.arange(n, dtype=jnp.int32)
    ang = ((j[:, None] * j[None, :]) % n).astype(F32) * (2.0 * math.pi / n)
    sc = 1.0 / math.sqrt(n)
    return jnp.cos(ang) * sc, jnp.sin(ang) * sc


def _fnet(ub, mats_lat, mats_ctx, mats_ch, T):
    B, Tt, W = ub.shape
    Lc = Tt - T
    cc, sc = mats_ch
    tm = min(512, T)
    const = lambda shape: pl.BlockSpec(shape, lambda m, b: (0, 0))
    out = pl.pallas_call(
        _fnet_kernel,
        grid=(T // tm, B),
        in_specs=[
            pl.BlockSpec((tm, T), lambda m, b: (m, 0)),
            pl.BlockSpec((tm, T), lambda m, b: (m, 0)),
            pl.BlockSpec((1, T, W), lambda m, b: (b, 0, 0)),
            const((W, W)), const((W, W)),
        ],
        out_specs=pl.BlockSpec((1, tm, W), lambda m, b: (b, m, 0)),
        out_shape=jax.ShapeDtypeStruct((B, Tt, W), BF16),
        compiler_params=_cparams(("parallel", "parallel"), VMEM_LIMIT_BIG),
        name="fnet_latent",
    )(mats_lat[0], mats_lat[1], ub, cc, sc)
    cblk = T // Lc
    return pl.pallas_call(
        _fnet_kernel,
        grid=(1, B),
        in_specs=[
            const((Lc, Lc)), const((Lc, Lc)),
            pl.BlockSpec((1, Lc, W), lambda m, b: (b, cblk, 0)),
            const((W, W)), const((W, W)),
            pl.BlockSpec(memory_space=pl.ANY),
        ],
        out_specs=pl.BlockSpec((1, Lc, W), lambda m, b: (b, cblk, 0)),
        out_shape=jax.ShapeDtypeStruct((B, Tt, W), BF16),
        input_output_aliases={5: 0},
        compiler_params=_cparams(("parallel", "parallel")),
        name="fnet_context",
    )(mats_ctx[0], mats_ctx[1], ub, cc, sc, out)


def _na_prep_kernel(x_ref, qw_ref, kw_ref, bd_ref, q_ref, k_ref):
    bd = bd_ref[...]

    def nrm(a, w):
        ms = _dot(a * a, bd, HI)
        return a * lax.rsqrt(ms + RMS_EPS) * w

    q = x_ref[0, :, 0:NA_W].astype(F32)
    k = x_ref[0, :, NA_W:2 * NA_W].astype(F32)
    q_ref[0] = (nrm(q, qw_ref[...]) * (NA_DH ** -0.5)).astype(BF16)
    k_ref[0] = nrm(k, kw_ref[...]).astype(BF16)


def _na_prep(naqkv, qw, kw, bd):
    B, Tt, _ = naqkv.shape
    const = lambda shape: pl.BlockSpec(shape, lambda b, t: (0, 0))
    tok = lambda w: pl.BlockSpec((1, TOK_BLK, w), lambda b, t: (b, t, 0))
    return pl.pallas_call(
        _na_prep_kernel,
        grid=(B, Tt // TOK_BLK),
        in_specs=[tok(NAQKV_W), const((1, NA_W)), const((1, NA_W)), const((NA_W, NA_W))],
        out_specs=[tok(NA_W), tok(NA_W)],
        out_shape=[jax.ShapeDtypeStruct((B, Tt, NA_W), BF16)] * 2,
        compiler_params=_cparams(("parallel", "parallel")),
        name="na_prep",
    )(naqkv, qw, kw, bd)


def _attend(q, keys, vals, biases, o_ref):
    for h in range(NA_HEADS):
        hs = slice(h * NA_DH, (h + 1) * NA_DH)
        qh = q[:, hs]
        ss = []
        for kk, bias in zip(keys, biases):
            s = _dg(qh, kk[:, hs], _NT)
            if bias is not None:
                s = s + bias[h]
            ss.append(s)
        m = functools.reduce(jnp.maximum, [jnp.max(s, axis=-1, keepdims=True) for s in ss])
        ps = [jnp.exp(s - m) for s in ss]
        den = sum(jnp.sum(p, axis=-1, keepdims=True) for p in ps)
        o = sum(_dot(p.astype(BF16), vv[:, hs]) for p, vv in zip(ps, vals))
        o_ref[0, :, hs] = (o / den).astype(o_ref.dtype)


def _na_latent_kernel(q_ref, k_ref, v_ref, bias_ref, o_ref, *, T, Lc, rows):
    r = pl.program_id(1)
    start = pl.multiple_of(jnp.clip(r - WIN_R // 2, 0, rows - WIN_R) * GRID_W, GRID_W)
    nwin = WIN_R * GRID_W
    keys = [k_ref[0, pl.ds(start, nwin), :], k_ref[0, T:T + Lc, :]]
    vals = [v_ref[0, pl.ds(start, nwin), :], v_ref[0, T:T + Lc, :]]
    _attend(q_ref[0], keys, vals, [bias_ref, None], o_ref)


def _na_context_kernel(q_ref, k_ref, v_ref, prev_ref, o_ref):
    del prev_ref
    _attend(q_ref[0], [k_ref[0]], [v_ref[0]], [None], o_ref)


def _na(qn, kn, naqkv, bias_tbl, T):
    B, Tt, _ = qn.shape
    Lc = Tt - T
    rows = T // GRID_W
    nwin = WIN_R * GRID_W

    def cls(r):
        return r - jnp.clip(r - WIN_R // 2, 0, rows - WIN_R)

    out = pl.pallas_call(
        functools.partial(_na_latent_kernel, T=T, Lc=Lc, rows=rows),
        grid=(B, rows),
        in_specs=[
            pl.BlockSpec((1, GRID_W, NA_W), lambda b, r: (b, r, 0)),
            pl.BlockSpec((1, Tt, NA_W), lambda b, r: (b, 0, 0)),
            pl.BlockSpec((1, Tt, NA_W), lambda b, r: (b, 0, 2)),
            pl.BlockSpec((None, NA_HEADS, GRID_W, nwin), lambda b, r: (cls(r), 0, 0, 0)),
        ],
        out_specs=pl.BlockSpec((1, GRID_W, NA_W), lambda b, r: (b, r, 0)),
        out_shape=jax.ShapeDtypeStruct((B, Tt, NA_W), BF16),
        compiler_params=_cparams(("parallel", "arbitrary")),
        name="na_latent",
    )(qn, kn, naqkv, bias_tbl)
    cblk = T // Lc
    return pl.pallas_call(
        _na_context_kernel,
        grid=(B,),
        in_specs=[
            pl.BlockSpec((1, Lc, NA_W), lambda b: (b, cblk, 0)),
            pl.BlockSpec((1, Lc, NA_W), lambda b: (b, cblk, 0)),
            pl.BlockSpec((1, Lc, NA_W), lambda b: (b, cblk, 2)),
            pl.BlockSpec(memory_space=pl.ANY),
        ],
        out_specs=pl.BlockSpec((1, Lc, NA_W), lambda b: (b, cblk, 0)),
        out_shape=jax.ShapeDtypeStruct((B, Tt, NA_W), BF16),
        input_output_aliases={3: 0},
        compiler_params=_cparams(("parallel",)),
        name="na_context",
    )(qn, kn, naqkv, out)


def _na_bias_table(rpb):
    cl = jnp.arange(WIN_R)[:, None]
    a = jnp.arange(WIN_R)[None, :]
    dr = a - cl + (WIN_R - 1)
    qc = jnp.arange(GRID_W)
    c_start = jnp.clip(qc - WIN_C // 2, 0, GRID_W - WIN_C)
    kc = jnp.arange(GRID_W)
    dc = kc[None, :] - qc[:, None] + (WIN_C - 1)
    ok = (kc[None, :] >= c_start[:, None]) & (kc[None, :] < c_start[:, None] + WIN_C)
    tbl = rpb[:, dr][:, :, :, jnp.clip(dc, 0, 2 * WIN_C - 2)]
    tbl = jnp.where(ok[None, None, None], tbl.astype(F32), NEG_BIG)
    return jnp.transpose(tbl, (1, 0, 3, 2, 4)).reshape(WIN_R, NA_HEADS, GRID_W, WIN_R * GRID_W)


def _merge_kernel(of_ref, ob_ref, z_ref, fb_ref, oc_ref, gr_ref, x_ref, mod_ref, gnw_ref, n2w_ref,
                  wa_ref, wb_ref, wc_ref, wo_ref, wr_ref, x1_ref, h2_ref, aff_ref):
    o = of_ref[0] + ob_ref[0]
    z = z_ref[0].astype(F32)
    ya = jnp.zeros((TOK_BLK, D_MODEL), F32)
    for h in range(GDN_HEADS):
        hs = slice(h * GDN_DV, (h + 1) * GDN_DV)
        oh = o[:, hs]
        on = oh * lax.rsqrt(jnp.mean(oh * oh, axis=-1, keepdims=True) + RMS_EPS) * gnw_ref[...]
        ya = ya + _dot((on * _silu(z[:, hs])).astype(BF16), wa_ref[hs, :])
    yb = _dot(fb_ref[0], wb_ref[...])
    yc = _dot(oc_ref[0], wc_ref[...])
    gate = lambda j: jax.nn.sigmoid(gr_ref[0, :, j * D_MODEL:(j + 1) * D_MODEL].astype(F32))
    y = gate(0) * ya + gate(1) * yb + gate(2) * yc
    x1 = x_ref[0] + mod_ref[2:3, :] * _dot(y.astype(BF16), wo_ref[...])
    x1_ref[0] = x1
    h2 = _rms_mod(x1, n2w_ref[...], mod_ref[3:4, :], mod_ref[4:5, :]).astype(BF16)
    h2_ref[0] = h2
    logits = _dg(wr_ref[...], h2, _NT)
    ex = jnp.exp(logits - jnp.max(logits, axis=0, keepdims=True))
    aff_ref[0] = ex / jnp.sum(ex, axis=0, keepdims=True)


def _merge(o_f, o_b, z, fb, oc, gr, xs, modv, gnw, n2w, wa, wb, wc, wo, wr, n_lat_blk):
    B, Tt, D = xs.shape
    tok = lambda w: pl.BlockSpec((1, TOK_BLK, w), lambda b, t: (b, t, 0))
    const = lambda shape: pl.BlockSpec(shape, lambda b, t: (0, 0))
    return pl.pallas_call(
        _merge_kernel,
        grid=(B, Tt // TOK_BLK),
        in_specs=[
            tok(GDN_V_W), tok(GDN_V_W), tok(GDN_V_W), tok(FNET_W), tok(NA_W), tok(GATE_W), tok(D),
            pl.BlockSpec((None, None, 6, D), lambda b, t: (b, t // n_lat_blk, 0, 0)),
            const((1, GDN_DV)), const((1, D)),
            const((GDN_V_W, D)), const((FNET_W, D)), const((NA_W, D)), const((D, D)),
            const((N_EXPERTS, D)),
        ],
        out_specs=[tok(D), tok(D), pl.BlockSpec((1, N_EXPERTS, TOK_BLK), lambda b, t: (b, 0, t))],
        out_shape=[jax.ShapeDtypeStruct((B, Tt, D), F32), jax.ShapeDtypeStruct((B, Tt, D), BF16),
                   jax.ShapeDtypeStruct((B, N_EXPERTS, Tt), F32)],
        compiler_params=_cparams(("parallel", "parallel"), VMEM_LIMIT_BIG),
        name="merge",
    )(o_f, o_b, z, fb, oc, gr, xs, modv, gnw, n2w, wa, wb, wc, wo, wr)


def _select_kernel(a_ref, pos_ref, wa_ref, *, T, Lc):
    ii = lax.broadcasted_iota(jnp.int32, (LANES, LANES), 0)
    jj = lax.broadcasted_iota(jnp.int32, (LANES, LANES), 1)
    tri = (ii < jj).astype(BF16)
    cap_lat = EC_CAPACITY_FACTOR * T // N_EXPERTS
    cap_ctx = EC_CAPACITY_FACTOR * Lc // N_EXPERTS
    for s0, n, cap, poff in ((0, T, cap_lat, 0), (T, Lc, cap_ctx, cap_lat)):
        a = a_ref[0, :, s0:s0 + n]
        bits = lax.bitcast_convert_type(a, jnp.int32)

        def count(mask):
            return jnp.sum(jnp.where(mask, 1.0, 0.0), axis=1, keepdims=True)

        def radix(i, pref):
            cand = pref | jnp.left_shift(jnp.int32(1), 30 - i)
            return jnp.where(count(bits >= cand) >= cap, cand, pref)

        thr = lax.fori_loop(0, 31, radix, jnp.zeros((N_EXPERTS, 1), jnp.int32))
        gt = bits > thr
        eq = bits == thr
        need = cap - count(gt)
        idx = lax.broadcasted_iota(jnp.int32, (N_EXPERTS, n), 1)
        nbits = max(1, (n - 1).bit_length())

        def tie(i, ans):
            cand = ans | jnp.left_shift(jnp.int32(1), nbits - 1 - i)
            return jnp.where(count(eq & (idx < cand)) < need, cand, ans)

        last = lax.fori_loop(0, nbits, tie, jnp.zeros((N_EXPERTS, 1), jnp.int32))
        sel = gt | (eq & (idx <= last))
        selb = jnp.where(sel, 1.0, 0.0).astype(BF16)
        nseg = n // LANES
        si = lax.broadcasted_iota(jnp.int32, (n, LANES), 0) // LANES
        sj = lax.broadcasted_iota(jnp.int32, (n, LANES), 1)
        seg_tot = _dot(selb, (si == sj).astype(BF16))
        seg_off = _dot(seg_tot.astype(BF16), tri)
        for s in range(nseg):
            ls = slice(s * LANES, (s + 1) * LANES)
            within = _dot(selb[:, ls], tri)
            p = (within + seg_off[:, s:s + 1]).astype(jnp.int32) + poff
            pos_ref[0, :, s0 + s * LANES:s0 + (s + 1) * LANES] = jnp.where(sel[:, ls], p, -1)
        wa_ref[0, :, s0:s0 + n] = jnp.where(sel, a, 0.0)


def _select(aff, T):
    B, E, Tt = aff.shape
    blk = pl.BlockSpec((1, E, Tt), lambda b: (b, 0, 0))
    return pl.pallas_call(
        functools.partial(_select_kernel, T=T, Lc=Tt - T),
        grid=(B,),
        in_specs=[blk],
        out_specs=[blk, blk],
        out_shape=[jax.ShapeDtypeStruct((B, E, Tt), jnp.int32), jax.ShapeDtypeStruct((B, E, Tt), F32)],
        compiler_params=_cparams(("parallel",)),
        name="ec_select",
    )(aff)


def _moe_kernel(h_ref, pos_ref, wa_ref, wg_ref, wu_ref, wd_ref, o_ref, xe_ref, *, R, nch):
    @pl.when(pl.program_id(1) == 0)
    def _():
        o_ref[...] = jnp.zeros_like(o_ref)

    slot = lax.broadcasted_iota(jnp.int32, (R, TOK_BLK), 0)
    xe_ref[...] = jnp.zeros_like(xe_ref)

    def gather(c, carry):
        st = pl.multiple_of(c * TOK_BLK, TOK_BLK)
        onehot = jnp.where(pos_ref[c] == slot, 1.0, 0.0).astype(BF16)
        xe_ref[...] += _dot(onehot, h_ref[0, pl.ds(st, TOK_BLK), :])
        return carry

    lax.fori_loop(0, nch, gather, 0)
    xe = xe_ref[...].astype(BF16)
    fstep = 512
    ye = jnp.zeros((R, D_MODEL), F32)
    for f in range(D_EXPERT // fstep):
        fs = slice(f * fstep, (f + 1) * fstep)
        hid = _silu(_dot(xe, wg_ref[:, fs])) * _dot(xe, wu_ref[:, fs])
        ye = ye + _dot(hid.astype(BF16), wd_ref[fs, :])
    ye16 = ye.astype(BF16)

    def scatter(c, carry):
        st = pl.multiple_of(c * TOK_BLK, TOK_BLK)
        weighted = jnp.where(pos_ref[c] == slot, wa_ref[c], 0.0).astype(BF16)
        o_ref[0, pl.ds(st, TOK_BLK), :] += _dg(weighted, ye16, _TN)
        return carry

    lax.fori_loop(0, nch, scatter, 0)


def _moe(h2, pos, waff, wg, wu, wd, T):
    B, Tt, D = h2.shape
    E = N_EXPERTS
    nch = Tt // TOK_BLK
    R = EC_CAPACITY_FACTOR * Tt // N_EXPERTS
    assert R % BF16_SUBLANES == 0
    pos = pos.reshape(B, E, nch, 1, TOK_BLK)
    waff = waff.reshape(B, E, nch, 1, TOK_BLK)
    one = pl.Buffered(1)
    rowspec = pl.BlockSpec((None, None, nch, 1, TOK_BLK), lambda b, e: (b, e, 0, 0, 0))
    wspec = lambda r, c: pl.BlockSpec((None, r, c), lambda b, e: (e, 0, 0))
    return pl.pallas_call(
        functools.partial(_moe_kernel, R=R, nch=nch),
        grid=(B, E),
        in_specs=[
            pl.BlockSpec((1, Tt, D), lambda b, e: (b, 0, 0), pipeline_mode=one),
            rowspec, rowspec,
            wspec(D, D_EXPERT), wspec(D, D_EXPERT), wspec(D_EXPERT, D),
        ],
        out_specs=pl.BlockSpec((1, Tt, D), lambda b, e: (b, 0, 0), pipeline_mode=one),
        out_shape=jax.ShapeDtypeStruct((B, Tt, D), F32),
        scratch_shapes=[pltpu.VMEM((R, D), F32)],
        compiler_params=_cparams(("parallel", "arbitrary"), VMEM_LIMIT_BIG),
        name="ec_moe",
    )(h2, pos, waff, wg, wu, wd)


def _combine_kernel(x_ref, y_ref, mod_ref, o_ref):
    o_ref[0] = x_ref[0] + mod_ref[5:6, :] * y_ref[0]


def _combine(x1, moe, modv, n_lat_blk, n_out_blk):
    B, Tt, D = x1.shape
    tok = pl.BlockSpec((1, TOK_BLK, D), lambda b, t: (b, t, 0))
    return pl.pallas_call(
        _combine_kernel,
        grid=(B, n_out_blk),
        in_specs=[tok, tok, pl.BlockSpec((None, None, 6, D), lambda b, t: (b, t // n_lat_blk, 0, 0))],
        out_specs=tok,
        out_shape=jax.ShapeDtypeStruct((B, n_out_blk * TOK_BLK, D), F32),
        compiler_params=_cparams(("parallel", "parallel")),
        name="residual2",
    )(x1, moe, modv)


def _rope_tables(T, Lc):
    nf = GDN_DK // 4
    inv = ROPE_BASE ** (-jnp.arange(nf, dtype=F32) / nf)
    t = jnp.arange(T)
    ang_r = (t // GRID_W).astype(F32)[:, None] * inv
    ang_c = (t % GRID_W).astype(F32)[:, None] * inv
    cos = jnp.concatenate([jnp.cos(ang_r)] * 2 + [jnp.cos(ang_c)] * 2, axis=-1)
    sin = jnp.concatenate([-jnp.sin(ang_r), jnp.sin(ang_r), -jnp.sin(ang_c), jnp.sin(ang_c)], axis=-1)
    cos = jnp.concatenate([cos, jnp.ones((Lc, GDN_DK), F32)], axis=0)
    sin = jnp.concatenate([sin, jnp.zeros((Lc, GDN_DK), F32)], axis=0)
    return cos, sin


def _reorder_w_in(w_in):
    qa, ka, va, za, aa, ba, ub, qn, kn, vn, gr = jnp.split(
        w_in, [512, 1024, 1536, 2048, 2056, 2064, 2320, 2576, 2832, 3088], axis=-1)
    pad = jnp.zeros(w_in.shape[:-1] + (AB_PAD - 2 * N_DIR * GDN_HEADS,), w_in.dtype)
    return jnp.concatenate([qa, ka, va, za, ub, qn, kn, vn, gr, aa, ba, pad], axis=-1).astype(BF16)


def kernel(x, c, ctx, c_ctx, w_mod, b_mod, norm1_w, norm2_w, w_in, conv_w, a_log, dt_bias, gdn_norm_w,
           na_qn_w, na_kn_w, na_rpb, w_br_a, w_br_b, w_br_c, w_out, w_router, w_e_gate, w_e_up, w_e_down):
    B, T, D = x.shape
    Lc = ctx.shape[1]
    Tt = T + Lc
    assert D == D_MODEL and T % TOK_BLK == 0 and Lc == TOK_BLK and T % Lc == 0
    assert T // GRID_W >= WIN_R
    n_lat_blk = T // TOK_BLK

    w_in_r = _reorder_w_in(w_in)
    bf = lambda a: a.astype(BF16)
    w_mod16, wa16, wb16, wc16, wo16 = bf(w_mod), bf(w_br_a), bf(w_br_b), bf(w_br_c), bf(w_out)
    wg16, wu16, wd16 = bf(w_e_gate), bf(w_e_up), bf(w_e_down)
    wr16 = bf(jnp.swapaxes(w_router, 1, 2))
    pad8 = AB_PAD - N_DIR * GDN_HEADS
    al = jnp.pad(a_log.reshape(DEPTH, 1, -1), ((0, 0), (0, 0), (0, pad8)))
    dtb = jnp.pad(dt_bias.reshape(DEPTH, 1, -1), ((0, 0), (0, 0), (0, pad8)))
    qnw = jnp.tile(na_qn_w, (1, NA_HEADS))[:, None, :]
    knw = jnp.tile(na_kn_w, (1, NA_HEADS))[:, None, :]
    head_of = jnp.arange(NA_W) // NA_DH
    bd = (head_of[:, None] == head_of[None, :]).astype(F32) / NA_DH
    cos_t, sin_t = _rope_tables(T, Lc)
    mats_lat = tuple(bf(m) for m in _dft_mats(T))
    mats_ctx = tuple(bf(m) for m in _dft_mats(Lc))
    cch, sch = _dft_mats(FNET_GROUP_CH)
    eye_g = jnp.eye(FNET_GROUPS, dtype=F32)
    mats_ch = (bf(jnp.kron(eye_g, cch)), bf(jnp.kron(eye_g, sch)))

    rows = -(-(B + 1) // 8) * 8
    cs = jnp.zeros((rows, D), F32).at[:B].set(c).at[B].set(c_ctx)
    mod = _modulation(cs, w_mod16, b_mod[:, None, :])
    mod_lat = mod[:, :B].reshape(DEPTH, B, 1, 6, D)
    mod_ctx = jnp.broadcast_to(mod[:, B].reshape(DEPTH, 1, 1, 6, D), (DEPTH, B, 1, 6, D))
    modv = jnp.concatenate([mod_lat, mod_ctx], axis=2)

    xs = jnp.concatenate([x, ctx], axis=1)
    for i in range(DEPTH):
        last = i == DEPTH - 1
        qkv, z, ub, naqkv, gr, gb = _input_proj(xs, modv[i], norm1_w[i][None], w_in_r[i], al[i], dtb[i],
                                                n_lat_blk)
        qkvp = _gdn_prep(qkv, conv_w[i], cos_t, sin_t, T)
        o_f, o_b = _gdn_scan(qkvp, gb, T)
        fb = _fnet(ub, mats_lat, mats_ctx, mats_ch, T)
        qn, kn = _na_prep(naqkv, qnw[i], knw[i], bd)
        oc = _na(qn, kn, naqkv, _na_bias_table(na_rpb[i]), T)
        x1, h2, aff = _merge(o_f, o_b, z, fb, oc, gr, xs, modv[i], gdn_norm_w[i][None], norm2_w[i][None],
                             wa16[i], wb16[i], wc16[i], wo16[i], wr16[i], n_lat_blk)
        pos, waff = _select(aff, T)
        moe = _moe(h2, pos, waff, wg16[i], wu16[i], wd16[i], T)
        xs = _combine(x1, moe, modv[i], n_lat_blk, n_lat_blk if last else Tt // TOK_BLK)
    return xs
```

```python
import functools
import math

import jax
import jax.numpy as jnp
from jax import lax
from jax.experimental import pallas as pl
from jax.experimental.pallas import tpu as pltpu

D_MODEL = 1024
DEPTH = 4
GRID_W = 64
RMS_EPS = 1e-6
ROPE_BASE = 10000.0

GDN_HEADS = 4
GDN_DK = 128
GDN_DV = 128
GDN_QK_W = GDN_HEADS * GDN_DK
GDN_V_W = GDN_HEADS * GDN_DV
CONV_W = 5
CHUNK = 64
N_DIR = 2

FNET_GROUPS = 4
FNET_GROUP_CH = 64
FNET_W = FNET_GROUPS * FNET_GROUP_CH

NA_HEADS = 4
NA_DH = 64
NA_W = NA_HEADS * NA_DH
WIN_R = 8
WIN_C = 16

N_BRANCH = 3
N_EXPERTS = 16
EC_CAPACITY_FACTOR = 2
D_EXPERT = 1024

LANES = 128
BF16_SUBLANES = 16
VMEM_LIMIT_BIG = 56 * 1024 * 1024

TOK_BLK = 256
QKV_W = 2 * GDN_QK_W + GDN_V_W
NAQKV_W = 3 * NA_W
GATE_W = N_BRANCH * D_MODEL
AB_PAD = LANES
N_IN_PAD = QKV_W + GDN_V_W + FNET_W + NAQKV_W + GATE_W + AB_PAD

F32 = jnp.float32
BF16 = jnp.bfloat16
HI = lax.Precision.HIGHEST
NEG_BIG = -1e30

_NT = (((1,), (1,)), ((), ()))
_TN = (((0,), (0,)), ((), ()))


def _dot(a, b, prec=None):
    return jnp.dot(a, b, preferred_element_type=F32, precision=prec)


def _dg(a, b, dims, prec=None):
    return lax.dot_general(a, b, dims, preferred_element_type=F32, precision=prec)


def _silu(v):
    return v * jax.nn.sigmoid(v)


def _cparams(sem, vmem=None):
    return pltpu.CompilerParams(dimension_semantics=sem, vmem_limit_bytes=vmem)


def _mod_kernel(c_ref, w_ref, b_ref, o_ref):
    s = _silu(c_ref[...]).astype(BF16)
    o_ref[0] = _dot(s, w_ref[0]) + b_ref[0]


def _modulation(cs, w_mod, b_mod):
    L, D, N = w_mod.shape
    R = cs.shape[0]
    tn = 1536
    return pl.pallas_call(
        _mod_kernel,
        grid=(L, N // tn),
        in_specs=[
            pl.BlockSpec((R, D), lambda l, n: (0, 0)),
            pl.BlockSpec((1, D, tn), lambda l, n: (l, 0, n)),
            pl.BlockSpec((1, 1, tn), lambda l, n: (l, 0, n)),
        ],
        out_specs=pl.BlockSpec((1, R, tn), lambda l, n: (l, 0, n)),
        out_shape=jax.ShapeDtypeStruct((L, R, N), F32),
        compiler_params=_cparams(("parallel", "parallel")),
        name="modulation",
    )(cs, w_mod, b_mod)


def _rms_mod(x, nw, shift, scale):
    ms = jnp.mean(x * x, axis=-1, keepdims=True)
    return (x * lax.rsqrt(ms + RMS_EPS) * nw) * (1.0 + scale) + shift


def _proj_kernel(x_ref, mod_ref, nw_ref, w_ref, al_ref, dt_ref,
                 qkv_ref, z_ref, ub_ref, naqkv_ref, gr_ref, gb_ref):
    h = _rms_mod(x_ref[0], nw_ref[...], mod_ref[0:1, :], mod_ref[1:2, :]).astype(BF16)

    def mm(lo, width):
        return _dot(h, w_ref[:, lo:lo + width])

    col = 0
    for ref, width in ((qkv_ref, QKV_W), (z_ref, GDN_V_W), (ub_ref, FNET_W),
                       (naqkv_ref, NAQKV_W), (gr_ref, GATE_W)):
        step = 512 if width % 512 == 0 else 256
        for j in range(width // step):
            ref[0, :, j * step:(j + 1) * step] = mm(col + j * step, step).astype(ref.dtype)
        col += width
    ab = mm(col, AB_PAD)
    sp_in = ab + dt_ref[...]
    softplus = jnp.maximum(sp_in, 0.0) + jnp.log(1.0 + jnp.exp(-jnp.abs(sp_in)))
    g = -jnp.exp(al_ref[...]) * softplus
    lane = lax.broadcasted_iota(jnp.int32, ab.shape, 1)
    gb_ref[0] = jnp.where(lane < N_DIR * GDN_HEADS, g, jax.nn.sigmoid(ab))


def _input_proj(xs, modv, nw, w_in, al, dtb, n_lat_blk):
    B, Tt, D = xs.shape
    nblk = Tt // TOK_BLK
    tok = lambda w: pl.BlockSpec((1, TOK_BLK, w), lambda b, t: (b, t, 0))
    widths = (QKV_W, GDN_V_W, FNET_W, NAQKV_W, GATE_W)
    return pl.pallas_call(
        _proj_kernel,
        grid=(B, nblk),
        in_specs=[
            tok(D),
            pl.BlockSpec((None, None, 6, D), lambda b, t: (b, t // n_lat_blk, 0, 0)),
            pl.BlockSpec((1, D), lambda b, t: (0, 0)),
            pl.BlockSpec((D, N_IN_PAD), lambda b, t: (0, 0)),
            pl.BlockSpec((1, AB_PAD), lambda b, t: (0, 0)),
            pl.BlockSpec((1, AB_PAD), lambda b, t: (0, 0)),
        ],
        out_specs=[tok(w) for w in widths] + [tok(AB_PAD)],
        out_shape=[jax.ShapeDtypeStruct((B, Tt, w), BF16) for w in widths]
        + [jax.ShapeDtypeStruct((B, Tt, AB_PAD), F32)],
        compiler_params=_cparams(("parallel", "parallel"), VMEM_LIMIT_BIG),
        name="input_proj",
    )(xs, modv, nw, w_in, al, dtb)


def _gdn_prep_kernel(u_ref, cw_ref, cos_ref, sin_ref, o_ref, *, T, Tt):
    j = pl.program_id(1)
    w = cw_ref[...]
    scale = jnp.where(j < GDN_HEADS, GDN_DK ** -0.5, 1.0).astype(F32)
    halo = BF16_SUBLANES
    lane = lax.broadcasted_iota(jnp.int32, (TOK_BLK, LANES), 1)
    rowi = lax.broadcasted_iota(jnp.int32, (TOK_BLK, LANES), 0)

    def conv_silu(start):
        ps = pl.multiple_of(jnp.maximum(start - halo, 0), halo)
        ns = pl.multiple_of(jnp.minimum(start + TOK_BLK, Tt - halo), halo)
        ext = jnp.concatenate([
            u_ref[0, pl.ds(ps, halo), :].astype(F32),
            u_ref[0, pl.ds(start, TOK_BLK), :].astype(F32),
            u_ref[0, pl.ds(ns, halo), :].astype(F32)], axis=0)
        row = start + rowi
        seg_lo = jnp.where(start < T, 0, T)
        seg_hi = jnp.where(start < T, T, Tt)
        acc = jnp.zeros((TOK_BLK, LANES), F32)
        for d in range(-(CONV_W // 2), CONV_W // 2 + 1):
            sh = ext[halo + d:halo + d + TOK_BLK, :]
            ok = (row + d >= seg_lo) & (row + d < seg_hi)
            acc = acc + jnp.where(ok, sh, 0.0) * w[d + CONV_W // 2:d + CONV_W // 2 + 1, :]
        return _silu(acc)

    def qk_body(c, carry):
        start = pl.multiple_of(c * TOK_BLK, TOK_BLK)
        y = conv_silu(start)
        yn = y * lax.rsqrt(jnp.sum(y * y, axis=-1, keepdims=True) + RMS_EPS)
        partner = jnp.where((lane & 32) == 0, pltpu.roll(yn, LANES - 32, 1), pltpu.roll(yn, 32, 1))
        rows = pl.ds(start, TOK_BLK)
        o_ref[0, rows, :] = (yn * cos_ref[rows, :] + partner * sin_ref[rows, :]) * scale
        return carry

    def v_body(c, carry):
        start = pl.multiple_of(c * TOK_BLK, TOK_BLK)
        o_ref[0, pl.ds(start, TOK_BLK), :] = conv_silu(start)
        return carry

    @pl.when(j < 2 * GDN_HEADS)
    def _():
        lax.fori_loop(0, Tt // TOK_BLK, qk_body, 0)

    @pl.when(j >= 2 * GDN_HEADS)
    def _():
        lax.fori_loop(0, Tt // TOK_BLK, v_body, 0)


def _gdn_prep(qkv, conv_w, cos_t, sin_t, T):
    B, Tt, W = qkv.shape
    return pl.pallas_call(
        functools.partial(_gdn_prep_kernel, T=T, Tt=Tt),
        grid=(B, W // LANES),
        in_specs=[
            pl.BlockSpec((1, Tt, LANES), lambda b, j: (b, 0, j)),
            pl.BlockSpec((CONV_W, LANES), lambda b, j: (0, j)),
            pl.BlockSpec((Tt, LANES), lambda b, j: (0, 0)),
            pl.BlockSpec((Tt, LANES), lambda b, j: (0, 0)),
        ],
        out_specs=pl.BlockSpec((1, Tt, LANES), lambda b, j: (b, 0, j)),
        out_shape=jax.ShapeDtypeStruct((B, Tt, W), F32),
        compiler_params=_cparams(("parallel", "parallel")),
        name="gdn_prep",
    )(qkv, conv_w, cos_t, sin_t)


N_CHAIN = N_DIR * GDN_HEADS


def _unit_tri_inverse(m, eye, blk16, blk32):
    b = lambda a: a.astype(BF16)
    n1 = -jnp.where(blk16, m, 0.0)
    n1b = b(n1)
    n2b = b(_dot(n1b, n1b))
    n4b = b(_dot(n2b, n2b))
    n8b = b(_dot(n4b, n4b))
    p = eye + n1
    p = p + _dot(b(p), n2b)
    p = p + _dot(b(p), n4b)
    p = p + _dot(b(p), n8b)
    for c in (jnp.where(blk32 & jnp.logical_not(blk16), m, 0.0), jnp.where(blk32, 0.0, m)):
        pb = b(p)
        p = p - _dot(b(_dot(pb, b(c))), pb)
    return p


def _gdn_chunk_kernel(q_ref, k_ref, v_ref, g_ref, wq_ref, kg_ref, u_ref, qkm_ref, gl_ref):
    HB = GDN_HEADS * CHUNK
    ii = lax.broadcasted_iota(jnp.int32, (HB, HB), 0)
    jj = lax.broadcasted_iota(jnp.int32, (HB, HB), 1)
    eye = (ii == jj).astype(F32)
    blk16 = (ii >> 4) == (jj >> 4)
    blk32 = (ii >> 5) == (jj >> 5)
    blk64 = (ii >> 6) == (jj >> 6)
    masks = ((blk64 & (ii >= jj), blk64 & (ii > jj)), (blk64 & (ii <= jj), blk64 & (ii < jj)))
    row = lax.broadcasted_iota(jnp.int32, (CHUNK, LANES), 0)

    def cumsum_rows(x, backward):
        sh = 1
        while sh < CHUNK:
            if backward:
                x = x + jnp.where(row < CHUNK - sh, pltpu.roll(x, CHUNK - sh, 0), 0.0)
            else:
                x = x + jnp.where(row >= sh, pltpu.roll(x, sh, 0), 0.0)
            sh *= 2
        return x

    for j in range(TOK_BLK // CHUNK):
        rows = slice(j * CHUNK, (j + 1) * CHUNK)
        gbv = g_ref[0, rows, :]
        tot_row = jnp.sum(gbv, axis=0, keepdims=True)
        gl_ref[0, rows, :] = jnp.broadcast_to(jnp.exp(tot_row), (CHUNK, LANES))
        stack = lambda ref: jnp.concatenate(
            [ref[0, rows, h * GDN_DK:(h + 1) * GDN_DK] for h in range(GDN_HEADS)], axis=0)
        qst, kst, vst = stack(q_ref), stack(k_ref), stack(v_ref)
        k16 = kst.astype(BF16)
        kk = _dg(k16, k16, _NT)
        qk = _dg(qst.astype(BF16), k16, _NT)
        for d, (le, lt) in enumerate(masks):
            gc_d = cumsum_rows(gbv, d == 1)
            col = lambda a, off: jnp.concatenate(
                [a[:, off + d * GDN_HEADS + h:off + d * GDN_HEADS + h + 1] for h in range(GDN_HEADS)], axis=0)
            gc = col(gc_d, 0)
            beta = col(gbv, N_CHAIN)
            tot = col(jnp.broadcast_to(tot_row, (CHUNK, LANES)), 0)
            gcm = jnp.broadcast_to(gc, (HB, HB))
            dec = jnp.where(le, jnp.exp(jnp.where(le, gcm - gcm.T, 0.0)), 0.0)
            tinv = _unit_tri_inverse(jnp.where(lt, beta * kk * dec, 0.0), eye, blk16, blk32)
            eg = jnp.exp(gc)
            rhs = jnp.concatenate([vst * beta, kst * (beta * eg)], axis=1)
            sol = _dot(tinv.astype(BF16), rhs.astype(BF16))
            qkm = jnp.where(le, qk * dec, 0.0).astype(BF16)
            qg = (qst * eg).astype(BF16)
            kg = (kst * jnp.exp(tot - gc)).astype(BF16)
            for h in range(GDN_HEADS):
                l = d * GDN_HEADS + h
                hr = slice(h * CHUNK, (h + 1) * CHUNK)
                ls = slice(l * GDN_DK, (l + 1) * GDN_DK)
                u_ref[0, rows, ls] = sol[hr, :GDN_DV]
                wq_ref[0, 2 * j * CHUNK:(2 * j + 1) * CHUNK, ls] = sol[hr, GDN_DV:].astype(BF16)
                wq_ref[0, (2 * j + 1) * CHUNK:(2 * j + 2) * CHUNK, ls] = qg[hr]
                kg_ref[0, rows, ls] = kg[hr]
                qkm_ref[0, rows, l * CHUNK:(l + 1) * CHUNK] = qkm[hr, h * CHUNK:(h + 1) * CHUNK]


def _gdn_chunk(qkvp, gb):
    B, Tt, _ = qkvp.shape
    cw = N_CHAIN * GDN_DK
    tok = lambda col, w, r=1: pl.BlockSpec((1, r * TOK_BLK, w), lambda b, t: (b, t, col))
    return pl.pallas_call(
        _gdn_chunk_kernel,
        grid=(B, Tt // TOK_BLK),
        in_specs=[tok(0, GDN_QK_W), tok(1, GDN_QK_W), tok(2, GDN_V_W), tok(0, LANES)],
        out_specs=[tok(0, cw, 2), tok(0, cw), tok(0, cw), tok(0, N_CHAIN * CHUNK), tok(0, LANES)],
        out_shape=[jax.ShapeDtypeStruct((B, 2 * Tt, cw), BF16), jax.ShapeDtypeStruct((B, Tt, cw), BF16),
                   jax.ShapeDtypeStruct((B, Tt, cw), F32), jax.ShapeDtypeStruct((B, Tt, N_CHAIN * CHUNK), BF16),
                   jax.ShapeDtypeStruct((B, Tt, LANES), F32)],
        compiler_params=_cparams(("parallel", "parallel")),
        name="gdn_chunk",
    )(qkvp, qkvp, qkvp, gb)


def _gdn_scan_kernel(wqf_ref, kgf_ref, uf_ref, qkmf_ref, glf_ref, wqb_ref, kgb_ref, ub_ref, qkmb_ref, glb_ref,
                     of_ref, ob_ref, s_ref):
    @pl.when(pl.program_id(1) == 0)
    def _():
        s_ref[...] = jnp.zeros_like(s_ref)

    n_sub = TOK_BLK // CHUNK
    dirs = ((wqf_ref, kgf_ref, uf_ref, qkmf_ref, glf_ref, of_ref), (wqb_ref, kgb_ref, ub_ref, qkmb_ref, glb_ref, ob_ref))
    for step in range(n_sub):
        for d, (wq_ref, kg_ref, u_ref, qkm_ref, gl_ref, o_ref) in enumerate(dirs):
            j = step if d == 0 else n_sub - 1 - step
            rows = slice(j * CHUNK, (j + 1) * CHUNK)
            gl = gl_ref[0, j * CHUNK:j * CHUNK + 1, :]
            for h in range(GDN_HEADS):
                l = d * GDN_HEADS + h
                hs = slice(h * GDN_DK, (h + 1) * GDN_DK)
                s = s_ref[l]
                ws = _dot(wq_ref[0, 2 * j * CHUNK:2 * (j + 1) * CHUNK, hs], s.astype(BF16))
                vn16 = (u_ref[0, rows, hs] - ws[:CHUNK]).astype(BF16)
                o_ref[0, rows, hs] = ws[CHUNK:] + _dot(qkm_ref[0, rows, h * CHUNK:(h + 1) * CHUNK], vn16)
                s_ref[l] = s * gl[:, l:l + 1] + _dg(kg_ref[0, rows, hs], vn16, _TN)


def _gdn_scan(wq, kg, u, qkm, gl, T):
    B, Tt, _ = kg.shape
    n_lat = T // TOK_BLK
    n_all = Tt // TOK_BLK
    n_ctx = n_all - n_lat

    def fwd(c):
        return jnp.where(c < n_ctx, n_lat + c, c - n_ctx)

    def bwd(c):
        return n_all - 1 - c

    hw = GDN_HEADS * GDN_DK

    def specs(order, d):
        blk = lambda r, w, col: pl.BlockSpec((1, r * TOK_BLK, w), lambda b, c: (b, order(c), col))
        return [blk(2, hw, d), blk(1, hw, d), blk(1, hw, d), blk(1, GDN_HEADS * CHUNK, d), blk(1, LANES, 0)]

    return pl.pallas_call(
        _gdn_scan_kernel,
        grid=(B, n_all),
        in_specs=specs(fwd, 0) + specs(bwd, 1),
        out_specs=[pl.BlockSpec((1, TOK_BLK, hw), lambda b, c: (b, fwd(c), 0)),
                   pl.BlockSpec((1, TOK_BLK, hw), lambda b, c: (b, bwd(c), 0))],
        out_shape=[jax.ShapeDtypeStruct((B, Tt, GDN_V_W), F32)] * 2,
        scratch_shapes=[pltpu.VMEM((N_CHAIN, GDN_DK, GDN_DV), F32)],
        compiler_params=_cparams(("parallel", "arbitrary")),
        name="gdn_scan",
    )(wq, kg, u, qkm, gl, wq, kg, u, qkm, gl)


def _fnet_kernel(c_ref, s_ref, u_ref, cc_ref, sc_ref, *rest):
    o_ref = rest[-1]
    u = u_ref[0]
    p = _dot(c_ref[...], u).astype(BF16)
    q = _dot(s_ref[...], u).astype(BF16)
    o_ref[0] = (_dot(p, cc_ref[...]) - _dot(q, sc_ref[...])).astype(o_ref.dtype)


def _dft_mats(n):
    j = jnp.arange(n, dtype=jnp.int32)
    ang = ((j[:, None] * j[None, :]) % n).astype(F32) * (2.0 * math.pi / n)
    sc = 1.0 / math.sqrt(n)
    return jnp.cos(ang) * sc, jnp.sin(ang) * sc


def _fnet(ub, mats_lat, mats_ctx, mats_ch, T):
    B, Tt, W = ub.shape
    Lc = Tt - T
    cc, sc = mats_ch
    tm = min(512, T)
    const = lambda shape: pl.BlockSpec(shape, lambda m, b: (0, 0))
    out = pl.pallas_call(
        _fnet_kernel,
        grid=(T // tm, B),
        in_specs=[
            pl.BlockSpec((tm, T), lambda m, b: (m, 0)),
            pl.BlockSpec((tm, T), lambda m, b: (m, 0)),
            pl.BlockSpec((1, T, W), lambda m, b: (b, 0, 0)),
            const((W, W)), const((W, W)),
        ],
        out_specs=pl.BlockSpec((1, tm, W), lambda m, b: (b, m, 0)),
        out_shape=jax.ShapeDtypeStruct((B, Tt, W), BF16),
        compiler_params=_cparams(("parallel", "parallel"), VMEM_LIMIT_BIG),
        name="fnet_latent",
    )(mats_lat[0], mats_lat[1], ub, cc, sc)
    cblk = T // Lc
    return pl.pallas_call(
        _fnet_kernel,
        grid=(1, B),
        in_specs=[
            const((Lc, Lc)), const((Lc, Lc)),
            pl.BlockSpec((1, Lc, W), lambda m, b: (b, cblk, 0)),
            const((W, W)), const((W, W)),
            pl.BlockSpec(memory_space=pl.ANY),
        ],
        out_specs=pl.BlockSpec((1, Lc, W), lambda m, b: (b, cblk, 0)),
        out_shape=jax.ShapeDtypeStruct((B, Tt, W), BF16),
        input_output_aliases={5: 0},
        compiler_params=_cparams(("parallel", "parallel")),
        name="fnet_context",
    )(mats_ctx[0], mats_ctx[1], ub, cc, sc, out)


def _na_prep_kernel(x_ref, qw_ref, kw_ref, bd_ref, q_ref, k_ref):
    bd = bd_ref[...]

    def nrm(a, w):
        ms = _dot(a * a, bd, HI)
        return a * lax.rsqrt(ms + RMS_EPS) * w

    q = x_ref[0, :, 0:NA_W].astype(F32)
    k = x_ref[0, :, NA_W:2 * NA_W].astype(F32)
    q_ref[0] = (nrm(q, qw_ref[...]) * (NA_DH ** -0.5)).astype(BF16)
    k_ref[0] = nrm(k, kw_ref[...]).astype(BF16)


def _na_prep(naqkv, qw, kw, bd):
    B, Tt, _ = naqkv.shape
    const = lambda shape: pl.BlockSpec(shape, lambda b, t: (0, 0))
    tok = lambda w: pl.BlockSpec((1, TOK_BLK, w), lambda b, t: (b, t, 0))
    return pl.pallas_call(
        _na_prep_kernel,
        grid=(B, Tt // TOK_BLK),
        in_specs=[tok(NAQKV_W), const((1, NA_W)), const((1, NA_W)), const((NA_W, NA_W))],
        out_specs=[tok(NA_W), tok(NA_W)],
        out_shape=[jax.ShapeDtypeStruct((B, Tt, NA_W), BF16)] * 2,
        compiler_params=_cparams(("parallel", "parallel")),
        name="na_prep",
    )(naqkv, qw, kw, bd)


def _attend(q, keys, vals, biases, o_ref, orow):
    for h in range(NA_HEADS):
        hs = slice(h * NA_DH, (h + 1) * NA_DH)
        qh = q[:, hs]
        ss = []
        for kk, bias in zip(keys, biases):
            s = _dg(qh, kk[:, hs], _NT)
            if bias is not None:
                s = s + bias[h]
            ss.append(s)
        m = functools.reduce(jnp.maximum, [jnp.max(s, axis=-1, keepdims=True) for s in ss])
        ps = [jnp.exp(s - m) for s in ss]
        den = sum(jnp.sum(p, axis=-1, keepdims=True) for p in ps)
        o = sum(_dot(p.astype(BF16), vv[:, hs]) for p, vv in zip(ps, vals))
        o_ref[0, orow, hs] = (o / den).astype(o_ref.dtype)


NA_ROWS = 2


def _na_latent_kernel(q_ref, k_ref, v_ref, bias_ref, o_ref, *, T, Lc, rows):
    r0 = pl.program_id(1) * NA_ROWS
    kc = k_ref[0, T:T + Lc, :]
    vc = v_ref[0, T:T + Lc, :]
    nwin = WIN_R * GRID_W
    for i in range(NA_ROWS):
        start = pl.multiple_of(jnp.clip(r0 + i - WIN_R // 2, 0, rows - WIN_R) * GRID_W, GRID_W)
        keys = [k_ref[0, pl.ds(start, nwin), :], kc]
        vals = [v_ref[0, pl.ds(start, nwin), :], vc]
        _attend(q_ref[0, i * GRID_W:(i + 1) * GRID_W, :], keys, vals, [bias_ref.at[i], None], o_ref,
                slice(i * GRID_W, (i + 1) * GRID_W))


def _na_context_kernel(q_ref, k_ref, v_ref, prev_ref, o_ref):
    del prev_ref
    _attend(q_ref[0], [k_ref[0]], [v_ref[0]], [None], o_ref, slice(None))


def _na(qn, kn, naqkv, bias_rows, T):
    B, Tt, _ = qn.shape
    Lc = Tt - T
    rows = T // GRID_W
    nwin = WIN_R * GRID_W
    out = pl.pallas_call(
        functools.partial(_na_latent_kernel, T=T, Lc=Lc, rows=rows),
        grid=(B, rows // NA_ROWS),
        in_specs=[
            pl.BlockSpec((1, NA_ROWS * GRID_W, NA_W), lambda b, r: (b, r, 0)),
            pl.BlockSpec((1, Tt, NA_W), lambda b, r: (b, 0, 0)),
            pl.BlockSpec((1, Tt, NA_W), lambda b, r: (b, 0, 2)),
            pl.BlockSpec((NA_ROWS, NA_HEADS, GRID_W, nwin), lambda b, r: (r, 0, 0, 0)),
        ],
        out_specs=pl.BlockSpec((1, NA_ROWS * GRID_W, NA_W), lambda b, r: (b, r, 0)),
        out_shape=jax.ShapeDtypeStruct((B, Tt, NA_W), BF16),
        compiler_params=_cparams(("parallel", "arbitrary")),
        name="na_latent",
    )(qn, kn, naqkv, bias_rows)
    cblk = T // Lc
    return pl.pallas_call(
        _na_context_kernel,
        grid=(B,),
        in_specs=[
            pl.BlockSpec((1, Lc, NA_W), lambda b: (b, cblk, 0)),
            pl.BlockSpec((1, Lc, NA_W), lambda b: (b, cblk, 0)),
            pl.BlockSpec((1, Lc, NA_W), lambda b: (b, cblk, 2)),
            pl.BlockSpec(memory_space=pl.ANY),
        ],
        out_specs=pl.BlockSpec((1, Lc, NA_W), lambda b: (b, cblk, 0)),
        out_shape=jax.ShapeDtypeStruct((B, Tt, NA_W), BF16),
        input_output_aliases={3: 0},
        compiler_params=_cparams(("parallel",)),
        name="na_context",
    )(qn, kn, naqkv, out)


def _na_bias_rows(rpb, rows):
    r = jnp.arange(rows)
    cl = (r - jnp.clip(r - WIN_R // 2, 0, rows - WIN_R))[:, None]
    a = jnp.arange(WIN_R)[None, :]
    dr = a - cl + (WIN_R - 1)
    qc = jnp.arange(GRID_W)
    c_start = jnp.clip(qc - WIN_C // 2, 0, GRID_W - WIN_C)
    kc = jnp.arange(GRID_W)
    dc = kc[None, :] - qc[:, None] + (WIN_C - 1)
    ok = (kc[None, :] >= c_start[:, None]) & (kc[None, :] < c_start[:, None] + WIN_C)
    tbl = rpb[:, dr][:, :, :, jnp.clip(dc, 0, 2 * WIN_C - 2)]
    tbl = jnp.where(ok[None, None, None], tbl.astype(F32), NEG_BIG)
    return jnp.transpose(tbl, (1, 0, 3, 2, 4)).reshape(rows, NA_HEADS, GRID_W, WIN_R * GRID_W)


def _merge_kernel(of_ref, ob_ref, z_ref, fb_ref, oc_ref, gr_ref, x_ref, mod_ref, gnw_ref, n2w_ref,
                  wa_ref, wb_ref, wc_ref, wo_ref, wr_ref, x1_ref, h2_ref, aff_ref):
    o = of_ref[0] + ob_ref[0]
    z = z_ref[0].astype(F32)
    ya = jnp.zeros((TOK_BLK, D_MODEL), F32)
    for h in range(GDN_HEADS):
        hs = slice(h * GDN_DV, (h + 1) * GDN_DV)
        oh = o[:, hs]
        on = oh * lax.rsqrt(jnp.mean(oh * oh, axis=-1, keepdims=True) + RMS_EPS) * gnw_ref[...]
        ya = ya + _dot((on * _silu(z[:, hs])).astype(BF16), wa_ref[hs, :])
    yb = _dot(fb_ref[0], wb_ref[...])
    yc = _dot(oc_ref[0], wc_ref[...])
    gate = lambda j: jax.nn.sigmoid(gr_ref[0, :, j * D_MODEL:(j + 1) * D_MODEL].astype(F32))
    y = gate(0) * ya + gate(1) * yb + gate(2) * yc
    x1 = x_ref[0] + mod_ref[2:3, :] * _dot(y.astype(BF16), wo_ref[...])
    x1_ref[0] = x1
    h2 = _rms_mod(x1, n2w_ref[...], mod_ref[3:4, :], mod_ref[4:5, :]).astype(BF16)
    h2_ref[0] = h2
    logits = _dg(wr_ref[...], h2, _NT)
    ex = jnp.exp(logits - jnp.max(logits, axis=0, keepdims=True))
    aff_ref[0] = ex / jnp.sum(ex, axis=0, keepdims=True)


def _merge(o_f, o_b, z, fb, oc, gr, xs, modv, gnw, n2w, wa, wb, wc, wo, wr, n_lat_blk):
    B, Tt, D = xs.shape
    tok = lambda w: pl.BlockSpec((1, TOK_BLK, w), lambda b, t: (b, t, 0))
    const = lambda shape: pl.BlockSpec(shape, lambda b, t: (0, 0))
    return pl.pallas_call(
        _merge_kernel,
        grid=(B, Tt // TOK_BLK),
        in_specs=[
            tok(GDN_V_W), tok(GDN_V_W), tok(GDN_V_W), tok(FNET_W), tok(NA_W), tok(GATE_W), tok(D),
            pl.BlockSpec((None, None, 6, D), lambda b, t: (b, t // n_lat_blk, 0, 0)),
            const((1, GDN_DV)), const((1, D)),
            const((GDN_V_W, D)), const((FNET_W, D)), const((NA_W, D)), const((D, D)),
            const((N_EXPERTS, D)),
        ],
        out_specs=[tok(D), tok(D), pl.BlockSpec((1, N_EXPERTS, TOK_BLK), lambda b, t: (b, 0, t))],
        out_shape=[jax.ShapeDtypeStruct((B, Tt, D), F32), jax.ShapeDtypeStruct((B, Tt, D), BF16),
                   jax.ShapeDtypeStruct((B, N_EXPERTS, Tt), F32)],
        compiler_params=_cparams(("parallel", "parallel"), VMEM_LIMIT_BIG),
        name="merge",
    )(o_f, o_b, z, fb, oc, gr, xs, modv, gnw, n2w, wa, wb, wc, wo, wr)


def _select_kernel(a_ref, pos_ref, wa_ref, lo_ref, cnt_ref, *, T, Lc):
    ii = lax.broadcasted_iota(jnp.int32, (LANES, LANES), 0)
    jj = lax.broadcasted_iota(jnp.int32, (LANES, LANES), 1)
    tri = (ii < jj).astype(BF16)
    cap_lat = EC_CAPACITY_FACTOR * T // N_EXPERTS
    cap_ctx = EC_CAPACITY_FACTOR * Lc // N_EXPERTS
    blk_lane = lax.broadcasted_iota(jnp.int32, (N_EXPERTS, LANES), 1)
    lo_all = jnp.zeros((N_EXPERTS, LANES), F32)
    cnt_all = jnp.zeros((N_EXPERTS, LANES), F32)
    for s0, n, cap, poff in ((0, T, cap_lat, 0), (T, Lc, cap_ctx, cap_lat)):
        a = a_ref[0, :, s0:s0 + n]
        bits = lax.bitcast_convert_type(a, jnp.int32)

        def count(mask):
            return jnp.sum(jnp.where(mask, 1.0, 0.0), axis=1, keepdims=True)

        def radix(i, pref):
            cand = pref | jnp.left_shift(jnp.int32(1), 30 - i)
            return jnp.where(count(bits >= cand) >= cap, cand, pref)

        thr = lax.fori_loop(0, 31, radix, jnp.zeros((N_EXPERTS, 1), jnp.int32))
        gt = bits > thr
        eq = bits == thr
        need = cap - count(gt)
        idx = lax.broadcasted_iota(jnp.int32, (N_EXPERTS, n), 1)
        nbits = max(1, (n - 1).bit_length())

        def tie(i, ans):
            cand = ans | jnp.left_shift(jnp.int32(1), nbits - 1 - i)
            return jnp.where(count(eq & (idx < cand)) < need, cand, ans)

        last = lax.fori_loop(0, nbits, tie, jnp.zeros((N_EXPERTS, 1), jnp.int32))
        sel = gt | (eq & (idx <= last))
        selb = jnp.where(sel, 1.0, 0.0).astype(BF16)
        ti = lax.broadcasted_iota(jnp.int32, (n, LANES), 0)
        tj = lax.broadcasted_iota(jnp.int32, (n, LANES), 1)
        seg_tot = _dot(selb, (ti // LANES == tj).astype(BF16))
        seg_off = _dot(seg_tot.astype(BF16), tri)
        for s in range(n // LANES):
            ls = slice(s * LANES, (s + 1) * LANES)
            within = _dot(selb[:, ls], tri)
            p = (within + seg_off[:, s:s + 1]).astype(jnp.int32) + poff
            pos_ref[0, :, s0 + s * LANES:s0 + (s + 1) * LANES] = jnp.where(sel[:, ls], p, -1)
        wa_ref[0, :, s0:s0 + n] = jnp.where(sel, a, 0.0)
        blk_tot = _dot(selb, ((ti + s0) // TOK_BLK == tj).astype(BF16))
        blk_off = _dot(blk_tot.astype(BF16), tri) + poff
        mine = (blk_lane >= s0 // TOK_BLK) & (blk_lane < (s0 + n) // TOK_BLK)
        lo_all = jnp.where(mine, blk_off, lo_all)
        cnt_all = jnp.where(mine, blk_tot, cnt_all)
    lo_ref[0] = lo_all.astype(jnp.int32)
    cnt_ref[0] = cnt_all.astype(jnp.int32)


def _select(aff, T):
    B, E, Tt = aff.shape
    blk = pl.BlockSpec((1, E, Tt), lambda b: (b, 0, 0))
    meta = pl.BlockSpec((1, E, LANES), lambda b: (b, 0, 0))
    return pl.pallas_call(
        functools.partial(_select_kernel, T=T, Lc=Tt - T),
        grid=(B,),
        in_specs=[blk],
        out_specs=[blk, blk, meta, meta],
        out_shape=[jax.ShapeDtypeStruct((B, E, Tt), jnp.int32), jax.ShapeDtypeStruct((B, E, Tt), F32),
                   jax.ShapeDtypeStruct((B, E, LANES), jnp.int32), jax.ShapeDtypeStruct((B, E, LANES), jnp.int32)],
        compiler_params=_cparams(("parallel",)),
        name="ec_select",
    )(aff)


MOE_WIN = 128


def _moe_kernel(lo_ref, cnt_ref, h_ref, pos_ref, wa_ref, wg_ref, wu_ref, wd_ref, o_ref, xe_ref, ye_ref, *, R, nch):
    base = (pl.program_id(0) * N_EXPERTS + pl.program_id(1)) * nch

    @pl.when(pl.program_id(1) == 0)
    def _():
        o_ref[...] = jnp.zeros_like(o_ref)

    slot0 = lax.broadcasted_iota(jnp.int32, (MOE_WIN, TOK_BLK), 0)
    xe_ref[...] = jnp.zeros_like(xe_ref)

    def for_windows(c, fn):
        lo = lo_ref[base + c]
        cnt = cnt_ref[base + c]
        w0 = (lo // BF16_SUBLANES) * BF16_SUBLANES
        nwin = jnp.where(cnt > 0, (lo + cnt - w0 + MOE_WIN - 1) // MOE_WIN, 0)
        st = pl.multiple_of(c * TOK_BLK, TOK_BLK)

        def win(i, carry):
            r0 = pl.multiple_of(w0 + i * MOE_WIN, BF16_SUBLANES)
            fn(st, r0, pos_ref[c] == slot0 + r0)
            return carry

        lax.fori_loop(0, nwin, win, 0)

    def gather(c, carry):
        def fn(st, r0, hit):
            onehot = jnp.where(hit, 1.0, 0.0).astype(BF16)
            xe_ref[pl.ds(r0, MOE_WIN), :] += _dot(onehot, h_ref[0, pl.ds(st, TOK_BLK), :])
        for_windows(c, fn)
        return carry

    lax.fori_loop(0, nch, gather, 0)
    xe = xe_ref[0:R, :].astype(BF16)
    fstep = 512
    ye = jnp.zeros((R, D_MODEL), F32)
    for f in range(D_EXPERT // fstep):
        fs = slice(f * fstep, (f + 1) * fstep)
        hid = _silu(_dot(xe, wg_ref[:, fs])) * _dot(xe, wu_ref[:, fs])
        ye = ye + _dot(hid.astype(BF16), wd_ref[fs, :])
    ye_ref[0:R, :] = ye.astype(BF16)
    ye_ref[R:R + MOE_WIN, :] = jnp.zeros((MOE_WIN, D_MODEL), BF16)

    def scatter(c, carry):
        def fn(st, r0, hit):
            weighted = jnp.where(hit, wa_ref[c], 0.0).astype(BF16)
            o_ref[0, pl.ds(st, TOK_BLK), :] += _dg(weighted, ye_ref[pl.ds(r0, MOE_WIN), :], _TN)
        for_windows(c, fn)
        return carry

    lax.fori_loop(0, nch, scatter, 0)


def _moe(h2, pos, waff, lo, cnt, wg, wu, wd):
    B, Tt, D = h2.shape
    E = N_EXPERTS
    nch = Tt // TOK_BLK
    R = EC_CAPACITY_FACTOR * Tt // N_EXPERTS
    assert R % BF16_SUBLANES == 0
    pos = pos.reshape(B, E, nch, 1, TOK_BLK)
    waff = waff.reshape(B, E, nch, 1, TOK_BLK)
    lo = lo[:, :, :nch].reshape(-1)
    cnt = cnt[:, :, :nch].reshape(-1)
    one = pl.Buffered(1)
    rowspec = pl.BlockSpec((None, None, nch, 1, TOK_BLK), lambda b, e, *_: (b, e, 0, 0, 0))
    wspec = lambda r, c: pl.BlockSpec((None, r, c), lambda b, e, *_: (e, 0, 0))
    return pl.pallas_call(
        functools.partial(_moe_kernel, R=R, nch=nch),
        grid_spec=pltpu.PrefetchScalarGridSpec(
            num_scalar_prefetch=2,
            grid=(B, E),
            in_specs=[
                pl.BlockSpec((1, Tt, D), lambda b, e, *_: (b, 0, 0), pipeline_mode=one),
                rowspec, rowspec,
                wspec(D, D_EXPERT), wspec(D, D_EXPERT), wspec(D_EXPERT, D),
            ],
            out_specs=pl.BlockSpec((1, Tt, D), lambda b, e, *_: (b, 0, 0), pipeline_mode=one),
            scratch_shapes=[pltpu.VMEM((R + MOE_WIN, D), F32), pltpu.VMEM((R + MOE_WIN, D), BF16)],
        ),
        out_shape=jax.ShapeDtypeStruct((B, Tt, D), F32),
        compiler_params=_cparams(("parallel", "arbitrary"), VMEM_LIMIT_BIG),
        name="ec_moe",
    )(lo, cnt, h2, pos, waff, wg, wu, wd)


def _combine_kernel(x_ref, y_ref, mod_ref, o_ref):
    o_ref[0] = x_ref[0] + mod_ref[5:6, :] * y_ref[0]


def _combine(x1, moe, modv, n_lat_blk, n_out_blk):
    B, Tt, D = x1.shape
    tok = pl.BlockSpec((1, TOK_BLK, D), lambda b, t: (b, t, 0))
    return pl.pallas_call(
        _combine_kernel,
        grid=(B, n_out_blk),
        in_specs=[tok, tok, pl.BlockSpec((None, None, 6, D), lambda b, t: (b, t // n_lat_blk, 0, 0))],
        out_specs=tok,
        out_shape=jax.ShapeDtypeStruct((B, n_out_blk * TOK_BLK, D), F32),
        compiler_params=_cparams(("parallel", "parallel")),
        name="residual2",
    )(x1, moe, modv)


def _rope_tables(T, Lc):
    nf = GDN_DK // 4
    inv = ROPE_BASE ** (-jnp.arange(nf, dtype=F32) / nf)
    t = jnp.arange(T)
    ang_r = (t // GRID_W).astype(F32)[:, None] * inv
    ang_c = (t % GRID_W).astype(F32)[:, None] * inv
    cos = jnp.concatenate([jnp.cos(ang_r)] * 2 + [jnp.cos(ang_c)] * 2, axis=-1)
    sin = jnp.concatenate([-jnp.sin(ang_r), jnp.sin(ang_r), -jnp.sin(ang_c), jnp.sin(ang_c)], axis=-1)
    cos = jnp.concatenate([cos, jnp.ones((Lc, GDN_DK), F32)], axis=0)
    sin = jnp.concatenate([sin, jnp.zeros((Lc, GDN_DK), F32)], axis=0)
    return cos, sin


def _reorder_w_in(w_in):
    qa, ka, va, za, aa, ba, ub, qn, kn, vn, gr = jnp.split(
        w_in, [512, 1024, 1536, 2048, 2056, 2064, 2320, 2576, 2832, 3088], axis=-1)
    pad = jnp.zeros(w_in.shape[:-1] + (AB_PAD - 2 * N_DIR * GDN_HEADS,), w_in.dtype)
    return jnp.concatenate([qa, ka, va, za, ub, qn, kn, vn, gr, aa, ba, pad], axis=-1).astype(BF16)


def kernel(x, c, ctx, c_ctx, w_mod, b_mod, norm1_w, norm2_w, w_in, conv_w, a_log, dt_bias, gdn_norm_w,
           na_qn_w, na_kn_w, na_rpb, w_br_a, w_br_b, w_br_c, w_out, w_router, w_e_gate, w_e_up, w_e_down):
    B, T, D = x.shape
    Lc = ctx.shape[1]
    Tt = T + Lc
    assert D == D_MODEL and T % TOK_BLK == 0 and Lc == TOK_BLK and T % Lc == 0
    assert T // GRID_W >= WIN_R and (T // GRID_W) % NA_ROWS == 0
    n_lat_blk = T // TOK_BLK

    w_in_r = _reorder_w_in(w_in)
    bf = lambda a: a.astype(BF16)
    w_mod16, wa16, wb16, wc16, wo16 = bf(w_mod), bf(w_br_a), bf(w_br_b), bf(w_br_c), bf(w_out)
    wg16, wu16, wd16 = bf(w_e_gate), bf(w_e_up), bf(w_e_down)
    wr16 = bf(jnp.swapaxes(w_router, 1, 2))
    pad8 = AB_PAD - N_DIR * GDN_HEADS
    al = jnp.pad(a_log.reshape(DEPTH, 1, -1), ((0, 0), (0, 0), (0, pad8)))
    dtb = jnp.pad(dt_bias.reshape(DEPTH, 1, -1), ((0, 0), (0, 0), (0, pad8)))
    qnw = jnp.tile(na_qn_w, (1, NA_HEADS))[:, None, :]
    knw = jnp.tile(na_kn_w, (1, NA_HEADS))[:, None, :]
    head_of = jnp.arange(NA_W) // NA_DH
    bd = (head_of[:, None] == head_of[None, :]).astype(F32) / NA_DH
    cos_t, sin_t = _rope_tables(T, Lc)
    mats_lat = tuple(bf(m) for m in _dft_mats(T))
    mats_ctx = tuple(bf(m) for m in _dft_mats(Lc))
    cch, sch = _dft_mats(FNET_GROUP_CH)
    eye_g = jnp.eye(FNET_GROUPS, dtype=F32)
    mats_ch = (bf(jnp.kron(eye_g, cch)), bf(jnp.kron(eye_g, sch)))

    rows = -(-(B + 1) // 8) * 8
    cs = jnp.zeros((rows, D), F32).at[:B].set(c).at[B].set(c_ctx)
    mod = _modulation(cs, w_mod16, b_mod[:, None, :])
    mod_lat = mod[:, :B].reshape(DEPTH, B, 1, 6, D)
    mod_ctx = jnp.broadcast_to(mod[:, B].reshape(DEPTH, 1, 1, 6, D), (DEPTH, B, 1, 6, D))
    modv = jnp.concatenate([mod_lat, mod_ctx], axis=2)

    xs = jnp.concatenate([x, ctx], axis=1)
    for i in range(DEPTH):
        last = i == DEPTH - 1
        qkv, z, ub, naqkv, gr, gb = _input_proj(xs, modv[i], norm1_w[i][None], w_in_r[i], al[i], dtb[i],
                                                n_lat_blk)
        qkvp = _gdn_prep(qkv, conv_w[i], cos_t, sin_t, T)
        o_f, o_b = _gdn_scan(*_gdn_chunk(qkvp, gb), T)
        fb = _fnet(ub, mats_lat, mats_ctx, mats_ch, T)
        qn, kn = _na_prep(naqkv, qnw[i], knw[i], bd)
        oc = _na(qn, kn, naqkv, _na_bias_rows(na_rpb[i], T // GRID_W), T)
        x1, h2, aff = _merge(o_f, o_b, z, fb, oc, gr, xs, modv[i], gdn_norm_w[i][None], norm2_w[i][None],
                             wa16[i], wb16[i], wc16[i], wo16[i], wr16[i], n_lat_blk)
        pos, waff, lo, cnt = _select(aff, T)
        moe = _moe(h2, pos, waff, lo, cnt, wg16[i], wu16[i], wd16[i])
        xs = _combine(x1, moe, modv[i], n_lat_blk, n_lat_blk if last else Tt // TOK_BLK)
    return xs
```

```python
import functools
import math

import jax
import jax.numpy as jnp
from jax import lax
from jax.experimental import pallas as pl
from jax.experimental.pallas import tpu as pltpu

D_MODEL = 1024
DEPTH = 4
GRID_W = 64
RMS_EPS = 1e-6
ROPE_BASE = 10000.0

GDN_HEADS = 4
GDN_DK = 128
GDN_DV = 128
GDN_QK_W = GDN_HEADS * GDN_DK
GDN_V_W = GDN_HEADS * GDN_DV
CONV_W = 5
CHUNK = 64
N_DIR = 2

FNET_GROUPS = 4
FNET_GROUP_CH = 64
FNET_W = FNET_GROUPS * FNET_GROUP_CH

NA_HEADS = 4
NA_DH = 64
NA_W = NA_HEADS * NA_DH
WIN_R = 8
WIN_C = 16

N_BRANCH = 3
N_EXPERTS = 16
EC_CAPACITY_FACTOR = 2
D_EXPERT = 1024

LANES = 128
BF16_SUBLANES = 16
VMEM_LIMIT_BIG = 56 * 1024 * 1024

TOK_BLK = 256
QKV_W = 2 * GDN_QK_W + GDN_V_W
NAQKV_W = 3 * NA_W
GATE_W = N_BRANCH * D_MODEL
AB_PAD = LANES
N_IN_PAD = QKV_W + GDN_V_W + FNET_W + NAQKV_W + GATE_W + AB_PAD

F32 = jnp.float32
BF16 = jnp.bfloat16
HI = lax.Precision.HIGHEST
NEG_BIG = -1e30

_NT = (((1,), (1,)), ((), ()))
_TN = (((0,), (0,)), ((), ()))


def _dot(a, b, prec=None):
    return jnp.dot(a, b, preferred_element_type=F32, precision=prec)


def _dg(a, b, dims, prec=None):
    return lax.dot_general(a, b, dims, preferred_element_type=F32, precision=prec)


def _silu(v):
    return v * jax.nn.sigmoid(v)


def _cparams(sem, vmem=None):
    return pltpu.CompilerParams(dimension_semantics=sem, vmem_limit_bytes=vmem)


def _mod_kernel(c_ref, w_ref, b_ref, o_ref):
    s = _silu(c_ref[...]).astype(BF16)
    o_ref[0] = _dot(s, w_ref[0]) + b_ref[0]


def _modulation(cs, w_mod, b_mod):
    L, D, N = w_mod.shape
    R = cs.shape[0]
    tn = 1536
    return pl.pallas_call(
        _mod_kernel,
        grid=(L, N // tn),
        in_specs=[
            pl.BlockSpec((R, D), lambda l, n: (0, 0)),
            pl.BlockSpec((1, D, tn), lambda l, n: (l, 0, n)),
            pl.BlockSpec((1, 1, tn), lambda l, n: (l, 0, n)),
        ],
        out_specs=pl.BlockSpec((1, R, tn), lambda l, n: (l, 0, n)),
        out_shape=jax.ShapeDtypeStruct((L, R, N), F32),
        compiler_params=_cparams(("parallel", "parallel")),
        name="modulation",
    )(cs, w_mod, b_mod)


def _rms_mod(x, nw, shift, scale):
    ms = jnp.mean(x * x, axis=-1, keepdims=True)
    return (x * lax.rsqrt(ms + RMS_EPS) * nw) * (1.0 + scale) + shift


def _proj_kernel(x_ref, mod_ref, nw_ref, w_ref, al_ref, dt_ref,
                 qkv_ref, z_ref, ub_ref, naqkv_ref, gr_ref, gb_ref):
    h = _rms_mod(x_ref[0], nw_ref[...], mod_ref[0:1, :], mod_ref[1:2, :]).astype(BF16)

    def mm(lo, width):
        return _dot(h, w_ref[:, lo:lo + width])

    col = 0
    for ref, width in ((qkv_ref, QKV_W), (z_ref, GDN_V_W), (ub_ref, FNET_W),
                       (naqkv_ref, NAQKV_W), (gr_ref, GATE_W)):
        step = 512 if width % 512 == 0 else 256
        for j in range(width // step):
            ref[0, :, j * step:(j + 1) * step] = mm(col + j * step, step).astype(ref.dtype)
        col += width
    ab = mm(col, AB_PAD)
    sp_in = ab + dt_ref[...]
    softplus = jnp.maximum(sp_in, 0.0) + jnp.log(1.0 + jnp.exp(-jnp.abs(sp_in)))
    g = -jnp.exp(al_ref[...]) * softplus
    lane = lax.broadcasted_iota(jnp.int32, ab.shape, 1)
    gb_ref[0] = jnp.where(lane < N_DIR * GDN_HEADS, g, jax.nn.sigmoid(ab))


def _input_proj(xs, modv, nw, w_in, al, dtb, n_lat_blk):
    B, Tt, D = xs.shape
    nblk = Tt // TOK_BLK
    tok = lambda w: pl.BlockSpec((1, TOK_BLK, w), lambda b, t: (b, t, 0))
    widths = (QKV_W, GDN_V_W, FNET_W, NAQKV_W, GATE_W)
    return pl.pallas_call(
        _proj_kernel,
        grid=(B, nblk),
        in_specs=[
            tok(D),
            pl.BlockSpec((None, None, 6, D), lambda b, t: (b, t // n_lat_blk, 0, 0)),
            pl.BlockSpec((1, D), lambda b, t: (0, 0)),
            pl.BlockSpec((D, N_IN_PAD), lambda b, t: (0, 0)),
            pl.BlockSpec((1, AB_PAD), lambda b, t: (0, 0)),
            pl.BlockSpec((1, AB_PAD), lambda b, t: (0, 0)),
        ],
        out_specs=[tok(w) for w in widths] + [tok(AB_PAD)],
        out_shape=[jax.ShapeDtypeStruct((B, Tt, w), BF16) for w in widths]
        + [jax.ShapeDtypeStruct((B, Tt, AB_PAD), F32)],
        compiler_params=_cparams(("parallel", "parallel"), VMEM_LIMIT_BIG),
        name="input_proj",
    )(xs, modv, nw, w_in, al, dtb)


def _gdn_prep_kernel(u_ref, cw_ref, cos_ref, sin_ref, o_ref, *, T, Tt):
    j = pl.program_id(1)
    w = cw_ref[...]
    scale = jnp.where(j < GDN_HEADS, GDN_DK ** -0.5, 1.0).astype(F32)
    halo = BF16_SUBLANES
    lane = lax.broadcasted_iota(jnp.int32, (TOK_BLK, LANES), 1)
    exti = lax.broadcasted_iota(jnp.int32, (TOK_BLK + 2 * halo, LANES), 0)

    def conv_silu(start):
        ps = pl.multiple_of(jnp.maximum(start - halo, 0), halo)
        ns = pl.multiple_of(jnp.minimum(start + TOK_BLK, Tt - halo), halo)
        ext = jnp.concatenate([
            u_ref[0, pl.ds(ps, halo), :].astype(F32),
            u_ref[0, pl.ds(start, TOK_BLK), :].astype(F32),
            u_ref[0, pl.ds(ns, halo), :].astype(F32)], axis=0)
        erow = start - halo + exti
        seg_lo = jnp.where(start < T, 0, T)
        seg_hi = jnp.where(start < T, T, Tt)
        ext = jnp.where((erow >= seg_lo) & (erow < seg_hi), ext, 0.0)
        acc = jnp.zeros((TOK_BLK, LANES), F32)
        for d in range(-(CONV_W // 2), CONV_W // 2 + 1):
            acc = acc + ext[halo + d:halo + d + TOK_BLK, :] * w[d + CONV_W // 2:d + CONV_W // 2 + 1, :]
        return _silu(acc)

    def qk_body(c, carry):
        start = pl.multiple_of(c * TOK_BLK, TOK_BLK)
        y = conv_silu(start)
        yn = y * lax.rsqrt(jnp.sum(y * y, axis=-1, keepdims=True) + RMS_EPS)
        partner = jnp.where((lane & 32) == 0, pltpu.roll(yn, LANES - 32, 1), pltpu.roll(yn, 32, 1))
        rows = pl.ds(start, TOK_BLK)
        o_ref[0, rows, :] = (yn * cos_ref[rows, :] + partner * sin_ref[rows, :]) * scale
        return carry

    def v_body(c, carry):
        start = pl.multiple_of(c * TOK_BLK, TOK_BLK)
        o_ref[0, pl.ds(start, TOK_BLK), :] = conv_silu(start)
        return carry

    @pl.when(j < 2 * GDN_HEADS)
    def _():
        lax.fori_loop(0, Tt // TOK_BLK, qk_body, 0)

    @pl.when(j >= 2 * GDN_HEADS)
    def _():
        lax.fori_loop(0, Tt // TOK_BLK, v_body, 0)


def _gdn_prep(qkv, conv_w, cos_t, sin_t, T):
    B, Tt, W = qkv.shape
    return pl.pallas_call(
        functools.partial(_gdn_prep_kernel, T=T, Tt=Tt),
        grid=(B, W // LANES),
        in_specs=[
            pl.BlockSpec((1, Tt, LANES), lambda b, j: (b, 0, j)),
            pl.BlockSpec((CONV_W, LANES), lambda b, j: (0, j)),
            pl.BlockSpec((Tt, LANES), lambda b, j: (0, 0)),
            pl.BlockSpec((Tt, LANES), lambda b, j: (0, 0)),
        ],
        out_specs=pl.BlockSpec((1, Tt, LANES), lambda b, j: (b, 0, j)),
        out_shape=jax.ShapeDtypeStruct((B, Tt, W), F32),
        compiler_params=_cparams(("parallel", "parallel")),
        name="gdn_prep",
    )(qkv, conv_w, cos_t, sin_t)


N_CHAIN = N_DIR * GDN_HEADS


def _unit_tri_inverse(m, eye, blk16, blk32):
    b = lambda a: a.astype(BF16)
    n1 = -jnp.where(blk16, m, 0.0)
    n1b = b(n1)
    n2b = b(_dot(n1b, n1b))
    n4b = b(_dot(n2b, n2b))
    n8b = b(_dot(n4b, n4b))
    p = eye + n1
    p = p + _dot(b(p), n2b)
    p = p + _dot(b(p), n4b)
    p = p + _dot(b(p), n8b)
    for c in (jnp.where(blk32 & jnp.logical_not(blk16), m, 0.0), jnp.where(blk32, 0.0, m)):
        pb = b(p)
        p = p - _dot(b(_dot(pb, b(c))), pb)
    return p


def _gdn_chunk_kernel(q_ref, k_ref, v_ref, g_ref, wq_ref, kg_ref, u_ref, qkm_ref, gl_ref):
    HB = GDN_HEADS * CHUNK
    ii = lax.broadcasted_iota(jnp.int32, (HB, HB), 0)
    jj = lax.broadcasted_iota(jnp.int32, (HB, HB), 1)
    eye = (ii == jj).astype(F32)
    blk16 = (ii >> 4) == (jj >> 4)
    blk32 = (ii >> 5) == (jj >> 5)
    blk64 = (ii >> 6) == (jj >> 6)
    masks = ((blk64 & (ii >= jj), blk64 & (ii > jj)), (blk64 & (ii <= jj), blk64 & (ii < jj)))
    row = lax.broadcasted_iota(jnp.int32, (CHUNK, LANES), 0)

    def cumsum_rows(x, backward):
        sh = 1
        while sh < CHUNK:
            if backward:
                x = x + jnp.where(row < CHUNK - sh, pltpu.roll(x, CHUNK - sh, 0), 0.0)
            else:
                x = x + jnp.where(row >= sh, pltpu.roll(x, sh, 0), 0.0)
            sh *= 2
        return x

    for j in range(TOK_BLK // CHUNK):
        rows = slice(j * CHUNK, (j + 1) * CHUNK)
        gbv = g_ref[0, rows, :]
        tot_row = jnp.sum(gbv, axis=0, keepdims=True)
        gl_ref[0, rows, :] = jnp.broadcast_to(jnp.exp(tot_row), (CHUNK, LANES))
        stack = lambda ref: jnp.concatenate(
            [ref[0, rows, h * GDN_DK:(h + 1) * GDN_DK] for h in range(GDN_HEADS)], axis=0)
        qst, kst, vst = stack(q_ref), stack(k_ref), stack(v_ref)
        k16 = kst.astype(BF16)
        kk = _dg(k16, k16, _NT)
        qk = _dg(qst.astype(BF16), k16, _NT)
        for d, (le, lt) in enumerate(masks):
            gc_d = cumsum_rows(gbv, d == 1)
            col = lambda a, off: jnp.concatenate(
                [a[:, off + d * GDN_HEADS + h:off + d * GDN_HEADS + h + 1] for h in range(GDN_HEADS)], axis=0)
            gc = col(gc_d, 0)
            beta = col(gbv, N_CHAIN)
            tot = col(jnp.broadcast_to(tot_row, (CHUNK, LANES)), 0)
            gcm = jnp.broadcast_to(gc, (HB, HB))
            dec = jnp.where(le, jnp.exp(jnp.where(le, gcm - gcm.T, 0.0)), 0.0)
            tinv = _unit_tri_inverse(jnp.where(lt, beta * kk * dec, 0.0), eye, blk16, blk32)
            eg = jnp.exp(gc)
            rhs = jnp.concatenate([vst * beta, kst * (beta * eg)], axis=1)
            sol = _dot(tinv.astype(BF16), rhs.astype(BF16))
            qkm = jnp.where(le, qk * dec, 0.0).astype(BF16)
            qg = (qst * eg).astype(BF16)
            kg = (kst * jnp.exp(tot - gc)).astype(BF16)
            for h in range(GDN_HEADS):
                l = d * GDN_HEADS + h
                hr = slice(h * CHUNK, (h + 1) * CHUNK)
                ls = slice(l * GDN_DK, (l + 1) * GDN_DK)
                u_ref[0, rows, ls] = sol[hr, :GDN_DV]
                wq_ref[0, 2 * j * CHUNK:(2 * j + 1) * CHUNK, ls] = sol[hr, GDN_DV:].astype(BF16)
                wq_ref[0, (2 * j + 1) * CHUNK:(2 * j + 2) * CHUNK, ls] = qg[hr]
                kg_ref[0, rows, ls] = kg[hr]
                qkm_ref[0, rows, l * CHUNK:(l + 1) * CHUNK] = qkm[hr, h * CHUNK:(h + 1) * CHUNK]


def _gdn_chunk(qkvp, gb):
    B, Tt, _ = qkvp.shape
    cw = N_CHAIN * GDN_DK
    tok = lambda col, w, r=1: pl.BlockSpec((1, r * TOK_BLK, w), lambda b, t: (b, t, col))
    return pl.pallas_call(
        _gdn_chunk_kernel,
        grid=(B, Tt // TOK_BLK),
        in_specs=[tok(0, GDN_QK_W), tok(1, GDN_QK_W), tok(2, GDN_V_W), tok(0, LANES)],
        out_specs=[tok(0, cw, 2), tok(0, cw), tok(0, cw), tok(0, N_CHAIN * CHUNK), tok(0, LANES)],
        out_shape=[jax.ShapeDtypeStruct((B, 2 * Tt, cw), BF16), jax.ShapeDtypeStruct((B, Tt, cw), BF16),
                   jax.ShapeDtypeStruct((B, Tt, cw), F32), jax.ShapeDtypeStruct((B, Tt, N_CHAIN * CHUNK), BF16),
                   jax.ShapeDtypeStruct((B, Tt, LANES), F32)],
        compiler_params=_cparams(("parallel", "parallel")),
        name="gdn_chunk",
    )(qkvp, qkvp, qkvp, gb)


def _gdn_scan_kernel(wqf_ref, kgf_ref, uf_ref, qkmf_ref, glf_ref, wqb_ref, kgb_ref, ub_ref, qkmb_ref, glb_ref,
                     of_ref, ob_ref, s_ref):
    @pl.when(pl.program_id(1) == 0)
    def _():
        s_ref[...] = jnp.zeros_like(s_ref)

    n_sub = TOK_BLK // CHUNK
    pw = 2 * GDN_DK
    same_head = (lax.broadcasted_iota(jnp.int32, (pw, pw), 0) >> 7) == (lax.broadcasted_iota(jnp.int32, (pw, pw), 1) >> 7)
    first = lax.broadcasted_iota(jnp.int32, (CHUNK, pw), 1) < GDN_DV
    first_row = first[0:1, :]
    dirs = ((wqf_ref, kgf_ref, uf_ref, qkmf_ref, glf_ref, of_ref), (wqb_ref, kgb_ref, ub_ref, qkmb_ref, glb_ref, ob_ref))
    for step in range(n_sub):
        for d, (wq_ref, kg_ref, u_ref, qkm_ref, gl_ref, o_ref) in enumerate(dirs):
            j = step if d == 0 else n_sub - 1 - step
            rows = slice(j * CHUNK, (j + 1) * CHUNK)
            gl = gl_ref[0, j * CHUNK:j * CHUNK + 1, :]
            for p in range(GDN_HEADS // 2):
                l0 = d * GDN_HEADS + 2 * p
                ps = slice(p * pw, (p + 1) * pw)
                s = s_ref[d * (GDN_HEADS // 2) + p]
                ws = _dot(wq_ref[0, 2 * j * CHUNK:2 * (j + 1) * CHUNK, ps], s.astype(BF16))
                vn = u_ref[0, rows, ps] - ws[:CHUNK]
                vn16 = vn.astype(BF16)
                vn_bd = jnp.concatenate([jnp.where(first, vn, 0.0), jnp.where(first, 0.0, vn)], axis=0).astype(BF16)
                o_ref[0, rows, ps] = ws[CHUNK:] + _dot(qkm_ref[0, rows, p * 2 * CHUNK:(p + 1) * 2 * CHUNK], vn_bd)
                glv = jnp.where(first_row, gl[:, l0:l0 + 1], gl[:, l0 + 1:l0 + 2])
                s_ref[d * (GDN_HEADS // 2) + p] = s * glv + jnp.where(same_head, _dg(kg_ref[0, rows, ps], vn16, _TN), 0.0)


def _gdn_scan(wq, kg, u, qkm, gl, T):
    B, Tt, _ = kg.shape
    n_lat = T // TOK_BLK
    n_all = Tt // TOK_BLK
    n_ctx = n_all - n_lat

    def fwd(c):
        return jnp.where(c < n_ctx, n_lat + c, c - n_ctx)

    def bwd(c):
        return n_all - 1 - c

    hw = GDN_HEADS * GDN_DK

    def specs(order, d):
        blk = lambda r, w, col: pl.BlockSpec((1, r * TOK_BLK, w), lambda b, c: (b, order(c), col))
        return [blk(2, hw, d), blk(1, hw, d), blk(1, hw, d), blk(1, GDN_HEADS * CHUNK, d), blk(1, LANES, 0)]

    return pl.pallas_call(
        _gdn_scan_kernel,
        grid=(B, n_all),
        in_specs=specs(fwd, 0) + specs(bwd, 1),
        out_specs=[pl.BlockSpec((1, TOK_BLK, hw), lambda b, c: (b, fwd(c), 0)),
                   pl.BlockSpec((1, TOK_BLK, hw), lambda b, c: (b, bwd(c), 0))],
        out_shape=[jax.ShapeDtypeStruct((B, Tt, GDN_V_W), F32)] * 2,
        scratch_shapes=[pltpu.VMEM((N_CHAIN // 2, 2 * GDN_DK, 2 * GDN_DV), F32)],
        compiler_params=_cparams(("parallel", "arbitrary")),
        name="gdn_scan",
    )(wq, kg, u, qkm, gl, wq, kg, u, qkm, gl)


def _fnet_kernel(c_ref, s_ref, u_ref, cc_ref, sc_ref, *rest):
    o_ref = rest[-1]
    u = u_ref[0]
    p = _dot(c_ref[...], u).astype(BF16)
    q = _dot(s_ref[...], u).astype(BF16)
    o_ref[0] = (_dot(p, cc_ref[...]) - _dot(q, sc_ref[...])).astype(o_ref.dtype)


def _dft_mats(n):
    j = jnp.arange(n, dtype=jnp.int32)
    ang = ((j[:, None] * j[None, :]) % n).astype(F32) * (2.0 * math.pi / n)
    sc = 1.0 / math.sqrt(n)
    return jnp.cos(ang) * sc, jnp.sin(ang) * sc


def _fnet(ub, mats_lat, mats_ctx, mats_ch, T):
    B, Tt, W = ub.shape
    Lc = Tt - T
    cc, sc = mats_ch
    tm = min(512, T)
    const = lambda shape: pl.BlockSpec(shape, lambda m, b: (0, 0))
    out = pl.pallas_call(
        _fnet_kernel,
        grid=(T // tm, B),
        in_specs=[
            pl.BlockSpec((tm, T), lambda m, b: (m, 0)),
            pl.BlockSpec((tm, T), lambda m, b: (m, 0)),
            pl.BlockSpec((1, T, W), lambda m, b: (b, 0, 0)),
            const((W, W)), const((W, W)),
        ],
        out_specs=pl.BlockSpec((1, tm, W), lambda m, b: (b, m, 0)),
        out_shape=jax.ShapeDtypeStruct((B, Tt, W), BF16),
        compiler_params=_cparams(("parallel", "parallel"), VMEM_LIMIT_BIG),
        name="fnet_latent",
    )(mats_lat[0], mats_lat[1], ub, cc, sc)
    cblk = T // Lc
    return pl.pallas_call(
        _fnet_kernel,
        grid=(1, B),
        in_specs=[
            const((Lc, Lc)), const((Lc, Lc)),
            pl.BlockSpec((1, Lc, W), lambda m, b: (b, cblk, 0)),
            const((W, W)), const((W, W)),
            pl.BlockSpec(memory_space=pl.ANY),
        ],
        out_specs=pl.BlockSpec((1, Lc, W), lambda m, b: (b, cblk, 0)),
        out_shape=jax.ShapeDtypeStruct((B, Tt, W), BF16),
        input_output_aliases={5: 0},
        compiler_params=_cparams(("parallel", "parallel")),
        name="fnet_context",
    )(mats_ctx[0], mats_ctx[1], ub, cc, sc, out)


def _na_prep_kernel(x_ref, qw_ref, kw_ref, bd_ref, q_ref, k_ref):
    bd = bd_ref[...]

    def nrm(a, w):
        ms = _dot(a * a, bd, HI)
        return a * lax.rsqrt(ms + RMS_EPS) * w

    q = x_ref[0, :, 0:NA_W].astype(F32)
    k = x_ref[0, :, NA_W:2 * NA_W].astype(F32)
    q_ref[0] = (nrm(q, qw_ref[...]) * (NA_DH ** -0.5)).astype(BF16)
    k_ref[0] = nrm(k, kw_ref[...]).astype(BF16)


def _na_prep(naqkv, qw, kw, bd):
    B, Tt, _ = naqkv.shape
    const = lambda shape: pl.BlockSpec(shape, lambda b, t: (0, 0))
    tok = lambda w: pl.BlockSpec((1, TOK_BLK, w), lambda b, t: (b, t, 0))
    return pl.pallas_call(
        _na_prep_kernel,
        grid=(B, Tt // TOK_BLK),
        in_specs=[tok(NAQKV_W), const((1, NA_W)), const((1, NA_W)), const((NA_W, NA_W))],
        out_specs=[tok(NA_W), tok(NA_W)],
        out_shape=[jax.ShapeDtypeStruct((B, Tt, NA_W), BF16)] * 2,
        compiler_params=_cparams(("parallel", "parallel")),
        name="na_prep",
    )(naqkv, qw, kw, bd)


def _attend(q, keys, vals, biases, o_ref, orow):
    for h in range(NA_HEADS):
        hs = slice(h * NA_DH, (h + 1) * NA_DH)
        qh = q[:, hs]
        ss = []
        for kk, bias in zip(keys, biases):
            s = _dg(qh, kk[:, hs], _NT)
            if bias is not None:
                s = s + bias[h]
            ss.append(s)
        m = functools.reduce(jnp.maximum, [jnp.max(s, axis=-1, keepdims=True) for s in ss])
        ps = [jnp.exp(s - m) for s in ss]
        den = sum(jnp.sum(p, axis=-1, keepdims=True) for p in ps)
        o = sum(_dot(p.astype(BF16), vv[:, hs]) for p, vv in zip(ps, vals))
        o_ref[0, orow, hs] = (o / den).astype(o_ref.dtype)


NA_ROWS = 2


def _na_latent_kernel(q_ref, k_ref, v_ref, bias_ref, o_ref, *, T, Lc, rows):
    r0 = pl.program_id(1) * NA_ROWS
    kc = k_ref[0, T:T + Lc, :]
    vc = v_ref[0, T:T + Lc, :]
    nwin = WIN_R * GRID_W
    for i in range(NA_ROWS):
        start = pl.multiple_of(jnp.clip(r0 + i - WIN_R // 2, 0, rows - WIN_R) * GRID_W, GRID_W)
        keys = [k_ref[0, pl.ds(start, nwin), :], kc]
        vals = [v_ref[0, pl.ds(start, nwin), :], vc]
        _attend(q_ref[0, i * GRID_W:(i + 1) * GRID_W, :], keys, vals, [bias_ref.at[i], None], o_ref,
                slice(i * GRID_W, (i + 1) * GRID_W))


def _na_context_kernel(q_ref, k_ref, v_ref, prev_ref, o_ref):
    del prev_ref
    _attend(q_ref[0], [k_ref[0]], [v_ref[0]], [None], o_ref, slice(None))


def _na(qn, kn, naqkv, bias_rows, T):
    B, Tt, _ = qn.shape
    Lc = Tt - T
    rows = T // GRID_W
    nwin = WIN_R * GRID_W
    out = pl.pallas_call(
        functools.partial(_na_latent_kernel, T=T, Lc=Lc, rows=rows),
        grid=(B, rows // NA_ROWS),
        in_specs=[
            pl.BlockSpec((1, NA_ROWS * GRID_W, NA_W), lambda b, r: (b, r, 0)),
            pl.BlockSpec((1, Tt, NA_W), lambda b, r: (b, 0, 0)),
            pl.BlockSpec((1, Tt, NA_W), lambda b, r: (b, 0, 2)),
            pl.BlockSpec((NA_ROWS, NA_HEADS, GRID_W, nwin), lambda b, r: (r, 0, 0, 0)),
        ],
        out_specs=pl.BlockSpec((1, NA_ROWS * GRID_W, NA_W), lambda b, r: (b, r, 0)),
        out_shape=jax.ShapeDtypeStruct((B, Tt, NA_W), BF16),
        compiler_params=_cparams(("parallel", "arbitrary")),
        name="na_latent",
    )(qn, kn, naqkv, bias_rows)
    cblk = T // Lc
    return pl.pallas_call(
        _na_context_kernel,
        grid=(B,),
        in_specs=[
            pl.BlockSpec((1, Lc, NA_W), lambda b: (b, cblk, 0)),
            pl.BlockSpec((1, Lc, NA_W), lambda b: (b, cblk, 0)),
            pl.BlockSpec((1, Lc, NA_W), lambda b: (b, cblk, 2)),
            pl.BlockSpec(memory_space=pl.ANY),
        ],
        out_specs=pl.BlockSpec((1, Lc, NA_W), lambda b: (b, cblk, 0)),
        out_shape=jax.ShapeDtypeStruct((B, Tt, NA_W), BF16),
        input_output_aliases={3: 0},
        compiler_params=_cparams(("parallel",)),
        name="na_context",
    )(qn, kn, naqkv, out)


def _na_bias_rows(rpb, rows):
    r = jnp.arange(rows)
    cl = (r - jnp.clip(r - WIN_R // 2, 0, rows - WIN_R))[:, None]
    a = jnp.arange(WIN_R)[None, :]
    dr = a - cl + (WIN_R - 1)
    qc = jnp.arange(GRID_W)
    c_start = jnp.clip(qc - WIN_C // 2, 0, GRID_W - WIN_C)
    kc = jnp.arange(GRID_W)
    dc = kc[None, :] - qc[:, None] + (WIN_C - 1)
    ok = (kc[None, :] >= c_start[:, None]) & (kc[None, :] < c_start[:, None] + WIN_C)
    tbl = rpb[:, dr][:, :, :, jnp.clip(dc, 0, 2 * WIN_C - 2)]
    tbl = jnp.where(ok[None, None, None], tbl.astype(F32), NEG_BIG)
    return jnp.transpose(tbl, (1, 0, 3, 2, 4)).reshape(rows, NA_HEADS, GRID_W, WIN_R * GRID_W)


def _merge_kernel(of_ref, ob_ref, z_ref, fb_ref, oc_ref, gr_ref, x_ref, mod_ref, gnw_ref, n2w_ref,
                  wa_ref, wb_ref, wc_ref, wo_ref, wr_ref, x1_ref, h2_ref, aff_ref):
    o = of_ref[0] + ob_ref[0]
    z = z_ref[0].astype(F32)
    ya = jnp.zeros((TOK_BLK, D_MODEL), F32)
    for h in range(GDN_HEADS):
        hs = slice(h * GDN_DV, (h + 1) * GDN_DV)
        oh = o[:, hs]
        on = oh * lax.rsqrt(jnp.mean(oh * oh, axis=-1, keepdims=True) + RMS_EPS) * gnw_ref[...]
        ya = ya + _dot((on * _silu(z[:, hs])).astype(BF16), wa_ref[hs, :])
    yb = _dot(fb_ref[0], wb_ref[...])
    yc = _dot(oc_ref[0], wc_ref[...])
    gate = lambda j: jax.nn.sigmoid(gr_ref[0, :, j * D_MODEL:(j + 1) * D_MODEL].astype(F32))
    y = gate(0) * ya + gate(1) * yb + gate(2) * yc
    x1 = x_ref[0] + mod_ref[2:3, :] * _dot(y.astype(BF16), wo_ref[...])
    x1_ref[0] = x1
    h2 = _rms_mod(x1, n2w_ref[...], mod_ref[3:4, :], mod_ref[4:5, :]).astype(BF16)
    h2_ref[0] = h2
    logits = _dg(wr_ref[...], h2, _NT)
    ex = jnp.exp(logits - jnp.max(logits, axis=0, keepdims=True))
    aff_ref[0] = ex / jnp.sum(ex, axis=0, keepdims=True)


def _merge(o_f, o_b, z, fb, oc, gr, xs, modv, gnw, n2w, wa, wb, wc, wo, wr, n_lat_blk):
    B, Tt, D = xs.shape
    tok = lambda w: pl.BlockSpec((1, TOK_BLK, w), lambda b, t: (b, t, 0))
    const = lambda shape: pl.BlockSpec(shape, lambda b, t: (0, 0))
    return pl.pallas_call(
        _merge_kernel,
        grid=(B, Tt // TOK_BLK),
        in_specs=[
            tok(GDN_V_W), tok(GDN_V_W), tok(GDN_V_W), tok(FNET_W), tok(NA_W), tok(GATE_W), tok(D),
            pl.BlockSpec((None, None, 6, D), lambda b, t: (b, t // n_lat_blk, 0, 0)),
            const((1, GDN_DV)), const((1, D)),
            const((GDN_V_W, D)), const((FNET_W, D)), const((NA_W, D)), const((D, D)),
            const((N_EXPERTS, D)),
        ],
        out_specs=[tok(D), tok(D), pl.BlockSpec((1, N_EXPERTS, TOK_BLK), lambda b, t: (b, 0, t))],
        out_shape=[jax.ShapeDtypeStruct((B, Tt, D), F32), jax.ShapeDtypeStruct((B, Tt, D), BF16),
                   jax.ShapeDtypeStruct((B, N_EXPERTS, Tt), F32)],
        compiler_params=_cparams(("parallel", "parallel"), VMEM_LIMIT_BIG),
        name="merge",
    )(o_f, o_b, z, fb, oc, gr, xs, modv, gnw, n2w, wa, wb, wc, wo, wr)


def _select_kernel(a_ref, pos_ref, wa_ref, lo_ref, cnt_ref, *, T, Lc):
    ii = lax.broadcasted_iota(jnp.int32, (LANES, LANES), 0)
    jj = lax.broadcasted_iota(jnp.int32, (LANES, LANES), 1)
    tri = (ii < jj).astype(BF16)
    cap_lat = EC_CAPACITY_FACTOR * T // N_EXPERTS
    cap_ctx = EC_CAPACITY_FACTOR * Lc // N_EXPERTS
    blk_lane = lax.broadcasted_iota(jnp.int32, (N_EXPERTS, LANES), 1)
    lo_all = jnp.zeros((N_EXPERTS, LANES), F32)
    cnt_all = jnp.zeros((N_EXPERTS, LANES), F32)
    for s0, n, cap, poff in ((0, T, cap_lat, 0), (T, Lc, cap_ctx, cap_lat)):
        a = a_ref[0, :, s0:s0 + n]
        bits = lax.bitcast_convert_type(a, jnp.int32)

        def count(mask):
            return jnp.sum(jnp.where(mask, 1.0, 0.0), axis=1, keepdims=True)

        def radix(i, pref):
            cand = pref | jnp.left_shift(jnp.int32(1), 30 - i)
            return jnp.where(count(bits >= cand) >= cap, cand, pref)

        thr = lax.fori_loop(0, 31, radix, jnp.zeros((N_EXPERTS, 1), jnp.int32))
        gt = bits > thr
        eq = bits == thr
        need = cap - count(gt)
        idx = lax.broadcasted_iota(jnp.int32, (N_EXPERTS, n), 1)
        nbits = max(1, (n - 1).bit_length())

        def tie(i, ans):
            cand = ans | jnp.left_shift(jnp.int32(1), nbits - 1 - i)
            return jnp.where(count(eq & (idx < cand)) < need, cand, ans)

        last = lax.fori_loop(0, nbits, tie, jnp.zeros((N_EXPERTS, 1), jnp.int32))
        sel = gt | (eq & (idx <= last))
        selb = jnp.where(sel, 1.0, 0.0).astype(BF16)
        ti = lax.broadcasted_iota(jnp.int32, (n, LANES), 0)
        tj = lax.broadcasted_iota(jnp.int32, (n, LANES), 1)
        seg_tot = _dot(selb, (ti // LANES == tj).astype(BF16))
        seg_off = _dot(seg_tot.astype(BF16), tri)
        for s in range(n // LANES):
            ls = slice(s * LANES, (s + 1) * LANES)
            within = _dot(selb[:, ls], tri)
            p = (within + seg_off[:, s:s + 1]).astype(jnp.int32) + poff
            pos_ref[0, :, s0 + s * LANES:s0 + (s + 1) * LANES] = jnp.where(sel[:, ls], p, -1)
        wa_ref[0, :, s0:s0 + n] = jnp.where(sel, a, 0.0)
        blk_tot = _dot(selb, ((ti + s0) // TOK_BLK == tj).astype(BF16))
        blk_off = _dot(blk_tot.astype(BF16), tri) + poff
        mine = (blk_lane >= s0 // TOK_BLK) & (blk_lane < (s0 + n) // TOK_BLK)
        lo_all = jnp.where(mine, blk_off, lo_all)
        cnt_all = jnp.where(mine, blk_tot, cnt_all)
    lo_ref[0] = lo_all.astype(jnp.int32)
    cnt_ref[0] = cnt_all.astype(jnp.int32)


def _select(aff, T):
    B, E, Tt = aff.shape
    blk = pl.BlockSpec((1, E, Tt), lambda b: (b, 0, 0))
    meta = pl.BlockSpec((1, E, LANES), lambda b: (b, 0, 0))
    return pl.pallas_call(
        functools.partial(_select_kernel, T=T, Lc=Tt - T),
        grid=(B,),
        in_specs=[blk],
        out_specs=[blk, blk, meta, meta],
        out_shape=[jax.ShapeDtypeStruct((B, E, Tt), jnp.int32), jax.ShapeDtypeStruct((B, E, Tt), F32),
                   jax.ShapeDtypeStruct((B, E, LANES), jnp.int32), jax.ShapeDtypeStruct((B, E, LANES), jnp.int32)],
        compiler_params=_cparams(("parallel",)),
        name="ec_select",
    )(aff)


MOE_WIN = 64


def _moe_windows(lo_ref, cnt_ref, nch):
    b, c = pl.program_id(0), pl.program_id(1)
    w0s = []
    nmax = jnp.int32(0)
    for e in range(N_EXPERTS):
        i = (b * N_EXPERTS + e) * nch + c
        lo, cnt = lo_ref[i], cnt_ref[i]
        w0 = (lo // BF16_SUBLANES) * BF16_SUBLANES
        w0s.append(w0)
        nmax = jnp.maximum(nmax, jnp.where(cnt > 0, (lo - w0 + cnt + MOE_WIN - 1) // MOE_WIN, 0))
    return w0s, nmax


def _window_starts(w0s, i, R):
    return [pl.multiple_of(jnp.minimum(w0 + i * MOE_WIN, R), BF16_SUBLANES) for w0 in w0s]


def _moe_gather_kernel(lo_ref, cnt_ref, h_ref, pos_ref, xe_ref, *, R, nch):
    @pl.when(pl.program_id(1) == 0)
    def _():
        xe_ref[...] = jnp.zeros_like(xe_ref)

    w0s, nmax = _moe_windows(lo_ref, cnt_ref, nch)
    slot0 = lax.broadcasted_iota(jnp.int32, (MOE_WIN, TOK_BLK), 0)

    def win(i, carry):
        r0s = _window_starts(w0s, i, R)
        onehot = jnp.concatenate(
            [jnp.where(pos_ref[e] == slot0 + r0s[e], 1.0, 0.0).astype(BF16) for e in range(N_EXPERTS)], axis=0)
        rows = _dot(onehot, h_ref[0]).astype(BF16)
        for e in range(N_EXPERTS):
            xe_ref[e, pl.ds(r0s[e], MOE_WIN), :] += rows[e * MOE_WIN:(e + 1) * MOE_WIN]
        return carry

    lax.fori_loop(0, nmax, win, 0)


def _moe_ffn_kernel(xe_ref, wg_ref, wu_ref, wd_ref, ye_ref, *, R):
    xe = xe_ref[0:R, :]
    fstep = 512
    ye = jnp.zeros((R, D_MODEL), F32)
    for f in range(D_EXPERT // fstep):
        fs = slice(f * fstep, (f + 1) * fstep)
        hid = _silu(_dot(xe, wg_ref[:, fs])) * _dot(xe, wu_ref[:, fs])
        ye = ye + _dot(hid.astype(BF16), wd_ref[fs, :])
    ye_ref[0:R, :] = ye.astype(BF16)
    ye_ref[R:R + MOE_WIN, :] = jnp.zeros((MOE_WIN, D_MODEL), BF16)


def _moe_scatter_kernel(lo_ref, cnt_ref, ye_ref, pos_ref, wa_ref, x_ref, mod_ref, o_ref, *, R, nch):
    w0s, nmax = _moe_windows(lo_ref, cnt_ref, nch)
    slot0 = lax.broadcasted_iota(jnp.int32, (MOE_WIN, TOK_BLK), 0)

    def win(i, acc):
        r0s = _window_starts(w0s, i, R)
        weighted = jnp.concatenate(
            [jnp.where(pos_ref[e] == slot0 + r0s[e], wa_ref[e], 0.0).astype(BF16) for e in range(N_EXPERTS)], axis=0)
        ye = jnp.concatenate([ye_ref[e, pl.ds(r0s[e], MOE_WIN), :] for e in range(N_EXPERTS)], axis=0)
        return acc + _dg(weighted, ye, _TN)

    acc = lax.fori_loop(0, nmax, win, jnp.zeros((TOK_BLK, D_MODEL), F32))
    o_ref[0] = x_ref[0] + mod_ref[5:6, :] * acc


def _moe(h2, pos, waff, lo, cnt, wg, wu, wd, x1, modv, n_lat_blk, n_out_blk):
    B, Tt, D = h2.shape
    E = N_EXPERTS
    nch = Tt // TOK_BLK
    R = EC_CAPACITY_FACTOR * Tt // N_EXPERTS
    RP = R + MOE_WIN
    assert R % BF16_SUBLANES == 0
    pos = pos.reshape(B, E, nch, 1, TOK_BLK)
    waff = waff.reshape(B, E, nch, 1, TOK_BLK)
    lo = lo[:, :, :nch].reshape(-1)
    cnt = cnt[:, :, :nch].reshape(-1)
    one = pl.Buffered(1)
    rowspec = pl.BlockSpec((None, E, None, 1, TOK_BLK), lambda b, c, *_: (b, 0, c, 0, 0))
    tok = pl.BlockSpec((1, TOK_BLK, D), lambda b, c, *_: (b, c, 0))
    slots = pl.BlockSpec((None, E, RP, D), lambda b, c, *_: (b, 0, 0, 0), pipeline_mode=one)
    xe = pl.pallas_call(
        functools.partial(_moe_gather_kernel, R=R, nch=nch),
        grid_spec=pltpu.PrefetchScalarGridSpec(
            num_scalar_prefetch=2, grid=(B, nch), in_specs=[tok, rowspec], out_specs=slots),
        out_shape=jax.ShapeDtypeStruct((B, E, RP, D), BF16),
        compiler_params=_cparams(("parallel", "arbitrary"), VMEM_LIMIT_BIG),
        name="ec_gather",
    )(lo, cnt, h2, pos)
    wspec = lambda r, c: pl.BlockSpec((None, r, c), lambda e, b: (e, 0, 0))
    slot1 = pl.BlockSpec((None, None, RP, D), lambda e, b: (b, e, 0, 0))
    ye = pl.pallas_call(
        functools.partial(_moe_ffn_kernel, R=R),
        grid=(E, B),
        in_specs=[slot1, wspec(D, D_EXPERT), wspec(D, D_EXPERT), wspec(D_EXPERT, D)],
        out_specs=slot1,
        out_shape=jax.ShapeDtypeStruct((B, E, RP, D), BF16),
        compiler_params=_cparams(("parallel", "parallel"), VMEM_LIMIT_BIG),
        name="ec_ffn",
    )(xe, wg, wu, wd)
    return pl.pallas_call(
        functools.partial(_moe_scatter_kernel, R=R, nch=nch),
        grid_spec=pltpu.PrefetchScalarGridSpec(
            num_scalar_prefetch=2, grid=(B, n_out_blk),
            in_specs=[slots, rowspec, rowspec, tok,
                      pl.BlockSpec((None, None, 6, D), lambda b, c, *_: (b, c // n_lat_blk, 0, 0))],
            out_specs=tok),
        out_shape=jax.ShapeDtypeStruct((B, n_out_blk * TOK_BLK, D), F32),
        compiler_params=_cparams(("parallel", "arbitrary"), VMEM_LIMIT_BIG),
        name="ec_scatter",
    )(lo, cnt, ye, pos, waff, x1, modv)


def _rope_tables(T, Lc):
    nf = GDN_DK // 4
    inv = ROPE_BASE ** (-jnp.arange(nf, dtype=F32) / nf)
    t = jnp.arange(T)
    ang_r = (t // GRID_W).astype(F32)[:, None] * inv
    ang_c = (t % GRID_W).astype(F32)[:, None] * inv
    cos = jnp.concatenate([jnp.cos(ang_r)] * 2 + [jnp.cos(ang_c)] * 2, axis=-1)
    sin = jnp.concatenate([-jnp.sin(ang_r), jnp.sin(ang_r), -jnp.sin(ang_c), jnp.sin(ang_c)], axis=-1)
    cos = jnp.concatenate([cos, jnp.ones((Lc, GDN_DK), F32)], axis=0)
    sin = jnp.concatenate([sin, jnp.zeros((Lc, GDN_DK), F32)], axis=0)
    return cos, sin


def _reorder_w_in(w_in):
    qa, ka, va, za, aa, ba, ub, qn, kn, vn, gr = jnp.split(
        w_in, [512, 1024, 1536, 2048, 2056, 2064, 2320, 2576, 2832, 3088], axis=-1)
    pad = jnp.zeros(w_in.shape[:-1] + (AB_PAD - 2 * N_DIR * GDN_HEADS,), w_in.dtype)
    return jnp.concatenate([qa, ka, va, za, ub, qn, kn, vn, gr, aa, ba, pad], axis=-1).astype(BF16)


def kernel(x, c, ctx, c_ctx, w_mod, b_mod, norm1_w, norm2_w, w_in, conv_w, a_log, dt_bias, gdn_norm_w,
           na_qn_w, na_kn_w, na_rpb, w_br_a, w_br_b, w_br_c, w_out, w_router, w_e_gate, w_e_up, w_e_down):
    B, T, D = x.shape
    Lc = ctx.shape[1]
    Tt = T + Lc
    assert D == D_MODEL and T % TOK_BLK == 0 and Lc == TOK_BLK and T % Lc == 0
    assert T // GRID_W >= WIN_R and (T // GRID_W) % NA_ROWS == 0
    n_lat_blk = T // TOK_BLK

    w_in_r = _reorder_w_in(w_in)
    bf = lambda a: a.astype(BF16)
    w_mod16, wa16, wb16, wc16, wo16 = bf(w_mod), bf(w_br_a), bf(w_br_b), bf(w_br_c), bf(w_out)
    wg16, wu16, wd16 = bf(w_e_gate), bf(w_e_up), bf(w_e_down)
    wr16 = bf(jnp.swapaxes(w_router, 1, 2))
    pad8 = AB_PAD - N_DIR * GDN_HEADS
    al = jnp.pad(a_log.reshape(DEPTH, 1, -1), ((0, 0), (0, 0), (0, pad8)))
    dtb = jnp.pad(dt_bias.reshape(DEPTH, 1, -1), ((0, 0), (0, 0), (0, pad8)))
    qnw = jnp.tile(na_qn_w, (1, NA_HEADS))[:, None, :]
    knw = jnp.tile(na_kn_w, (1, NA_HEADS))[:, None, :]
    head_of = jnp.arange(NA_W) // NA_DH
    bd = (head_of[:, None] == head_of[None, :]).astype(F32) / NA_DH
    cos_t, sin_t = _rope_tables(T, Lc)
    mats_lat = tuple(bf(m) for m in _dft_mats(T))
    mats_ctx = tuple(bf(m) for m in _dft_mats(Lc))
    cch, sch = _dft_mats(FNET_GROUP_CH)
    eye_g = jnp.eye(FNET_GROUPS, dtype=F32)
    mats_ch = (bf(jnp.kron(eye_g, cch)), bf(jnp.kron(eye_g, sch)))

    rows = -(-(B + 1) // 8) * 8
    cs = jnp.zeros((rows, D), F32).at[:B].set(c).at[B].set(c_ctx)
    mod = _modulation(cs, w_mod16, b_mod[:, None, :])
    mod_lat = mod[:, :B].reshape(DEPTH, B, 1, 6, D)
    mod_ctx = jnp.broadcast_to(mod[:, B].reshape(DEPTH, 1, 1, 6, D), (DEPTH, B, 1, 6, D))
    modv = jnp.concatenate([mod_lat, mod_ctx], axis=2)

    xs = jnp.concatenate([x, ctx], axis=1)
    for i in range(DEPTH):
        last = i == DEPTH - 1
        qkv, z, ub, naqkv, gr, gb = _input_proj(xs, modv[i], norm1_w[i][None], w_in_r[i], al[i], dtb[i],
                                                n_lat_blk)
        qkvp = _gdn_prep(qkv, conv_w[i], cos_t, sin_t, T)
        o_f, o_b = _gdn_scan(*_gdn_chunk(qkvp, gb), T)
        fb = _fnet(ub, mats_lat, mats_ctx, mats_ch, T)
        qn, kn = _na_prep(naqkv, qnw[i], knw[i], bd)
        oc = _na(qn, kn, naqkv, _na_bias_rows(na_rpb[i], T // GRID_W), T)
        x1, h2, aff = _merge(o_f, o_b, z, fb, oc, gr, xs, modv[i], gdn_norm_w[i][None], norm2_w[i][None],
                             wa16[i], wb16[i], wc16[i], wo16[i], wr16[i], n_lat_blk)
        pos, waff, lo, cnt = _select(aff, T)
        xs = _moe(h2, pos, waff, lo, cnt, wg16[i], wu16[i], wd16[i], x1, modv[i], n_lat_blk,
                  n_lat_blk if last else Tt // TOK_BLK)
    return xs
```

```python
import functools
import math

import jax
import jax.numpy as jnp
from jax import lax
from jax.experimental import pallas as pl
from jax.experimental.pallas import tpu as pltpu

D_MODEL = 1024
DEPTH = 4
GRID_W = 64
RMS_EPS = 1e-6
ROPE_BASE = 10000.0

GDN_HEADS = 4
GDN_DK = 128
GDN_DV = 128
GDN_QK_W = GDN_HEADS * GDN_DK
GDN_V_W = GDN_HEADS * GDN_DV
CONV_W = 5
CHUNK = 64
N_DIR = 2

FNET_GROUPS = 4
FNET_GROUP_CH = 64
FNET_W = FNET_GROUPS * FNET_GROUP_CH

NA_HEADS = 4
NA_DH = 64
NA_W = NA_HEADS * NA_DH
WIN_R = 8
WIN_C = 16

N_BRANCH = 3
N_EXPERTS = 16
EC_CAPACITY_FACTOR = 2
D_EXPERT = 1024

LANES = 128
BF16_SUBLANES = 16
VMEM_LIMIT_BIG = 56 * 1024 * 1024

TOK_BLK = 256
QKV_W = 2 * GDN_QK_W + GDN_V_W
NAQKV_W = 3 * NA_W
GATE_W = N_BRANCH * D_MODEL
AB_PAD = LANES
N_IN_PAD = QKV_W + GDN_V_W + FNET_W + NAQKV_W + GATE_W + AB_PAD

F32 = jnp.float32
BF16 = jnp.bfloat16
HI = lax.Precision.HIGHEST
NEG_BIG = -1e30

_NT = (((1,), (1,)), ((), ()))
_TN = (((0,), (0,)), ((), ()))


def _dot(a, b, prec=None):
    return jnp.dot(a, b, preferred_element_type=F32, precision=prec)


def _dg(a, b, dims, prec=None):
    return lax.dot_general(a, b, dims, preferred_element_type=F32, precision=prec)


def _silu(v):
    return v * jax.nn.sigmoid(v)


def _cparams(sem, vmem=None):
    return pltpu.CompilerParams(dimension_semantics=sem, vmem_limit_bytes=vmem)


def _mod_kernel(c_ref, w_ref, b_ref, o_ref):
    s = _silu(c_ref[...]).astype(BF16)
    o_ref[0] = _dot(s, w_ref[0]) + b_ref[0]


def _modulation(cs, w_mod, b_mod):
    L, D, N = w_mod.shape
    R = cs.shape[0]
    tn = 1536
    return pl.pallas_call(
        _mod_kernel,
        grid=(L, N // tn),
        in_specs=[
            pl.BlockSpec((R, D), lambda l, n: (0, 0)),
            pl.BlockSpec((1, D, tn), lambda l, n: (l, 0, n)),
            pl.BlockSpec((1, 1, tn), lambda l, n: (l, 0, n)),
        ],
        out_specs=pl.BlockSpec((1, R, tn), lambda l, n: (l, 0, n)),
        out_shape=jax.ShapeDtypeStruct((L, R, N), F32),
        compiler_params=_cparams(("parallel", "parallel")),
        name="modulation",
    )(cs, w_mod, b_mod)


def _rms_mod(x, nw, shift, scale):
    ms = jnp.mean(x * x, axis=-1, keepdims=True)
    return (x * lax.rsqrt(ms + RMS_EPS) * nw) * (1.0 + scale) + shift


def _proj_kernel(x_ref, mod_ref, nw_ref, w_ref, al_ref, dt_ref,
                 qkv_ref, z_ref, ub_ref, naqkv_ref, gr_ref, gb_ref):
    h = _rms_mod(x_ref[0], nw_ref[...], mod_ref[0:1, :], mod_ref[1:2, :]).astype(BF16)

    def mm(lo, width):
        return _dot(h, w_ref[:, lo:lo + width])

    col = 0
    for ref, width in ((qkv_ref, QKV_W), (z_ref, GDN_V_W), (ub_ref, FNET_W),
                       (naqkv_ref, NAQKV_W), (gr_ref, GATE_W)):
        step = 512 if width % 512 == 0 else 256
        for j in range(width // step):
            ref[0, :, j * step:(j + 1) * step] = mm(col + j * step, step).astype(ref.dtype)
        col += width
    ab = mm(col, AB_PAD)
    sp_in = ab + dt_ref[...]
    softplus = jnp.maximum(sp_in, 0.0) + jnp.log(1.0 + jnp.exp(-jnp.abs(sp_in)))
    g = -jnp.exp(al_ref[...]) * softplus
    lane = lax.broadcasted_iota(jnp.int32, ab.shape, 1)
    gb_ref[0] = jnp.where(lane < N_DIR * GDN_HEADS, g, jax.nn.sigmoid(ab))


def _input_proj(xs, modv, nw, w_in, al, dtb, n_lat_blk):
    B, Tt, D = xs.shape
    nblk = Tt // TOK_BLK
    tok = lambda w: pl.BlockSpec((1, TOK_BLK, w), lambda b, t: (b, t, 0))
    widths = (QKV_W, GDN_V_W, FNET_W, NAQKV_W, GATE_W)
    return pl.pallas_call(
        _proj_kernel,
        grid=(B, nblk),
        in_specs=[
            tok(D),
            pl.BlockSpec((None, None, 6, D), lambda b, t: (b, t // n_lat_blk, 0, 0)),
            pl.BlockSpec((1, D), lambda b, t: (0, 0)),
            pl.BlockSpec((D, N_IN_PAD), lambda b, t: (0, 0)),
            pl.BlockSpec((1, AB_PAD), lambda b, t: (0, 0)),
            pl.BlockSpec((1, AB_PAD), lambda b, t: (0, 0)),
        ],
        out_specs=[tok(w) for w in widths] + [tok(AB_PAD)],
        out_shape=[jax.ShapeDtypeStruct((B, Tt, w), BF16) for w in widths]
        + [jax.ShapeDtypeStruct((B, Tt, AB_PAD), F32)],
        compiler_params=_cparams(("parallel", "parallel"), VMEM_LIMIT_BIG),
        name="input_proj",
    )(xs, modv, nw, w_in, al, dtb)


def _gdn_prep_kernel(u_ref, cw_ref, cos_ref, sin_ref, o_ref, *, T, Tt):
    j = pl.program_id(1)
    w = cw_ref[...]
    scale = jnp.where(j < GDN_HEADS, GDN_DK ** -0.5, 1.0).astype(F32)
    halo = BF16_SUBLANES
    lane = lax.broadcasted_iota(jnp.int32, (TOK_BLK, LANES), 1)
    exti = lax.broadcasted_iota(jnp.int32, (TOK_BLK + 2 * halo, LANES), 0)

    def conv_silu(start):
        ps = pl.multiple_of(jnp.maximum(start - halo, 0), halo)
        ns = pl.multiple_of(jnp.minimum(start + TOK_BLK, Tt - halo), halo)
        ext = jnp.concatenate([
            u_ref[0, pl.ds(ps, halo), :].astype(F32),
            u_ref[0, pl.ds(start, TOK_BLK), :].astype(F32),
            u_ref[0, pl.ds(ns, halo), :].astype(F32)], axis=0)
        erow = start - halo + exti
        seg_lo = jnp.where(start < T, 0, T)
        seg_hi = jnp.where(start < T, T, Tt)
        ext = jnp.where((erow >= seg_lo) & (erow < seg_hi), ext, 0.0)
        acc = jnp.zeros((TOK_BLK, LANES), F32)
        for d in range(-(CONV_W // 2), CONV_W // 2 + 1):
            acc = acc + ext[halo + d:halo + d + TOK_BLK, :] * w[d + CONV_W // 2:d + CONV_W // 2 + 1, :]
        return _silu(acc)

    def qk_body(c, carry):
        start = pl.multiple_of(c * TOK_BLK, TOK_BLK)
        y = conv_silu(start)
        yn = y * lax.rsqrt(jnp.sum(y * y, axis=-1, keepdims=True) + RMS_EPS)
        partner = jnp.where((lane & 32) == 0, pltpu.roll(yn, LANES - 32, 1), pltpu.roll(yn, 32, 1))
        rows = pl.ds(start, TOK_BLK)
        o_ref[0, rows, :] = (yn * cos_ref[rows, :] + partner * sin_ref[rows, :]) * scale
        return carry

    def v_body(c, carry):
        start = pl.multiple_of(c * TOK_BLK, TOK_BLK)
        o_ref[0, pl.ds(start, TOK_BLK), :] = conv_silu(start)
        return carry

    @pl.when(j < 2 * GDN_HEADS)
    def _():
        lax.fori_loop(0, Tt // TOK_BLK, qk_body, 0)

    @pl.when(j >= 2 * GDN_HEADS)
    def _():
        lax.fori_loop(0, Tt // TOK_BLK, v_body, 0)


def _gdn_prep(qkv, conv_w, cos_t, sin_t, T):
    B, Tt, W = qkv.shape
    return pl.pallas_call(
        functools.partial(_gdn_prep_kernel, T=T, Tt=Tt),
        grid=(B, W // LANES),
        in_specs=[
            pl.BlockSpec((1, Tt, LANES), lambda b, j: (b, 0, j)),
            pl.BlockSpec((CONV_W, LANES), lambda b, j: (0, j)),
            pl.BlockSpec((Tt, LANES), lambda b, j: (0, 0)),
            pl.BlockSpec((Tt, LANES), lambda b, j: (0, 0)),
        ],
        out_specs=pl.BlockSpec((1, Tt, LANES), lambda b, j: (b, 0, j)),
        out_shape=jax.ShapeDtypeStruct((B, Tt, W), F32),
        compiler_params=_cparams(("parallel", "parallel")),
        name="gdn_prep",
    )(qkv, conv_w, cos_t, sin_t)


N_CHAIN = N_DIR * GDN_HEADS


def _unit_tri_inverses(ms, eye, blk16, blk32):
    b = lambda a: a.astype(BF16)
    n1 = [-jnp.where(blk16, m, 0.0) for m in ms]
    n1b = [b(n) for n in n1]
    n2b = [b(_dot(n, n)) for n in n1b]
    n4b = [b(_dot(n, n)) for n in n2b]
    n8b = [b(_dot(n, n)) for n in n4b]
    ps = [eye + n for n in n1]
    for powers in (n2b, n4b, n8b):
        ps = [p + _dot(b(p), n) for p, n in zip(ps, powers)]
    for pick in (lambda m: jnp.where(blk32 & jnp.logical_not(blk16), m, 0.0), lambda m: jnp.where(blk32, 0.0, m)):
        pbs = [b(p) for p in ps]
        ts = [b(_dot(pb, b(pick(m)))) for pb, m in zip(pbs, ms)]
        ps = [p - _dot(t, pb) for p, t, pb in zip(ps, ts, pbs)]
    return ps


def _gdn_chunk_kernel(q_ref, k_ref, v_ref, g_ref, wq_ref, kg_ref, u_ref, qkm_ref, gl_ref):
    HB = GDN_HEADS * CHUNK
    ii = lax.broadcasted_iota(jnp.int32, (HB, HB), 0)
    jj = lax.broadcasted_iota(jnp.int32, (HB, HB), 1)
    eye = (ii == jj).astype(F32)
    blk16 = (ii >> 4) == (jj >> 4)
    blk32 = (ii >> 5) == (jj >> 5)
    blk64 = (ii >> 6) == (jj >> 6)
    masks = ((blk64 & (ii >= jj), blk64 & (ii > jj)), (blk64 & (ii <= jj), blk64 & (ii < jj)))
    row = lax.broadcasted_iota(jnp.int32, (CHUNK, LANES), 0)

    def cumsum_rows(x, backward):
        sh = 1
        while sh < CHUNK:
            if backward:
                x = x + jnp.where(row < CHUNK - sh, pltpu.roll(x, CHUNK - sh, 0), 0.0)
            else:
                x = x + jnp.where(row >= sh, pltpu.roll(x, sh, 0), 0.0)
            sh *= 2
        return x

    ms, rhss, dests = [], [], []
    for j in range(TOK_BLK // CHUNK):
        rows = slice(j * CHUNK, (j + 1) * CHUNK)
        gbv = g_ref[0, rows, :]
        tot_row = jnp.sum(gbv, axis=0, keepdims=True)
        gl_ref[0, rows, :] = jnp.broadcast_to(jnp.exp(tot_row), (CHUNK, LANES))
        stack = lambda ref: jnp.concatenate(
            [ref[0, rows, h * GDN_DK:(h + 1) * GDN_DK] for h in range(GDN_HEADS)], axis=0)
        qst, kst, vst = stack(q_ref), stack(k_ref), stack(v_ref)
        k16 = kst.astype(BF16)
        kk = _dg(k16, k16, _NT)
        qk = _dg(qst.astype(BF16), k16, _NT)
        for d, (le, lt) in enumerate(masks):
            gc_d = cumsum_rows(gbv, d == 1)
            col = lambda a, off: jnp.concatenate(
                [a[:, off + d * GDN_HEADS + h:off + d * GDN_HEADS + h + 1] for h in range(GDN_HEADS)], axis=0)
            gc = col(gc_d, 0)
            beta = col(gbv, N_CHAIN)
            tot = col(jnp.broadcast_to(tot_row, (CHUNK, LANES)), 0)
            gcm = jnp.broadcast_to(gc, (HB, HB))
            dec = jnp.where(le, jnp.exp(jnp.where(le, gcm - gcm.T, 0.0)), 0.0)
            ms.append(jnp.where(lt, beta * kk * dec, 0.0))
            eg = jnp.exp(gc)
            rhss.append(jnp.concatenate([vst * beta, kst * (beta * eg)], axis=1).astype(BF16))
            dests.append((j, d))
            qkm = jnp.where(le, qk * dec, 0.0).astype(BF16)
            qg = (qst * eg).astype(BF16)
            kg = (kst * jnp.exp(tot - gc)).astype(BF16)
            for h in range(GDN_HEADS):
                l = d * GDN_HEADS + h
                hr = slice(h * CHUNK, (h + 1) * CHUNK)
                ls = slice(l * GDN_DK, (l + 1) * GDN_DK)
                wq_ref[0, (2 * j + 1) * CHUNK:(2 * j + 2) * CHUNK, ls] = qg[hr]
                kg_ref[0, rows, ls] = kg[hr]
                qkm_ref[0, rows, l * CHUNK:(l + 1) * CHUNK] = qkm[hr, h * CHUNK:(h + 1) * CHUNK]
    tinvs = _unit_tri_inverses(ms, eye, blk16, blk32)
    sols = [_dot(t.astype(BF16), r) for t, r in zip(tinvs, rhss)]
    for (j, d), sol in zip(dests, sols):
        for h in range(GDN_HEADS):
            hr = slice(h * CHUNK, (h + 1) * CHUNK)
            ls = slice((d * GDN_HEADS + h) * GDN_DK, (d * GDN_HEADS + h + 1) * GDN_DK)
            u_ref[0, j * CHUNK:(j + 1) * CHUNK, ls] = sol[hr, :GDN_DV]
            wq_ref[0, 2 * j * CHUNK:(2 * j + 1) * CHUNK, ls] = sol[hr, GDN_DV:].astype(BF16)


def _gdn_chunk(qkvp, gb):
    B, Tt, _ = qkvp.shape
    cw = N_CHAIN * GDN_DK
    tok = lambda col, w, r=1: pl.BlockSpec((1, r * TOK_BLK, w), lambda b, t: (b, t, col))
    return pl.pallas_call(
        _gdn_chunk_kernel,
        grid=(B, Tt // TOK_BLK),
        in_specs=[tok(0, GDN_QK_W), tok(1, GDN_QK_W), tok(2, GDN_V_W), tok(0, LANES)],
        out_specs=[tok(0, cw, 2), tok(0, cw), tok(0, cw), tok(0, N_CHAIN * CHUNK), tok(0, LANES)],
        out_shape=[jax.ShapeDtypeStruct((B, 2 * Tt, cw), BF16), jax.ShapeDtypeStruct((B, Tt, cw), BF16),
                   jax.ShapeDtypeStruct((B, Tt, cw), F32), jax.ShapeDtypeStruct((B, Tt, N_CHAIN * CHUNK), BF16),
                   jax.ShapeDtypeStruct((B, Tt, LANES), F32)],
        compiler_params=_cparams(("parallel", "parallel")),
        name="gdn_chunk",
    )(qkvp, qkvp, qkvp, gb)


def _gdn_scan_kernel(wqf_ref, kgf_ref, uf_ref, qkmf_ref, glf_ref, wqb_ref, kgb_ref, ub_ref, qkmb_ref, glb_ref,
                     of_ref, ob_ref, s_ref):
    @pl.when(pl.program_id(1) == 0)
    def _():
        s_ref[...] = jnp.zeros_like(s_ref)

    n_sub = TOK_BLK // CHUNK
    pw = 2 * GDN_DK
    same_head = (lax.broadcasted_iota(jnp.int32, (pw, pw), 0) >> 7) == (lax.broadcasted_iota(jnp.int32, (pw, pw), 1) >> 7)
    first = lax.broadcasted_iota(jnp.int32, (CHUNK, pw), 1) < GDN_DV
    first_row = first[0:1, :]
    dirs = ((wqf_ref, kgf_ref, uf_ref, qkmf_ref, glf_ref, of_ref), (wqb_ref, kgb_ref, ub_ref, qkmb_ref, glb_ref, ob_ref))
    n_pair = GDN_HEADS // 2
    for step in range(n_sub):
        work = []
        for d, refs in enumerate(dirs):
            j = step if d == 0 else n_sub - 1 - step
            work += [(d, p, j, slice(j * CHUNK, (j + 1) * CHUNK), slice(p * pw, (p + 1) * pw)) + refs
                     for p in range(n_pair)]
        ss = [s_ref[d * n_pair + p] for d, p, *_ in work]
        wss = [_dot(wq_ref[0, 2 * j * CHUNK:2 * (j + 1) * CHUNK, ps], s.astype(BF16))
               for s, (d, p, j, rows, ps, wq_ref, *_) in zip(ss, work)]
        vns = [u_ref[0, rows, ps] - ws[:CHUNK]
               for ws, (d, p, j, rows, ps, wq_ref, kg_ref, u_ref, *_) in zip(wss, work)]
        upds = [_dg(kg_ref[0, rows, ps], vn.astype(BF16), _TN)
                for vn, (d, p, j, rows, ps, wq_ref, kg_ref, *_) in zip(vns, work)]
        for s, ws, vn, upd, (d, p, j, rows, ps, wq_ref, kg_ref, u_ref, qkm_ref, gl_ref, o_ref) in zip(
                ss, wss, vns, upds, work):
            vn_bd = jnp.concatenate([jnp.where(first, vn, 0.0), jnp.where(first, 0.0, vn)], axis=0).astype(BF16)
            o_ref[0, rows, ps] = ws[CHUNK:] + _dot(qkm_ref[0, rows, p * 2 * CHUNK:(p + 1) * 2 * CHUNK], vn_bd)
            gl = gl_ref[0, j * CHUNK:j * CHUNK + 1, :]
            l0 = d * GDN_HEADS + 2 * p
            glv = jnp.where(first_row, gl[:, l0:l0 + 1], gl[:, l0 + 1:l0 + 2])
            s_ref[d * n_pair + p] = s * glv + jnp.where(same_head, upd, 0.0)


def _gdn_scan(wq, kg, u, qkm, gl, T):
    B, Tt, _ = kg.shape
    n_lat = T // TOK_BLK
    n_all = Tt // TOK_BLK
    n_ctx = n_all - n_lat

    def fwd(c):
        return jnp.where(c < n_ctx, n_lat + c, c - n_ctx)

    def bwd(c):
        return n_all - 1 - c

    hw = GDN_HEADS * GDN_DK

    def specs(order, d):
        blk = lambda r, w, col: pl.BlockSpec((1, r * TOK_BLK, w), lambda b, c: (b, order(c), col))
        return [blk(2, hw, d), blk(1, hw, d), blk(1, hw, d), blk(1, GDN_HEADS * CHUNK, d), blk(1, LANES, 0)]

    return pl.pallas_call(
        _gdn_scan_kernel,
        grid=(B, n_all),
        in_specs=specs(fwd, 0) + specs(bwd, 1),
        out_specs=[pl.BlockSpec((1, TOK_BLK, hw), lambda b, c: (b, fwd(c), 0)),
                   pl.BlockSpec((1, TOK_BLK, hw), lambda b, c: (b, bwd(c), 0))],
        out_shape=[jax.ShapeDtypeStruct((B, Tt, GDN_V_W), F32)] * 2,
        scratch_shapes=[pltpu.VMEM((N_CHAIN // 2, 2 * GDN_DK, 2 * GDN_DV), F32)],
        compiler_params=_cparams(("parallel", "arbitrary")),
        name="gdn_scan",
    )(wq, kg, u, qkm, gl, wq, kg, u, qkm, gl)


def _fnet_kernel(c_ref, s_ref, u_ref, cc_ref, sc_ref, *rest):
    o_ref = rest[-1]
    u = u_ref[0]
    p = _dot(c_ref[...], u).astype(BF16)
    q = _dot(s_ref[...], u).astype(BF16)
    o_ref[0] = (_dot(p, cc_ref[...]) - _dot(q, sc_ref[...])).astype(o_ref.dtype)


def _dft_mats(n):
    j = jnp.arange(n, dtype=jnp.int32)
    ang = ((j[:, None] * j[None, :]) % n).astype(F32) * (2.0 * math.pi / n)
    sc = 1.0 / math.sqrt(n)
    return jnp.cos(ang) * sc, jnp.sin(ang) * sc


def _fnet(ub, mats_lat, mats_ctx, mats_ch, T):
    B, Tt, W = ub.shape
    Lc = Tt - T
    cc, sc = mats_ch
    tm = min(512, T)
    const = lambda shape: pl.BlockSpec(shape, lambda m, b: (0, 0))
    out = pl.pallas_call(
        _fnet_kernel,
        grid=(T // tm, B),
        in_specs=[
            pl.BlockSpec((tm, T), lambda m, b: (m, 0)),
            pl.BlockSpec((tm, T), lambda m, b: (m, 0)),
            pl.BlockSpec((1, T, W), lambda m, b: (b, 0, 0)),
            const((W, W)), const((W, W)),
        ],
        out_specs=pl.BlockSpec((1, tm, W), lambda m, b: (b, m, 0)),
        out_shape=jax.ShapeDtypeStruct((B, Tt, W), BF16),
        compiler_params=_cparams(("parallel", "parallel"), VMEM_LIMIT_BIG),
        name="fnet_latent",
    )(mats_lat[0], mats_lat[1], ub, cc, sc)
    cblk = T // Lc
    return pl.pallas_call(
        _fnet_kernel,
        grid=(1, B),
        in_specs=[
            const((Lc, Lc)), const((Lc, Lc)),
            pl.BlockSpec((1, Lc, W), lambda m, b: (b, cblk, 0)),
            const((W, W)), const((W, W)),
            pl.BlockSpec(memory_space=pl.ANY),
        ],
        out_specs=pl.BlockSpec((1, Lc, W), lambda m, b: (b, cblk, 0)),
        out_shape=jax.ShapeDtypeStruct((B, Tt, W), BF16),
        input_output_aliases={5: 0},
        compiler_params=_cparams(("parallel", "parallel")),
        name="fnet_context",
    )(mats_ctx[0], mats_ctx[1], ub, cc, sc, out)


def _na_prep_kernel(x_ref, qw_ref, kw_ref, bd_ref, q_ref, k_ref):
    bd = bd_ref[...]

    def nrm(a, w):
        ms = _dot(a * a, bd, HI)
        return a * lax.rsqrt(ms + RMS_EPS) * w

    q = x_ref[0, :, 0:NA_W].astype(F32)
    k = x_ref[0, :, NA_W:2 * NA_W].astype(F32)
    q_ref[0] = (nrm(q, qw_ref[...]) * (NA_DH ** -0.5)).astype(BF16)
    k_ref[0] = nrm(k, kw_ref[...]).astype(BF16)


def _na_prep(naqkv, qw, kw, bd):
    B, Tt, _ = naqkv.shape
    const = lambda shape: pl.BlockSpec(shape, lambda b, t: (0, 0))
    tok = lambda w: pl.BlockSpec((1, TOK_BLK, w), lambda b, t: (b, t, 0))
    return pl.pallas_call(
        _na_prep_kernel,
        grid=(B, Tt // TOK_BLK),
        in_specs=[tok(NAQKV_W), const((1, NA_W)), const((1, NA_W)), const((NA_W, NA_W))],
        out_specs=[tok(NA_W), tok(NA_W)],
        out_shape=[jax.ShapeDtypeStruct((B, Tt, NA_W), BF16)] * 2,
        compiler_params=_cparams(("parallel", "parallel")),
        name="na_prep",
    )(naqkv, qw, kw, bd)


def _attend(q, keys, vals, biases, o_ref, orow):
    for h in range(NA_HEADS):
        hs = slice(h * NA_DH, (h + 1) * NA_DH)
        qh = q[:, hs]
        ss = []
        for kk, bias in zip(keys, biases):
            s = _dg(qh, kk[:, hs], _NT)
            if bias is not None:
                s = s + bias[h]
            ss.append(s)
        m = functools.reduce(jnp.maximum, [jnp.max(s, axis=-1, keepdims=True) for s in ss])
        ps = [jnp.exp(s - m) for s in ss]
        den = sum(jnp.sum(p, axis=-1, keepdims=True) for p in ps)
        o = sum(_dot(p.astype(BF16), vv[:, hs]) for p, vv in zip(ps, vals))
        o_ref[0, orow, hs] = (o / den).astype(o_ref.dtype)


NA_ROWS = 4
NA_KROWS = WIN_R + NA_ROWS - 1


def _na_window_row(g, rows):
    return jnp.clip(g * NA_ROWS - WIN_R // 2, 0, rows - NA_KROWS)


def _na_latent_kernel(q_ref, k_ref, v_ref, bias_ref, o_ref, *, T, Lc, rows):
    start = pl.multiple_of(_na_window_row(pl.program_id(1), rows) * GRID_W, GRID_W)
    nwin = NA_KROWS * GRID_W
    keys = [k_ref[0, pl.ds(start, nwin), :], k_ref[0, T:T + Lc, :]]
    vals = [v_ref[0, pl.ds(start, nwin), :], v_ref[0, T:T + Lc, :]]
    _attend(q_ref[0], keys, vals, [bias_ref, None], o_ref, slice(None))


def _na_context_kernel(q_ref, k_ref, v_ref, prev_ref, o_ref):
    del prev_ref
    _attend(q_ref[0], [k_ref[0]], [v_ref[0]], [None], o_ref, slice(None))


def _na(qn, kn, naqkv, bias_tbl, T):
    B, Tt, _ = qn.shape
    Lc = Tt - T
    rows = T // GRID_W
    uniq, _ = _na_group_offsets(rows)

    def table_of(g):
        off = g * NA_ROWS - _na_window_row(g, rows)
        return sum(jnp.where(off > u, 1, 0) for u in uniq)

    out = pl.pallas_call(
        functools.partial(_na_latent_kernel, T=T, Lc=Lc, rows=rows),
        grid=(B, rows // NA_ROWS),
        in_specs=[
            pl.BlockSpec((1, NA_ROWS * GRID_W, NA_W), lambda b, g: (b, g, 0)),
            pl.BlockSpec((1, Tt, NA_W), lambda b, g: (b, 0, 0)),
            pl.BlockSpec((1, Tt, NA_W), lambda b, g: (b, 0, 2)),
            pl.BlockSpec((None, NA_HEADS, NA_ROWS * GRID_W, NA_KROWS * GRID_W), lambda b, g: (table_of(g), 0, 0, 0)),
        ],
        out_specs=pl.BlockSpec((1, NA_ROWS * GRID_W, NA_W), lambda b, g: (b, g, 0)),
        out_shape=jax.ShapeDtypeStruct((B, Tt, NA_W), BF16),
        compiler_params=_cparams(("parallel", "arbitrary")),
        name="na_latent",
    )(qn, kn, naqkv, bias_tbl)
    cblk = T // Lc
    return pl.pallas_call(
        _na_context_kernel,
        grid=(B,),
        in_specs=[
            pl.BlockSpec((1, Lc, NA_W), lambda b: (b, cblk, 0)),
            pl.BlockSpec((1, Lc, NA_W), lambda b: (b, cblk, 0)),
            pl.BlockSpec((1, Lc, NA_W), lambda b: (b, cblk, 2)),
            pl.BlockSpec(memory_space=pl.ANY),
        ],
        out_specs=pl.BlockSpec((1, Lc, NA_W), lambda b: (b, cblk, 0)),
        out_shape=jax.ShapeDtypeStruct((B, Tt, NA_W), BF16),
        input_output_aliases={3: 0},
        compiler_params=_cparams(("parallel",)),
        name="na_context",
    )(qn, kn, naqkv, out)


def _na_group_offsets(rows):
    offs = [g * NA_ROWS - min(max(g * NA_ROWS - WIN_R // 2, 0), rows - NA_KROWS) for g in range(rows // NA_ROWS)]
    return sorted(set(offs)), offs


def _na_bias_table(rpb, rows):
    uniq, offs = _na_group_offsets(rows)
    qc = jnp.arange(GRID_W)
    c_start = jnp.clip(qc - WIN_C // 2, 0, GRID_W - WIN_C)
    kc = jnp.arange(GRID_W)
    dc = kc[None, :] - qc[:, None] + (WIN_C - 1)
    col_ok = (kc[None, :] >= c_start[:, None]) & (kc[None, :] < c_start[:, None] + WIN_C)
    tables = []
    for off in uniq:
        g = offs.index(off)
        ws = g * NA_ROWS - off
        r = g * NA_ROWS + jnp.arange(NA_ROWS)
        rs = jnp.clip(r - WIN_R // 2, 0, rows - WIN_R)
        krow = ws + jnp.arange(NA_KROWS)
        row_ok = (krow[None, :] >= rs[:, None]) & (krow[None, :] < rs[:, None] + WIN_R)
        dr = jnp.clip(krow[None, :] - r[:, None] + (WIN_R - 1), 0, 2 * WIN_R - 2)
        tbl = rpb[:, dr][:, :, :, jnp.clip(dc, 0, 2 * WIN_C - 2)]
        ok = row_ok[None, :, :, None, None] & col_ok[None, None, None]
        tbl = jnp.where(ok, tbl.astype(F32), NEG_BIG)
        tables.append(jnp.transpose(tbl, (0, 1, 3, 2, 4)).reshape(NA_HEADS, NA_ROWS * GRID_W, NA_KROWS * GRID_W))
    return jnp.stack(tables)


def _merge_kernel(of_ref, ob_ref, z_ref, fb_ref, oc_ref, gr_ref, x_ref, mod_ref, gnw_ref, n2w_ref,
                  wa_ref, wb_ref, wc_ref, wo_ref, wr_ref, x1_ref, h2_ref, aff_ref):
    o = of_ref[0] + ob_ref[0]
    z = z_ref[0].astype(F32)
    ya = jnp.zeros((TOK_BLK, D_MODEL), F32)
    for h in range(GDN_HEADS):
        hs = slice(h * GDN_DV, (h + 1) * GDN_DV)
        oh = o[:, hs]
        on = oh * lax.rsqrt(jnp.mean(oh * oh, axis=-1, keepdims=True) + RMS_EPS) * gnw_ref[...]
        ya = ya + _dot((on * _silu(z[:, hs])).astype(BF16), wa_ref[hs, :])
    yb = _dot(fb_ref[0], wb_ref[...])
    yc = _dot(oc_ref[0], wc_ref[...])
    gate = lambda j: jax.nn.sigmoid(gr_ref[0, :, j * D_MODEL:(j + 1) * D_MODEL].astype(F32))
    y = gate(0) * ya + gate(1) * yb + gate(2) * yc
    x1 = x_ref[0] + mod_ref[2:3, :] * _dot(y.astype(BF16), wo_ref[...])
    x1_ref[0] = x1
    h2 = _rms_mod(x1, n2w_ref[...], mod_ref[3:4, :], mod_ref[4:5, :]).astype(BF16)
    h2_ref[0] = h2
    logits = _dg(wr_ref[...], h2, _NT)
    ex = jnp.exp(logits - jnp.max(logits, axis=0, keepdims=True))
    aff_ref[0] = ex / jnp.sum(ex, axis=0, keepdims=True)


def _merge(o_f, o_b, z, fb, oc, gr, xs, modv, gnw, n2w, wa, wb, wc, wo, wr, n_lat_blk):
    B, Tt, D = xs.shape
    tok = lambda w: pl.BlockSpec((1, TOK_BLK, w), lambda b, t: (b, t, 0))
    const = lambda shape: pl.BlockSpec(shape, lambda b, t: (0, 0))
    return pl.pallas_call(
        _merge_kernel,
        grid=(B, Tt // TOK_BLK),
        in_specs=[
            tok(GDN_V_W), tok(GDN_V_W), tok(GDN_V_W), tok(FNET_W), tok(NA_W), tok(GATE_W), tok(D),
            pl.BlockSpec((None, None, 6, D), lambda b, t: (b, t // n_lat_blk, 0, 0)),
            const((1, GDN_DV)), const((1, D)),
            const((GDN_V_W, D)), const((FNET_W, D)), const((NA_W, D)), const((D, D)),
            const((N_EXPERTS, D)),
        ],
        out_specs=[tok(D), tok(D), pl.BlockSpec((1, N_EXPERTS, TOK_BLK), lambda b, t: (b, 0, t))],
        out_shape=[jax.ShapeDtypeStruct((B, Tt, D), F32), jax.ShapeDtypeStruct((B, Tt, D), BF16),
                   jax.ShapeDtypeStruct((B, N_EXPERTS, Tt), F32)],
        compiler_params=_cparams(("parallel", "parallel"), VMEM_LIMIT_BIG),
        name="merge",
    )(o_f, o_b, z, fb, oc, gr, xs, modv, gnw, n2w, wa, wb, wc, wo, wr)


def _select_kernel(a_ref, pos_ref, wa_ref, lo_ref, cnt_ref, *, T, Lc):
    ii = lax.broadcasted_iota(jnp.int32, (LANES, LANES), 0)
    jj = lax.broadcasted_iota(jnp.int32, (LANES, LANES), 1)
    tri = (ii < jj).astype(BF16)
    cap_lat = EC_CAPACITY_FACTOR * T // N_EXPERTS
    cap_ctx = EC_CAPACITY_FACTOR * Lc // N_EXPERTS
    blk_lane = lax.broadcasted_iota(jnp.int32, (N_EXPERTS, LANES), 1)
    lo_all = jnp.zeros((N_EXPERTS, LANES), F32)
    cnt_all = jnp.zeros((N_EXPERTS, LANES), F32)
    for s0, n, cap, poff in ((0, T, cap_lat, 0), (T, Lc, cap_ctx, cap_lat)):
        a = a_ref[0, :, s0:s0 + n]
        bits = lax.bitcast_convert_type(a, jnp.int32)

        def count(mask):
            return jnp.sum(jnp.where(mask, 1.0, 0.0), axis=1, keepdims=True)

        def radix(i, pref):
            cand = pref | jnp.left_shift(jnp.int32(1), 30 - i)
            return jnp.where(count(bits >= cand) >= cap, cand, pref)

        thr = lax.fori_loop(0, 31, radix, jnp.zeros((N_EXPERTS, 1), jnp.int32))
        gt = bits > thr
        eq = bits == thr
        need = cap - count(gt)
        idx = lax.broadcasted_iota(jnp.int32, (N_EXPERTS, n), 1)
        nbits = max(1, (n - 1).bit_length())

        def tie(i, ans):
            cand = ans | jnp.left_shift(jnp.int32(1), nbits - 1 - i)
            return jnp.where(count(eq & (idx < cand)) < need, cand, ans)

        last = lax.fori_loop(0, nbits, tie, jnp.zeros((N_EXPERTS, 1), jnp.int32))
        sel = gt | (eq & (idx <= last))
        selb = jnp.where(sel, 1.0, 0.0).astype(BF16)
        ti = lax.broadcasted_iota(jnp.int32, (n, LANES), 0)
        tj = lax.broadcasted_iota(jnp.int32, (n, LANES), 1)
        seg_tot = _dot(selb, (ti // LANES == tj).astype(BF16))
        seg_off = _dot(seg_tot.astype(BF16), tri)
        for s in range(n // LANES):
            ls = slice(s * LANES, (s + 1) * LANES)
            within = _dot(selb[:, ls], tri)
            p = (within + seg_off[:, s:s + 1]).astype(jnp.int32) + poff
            pos_ref[0, :, s0 + s * LANES:s0 + (s + 1) * LANES] = jnp.where(sel[:, ls], p, -1)
        wa_ref[0, :, s0:s0 + n] = jnp.where(sel, a, 0.0)
        blk_tot = _dot(selb, ((ti + s0) // TOK_BLK == tj).astype(BF16))
        blk_off = _dot(blk_tot.astype(BF16), tri) + poff
        mine = (blk_lane >= s0 // TOK_BLK) & (blk_lane < (s0 + n) // TOK_BLK)
        lo_all = jnp.where(mine, blk_off, lo_all)
        cnt_all = jnp.where(mine, blk_tot, cnt_all)
    lo_ref[0] = lo_all.astype(jnp.int32)
    cnt_ref[0] = cnt_all.astype(jnp.int32)


def _select(aff, T):
    B, E, Tt = aff.shape
    blk = pl.BlockSpec((1, E, Tt), lambda b: (b, 0, 0))
    meta = pl.BlockSpec((1, E, LANES), lambda b: (b, 0, 0))
    return pl.pallas_call(
        functools.partial(_select_kernel, T=T, Lc=Tt - T),
        grid=(B,),
        in_specs=[blk],
        out_specs=[blk, blk, meta, meta],
        out_shape=[jax.ShapeDtypeStruct((B, E, Tt), jnp.int32), jax.ShapeDtypeStruct((B, E, Tt), F32),
                   jax.ShapeDtypeStruct((B, E, LANES), jnp.int32), jax.ShapeDtypeStruct((B, E, LANES), jnp.int32)],
        compiler_params=_cparams(("parallel",)),
        name="ec_select",
    )(aff)


MOE_WIN = 64


def _moe_windows(lo_ref, cnt_ref, nch):
    b, c = pl.program_id(0), pl.program_id(1)
    w0s = []
    nmax = jnp.int32(0)
    for e in range(N_EXPERTS):
        i = (b * N_EXPERTS + e) * nch + c
        lo, cnt = lo_ref[i], cnt_ref[i]
        w0 = (lo // BF16_SUBLANES) * BF16_SUBLANES
        w0s.append(w0)
        nmax = jnp.maximum(nmax, jnp.where(cnt > 0, (lo - w0 + cnt + MOE_WIN - 1) // MOE_WIN, 0))
    return w0s, nmax


def _window_starts(w0s, i, R):
    return [pl.multiple_of(jnp.minimum(w0 + i * MOE_WIN, R), BF16_SUBLANES) for w0 in w0s]


def _moe_gather_kernel(lo_ref, cnt_ref, h_ref, pos_ref, xe_ref, *, R, nch):
    @pl.when(pl.program_id(1) == 0)
    def _():
        xe_ref[...] = jnp.zeros_like(xe_ref)

    w0s, nmax = _moe_windows(lo_ref, cnt_ref, nch)
    slot0 = lax.broadcasted_iota(jnp.int32, (MOE_WIN, TOK_BLK), 0)

    def win(i, carry):
        r0s = _window_starts(w0s, i, R)
        onehot = jnp.concatenate(
            [jnp.where(pos_ref[e] == slot0 + r0s[e], 1.0, 0.0).astype(BF16) for e in range(N_EXPERTS)], axis=0)
        rows = _dot(onehot, h_ref[0]).astype(BF16)
        for e in range(N_EXPERTS):
            xe_ref[e, pl.ds(r0s[e], MOE_WIN), :] += rows[e * MOE_WIN:(e + 1) * MOE_WIN]
        return carry

    lax.fori_loop(0, nmax, win, 0)


def _moe_ffn_kernel(xe_ref, wg_ref, wu_ref, wd_ref, ye_ref, *, R):
    xe = xe_ref[0:R, :]
    fstep = 512
    ye = jnp.zeros((R, D_MODEL), F32)
    for f in range(D_EXPERT // fstep):
        fs = slice(f * fstep, (f + 1) * fstep)
        hid = _silu(_dot(xe, wg_ref[:, fs])) * _dot(xe, wu_ref[:, fs])
        ye = ye + _dot(hid.astype(BF16), wd_ref[fs, :])
    ye_ref[0:R, :] = ye.astype(BF16)
    ye_ref[R:R + MOE_WIN, :] = jnp.zeros((MOE_WIN, D_MODEL), BF16)


def _moe_scatter_kernel(lo_ref, cnt_ref, ye_ref, pos_ref, wa_ref, x_ref, mod_ref, o_ref, *, R, nch):
    w0s, nmax = _moe_windows(lo_ref, cnt_ref, nch)
    slot0 = lax.broadcasted_iota(jnp.int32, (MOE_WIN, TOK_BLK), 0)

    def win(i, acc):
        r0s = _window_starts(w0s, i, R)
        weighted = jnp.concatenate(
            [jnp.where(pos_ref[e] == slot0 + r0s[e], wa_ref[e], 0.0).astype(BF16) for e in range(N_EXPERTS)], axis=0)
        ye = jnp.concatenate([ye_ref[e, pl.ds(r0s[e], MOE_WIN), :] for e in range(N_EXPERTS)], axis=0)
        return acc + _dg(weighted, ye, _TN)

    acc = lax.fori_loop(0, nmax, win, jnp.zeros((TOK_BLK, D_MODEL), F32))
    o_ref[0] = x_ref[0] + mod_ref[5:6, :] * acc


def _moe(h2, pos, waff, lo, cnt, wg, wu, wd, x1, modv, n_lat_blk, n_out_blk):
    B, Tt, D = h2.shape
    E = N_EXPERTS
    nch = Tt // TOK_BLK
    R = EC_CAPACITY_FACTOR * Tt // N_EXPERTS
    RP = R + MOE_WIN
    assert R % BF16_SUBLANES == 0
    pos = pos.reshape(B, E, nch, 1, TOK_BLK)
    waff = waff.reshape(B, E, nch, 1, TOK_BLK)
    lo = lo[:, :, :nch].reshape(-1)
    cnt = cnt[:, :, :nch].reshape(-1)
    one = pl.Buffered(1)
    rowspec = pl.BlockSpec((None, E, None, 1, TOK_BLK), lambda b, c, *_: (b, 0, c, 0, 0))
    tok = pl.BlockSpec((1, TOK_BLK, D), lambda b, c, *_: (b, c, 0))
    slots = pl.BlockSpec((None, E, RP, D), lambda b, c, *_: (b, 0, 0, 0), pipeline_mode=one)
    xe = pl.pallas_call(
        functools.partial(_moe_gather_kernel, R=R, nch=nch),
        grid_spec=pltpu.PrefetchScalarGridSpec(
            num_scalar_prefetch=2, grid=(B, nch), in_specs=[tok, rowspec], out_specs=slots),
        out_shape=jax.ShapeDtypeStruct((B, E, RP, D), BF16),
        compiler_params=_cparams(("parallel", "arbitrary"), VMEM_LIMIT_BIG),
        name="ec_gather",
    )(lo, cnt, h2, pos)
    wspec = lambda r, c: pl.BlockSpec((None, r, c), lambda e, b: (e, 0, 0))
    slot1 = pl.BlockSpec((None, None, RP, D), lambda e, b: (b, e, 0, 0))
    ye = pl.pallas_call(
        functools.partial(_moe_ffn_kernel, R=R),
        grid=(E, B),
        in_specs=[slot1, wspec(D, D_EXPERT), wspec(D, D_EXPERT), wspec(D_EXPERT, D)],
        out_specs=slot1,
        out_shape=jax.ShapeDtypeStruct((B, E, RP, D), BF16),
        compiler_params=_cparams(("parallel", "parallel"), VMEM_LIMIT_BIG),
        name="ec_ffn",
    )(xe, wg, wu, wd)
    return pl.pallas_call(
        functools.partial(_moe_scatter_kernel, R=R, nch=nch),
        grid_spec=pltpu.PrefetchScalarGridSpec(
            num_scalar_prefetch=2, grid=(B, n_out_blk),
            in_specs=[slots, rowspec, rowspec, tok,
                      pl.BlockSpec((None, None, 6, D), lambda b, c, *_: (b, c // n_lat_blk, 0, 0))],
            out_specs=tok),
        out_shape=jax.ShapeDtypeStruct((B, n_out_blk * TOK_BLK, D), F32),
        compiler_params=_cparams(("parallel", "arbitrary"), VMEM_LIMIT_BIG),
        name="ec_scatter",
    )(lo, cnt, ye, pos, waff, x1, modv)


def _rope_tables(T, Lc):
    nf = GDN_DK // 4
    inv = ROPE_BASE ** (-jnp.arange(nf, dtype=F32) / nf)
    t = jnp.arange(T)
    ang_r = (t // GRID_W).astype(F32)[:, None] * inv
    ang_c = (t % GRID_W).astype(F32)[:, None] * inv
    cos = jnp.concatenate([jnp.cos(ang_r)] * 2 + [jnp.cos(ang_c)] * 2, axis=-1)
    sin = jnp.concatenate([-jnp.sin(ang_r), jnp.sin(ang_r), -jnp.sin(ang_c), jnp.sin(ang_c)], axis=-1)
    cos = jnp.concatenate([cos, jnp.ones((Lc, GDN_DK), F32)], axis=0)
    sin = jnp.concatenate([sin, jnp.zeros((Lc, GDN_DK), F32)], axis=0)
    return cos, sin


def _reorder_w_in(w_in):
    qa, ka, va, za, aa, ba, ub, qn, kn, vn, gr = jnp.split(
        w_in, [512, 1024, 1536, 2048, 2056, 2064, 2320, 2576, 2832, 3088], axis=-1)
    pad = jnp.zeros(w_in.shape[:-1] + (AB_PAD - 2 * N_DIR * GDN_HEADS,), w_in.dtype)
    return jnp.concatenate([qa, ka, va, za, ub, qn, kn, vn, gr, aa, ba, pad], axis=-1).astype(BF16)


def kernel(x, c, ctx, c_ctx, w_mod, b_mod, norm1_w, norm2_w, w_in, conv_w, a_log, dt_bias, gdn_norm_w,
           na_qn_w, na_kn_w, na_rpb, w_br_a, w_br_b, w_br_c, w_out, w_router, w_e_gate, w_e_up, w_e_down):
    B, T, D = x.shape
    Lc = ctx.shape[1]
    Tt = T + Lc
    assert D == D_MODEL and T % TOK_BLK == 0 and Lc == TOK_BLK and T % Lc == 0
    assert T // GRID_W >= NA_KROWS and (T // GRID_W) % NA_ROWS == 0
    n_lat_blk = T // TOK_BLK

    w_in_r = _reorder_w_in(w_in)
    bf = lambda a: a.astype(BF16)
    w_mod16, wa16, wb16, wc16, wo16 = bf(w_mod), bf(w_br_a), bf(w_br_b), bf(w_br_c), bf(w_out)
    wg16, wu16, wd16 = bf(w_e_gate), bf(w_e_up), bf(w_e_down)
    wr16 = bf(jnp.swapaxes(w_router, 1, 2))
    pad8 = AB_PAD - N_DIR * GDN_HEADS
    al = jnp.pad(a_log.reshape(DEPTH, 1, -1), ((0, 0), (0, 0), (0, pad8)))
    dtb = jnp.pad(dt_bias.reshape(DEPTH, 1, -1), ((0, 0), (0, 0), (0, pad8)))
    qnw = jnp.tile(na_qn_w, (1, NA_HEADS))[:, None, :]
    knw = jnp.tile(na_kn_w, (1, NA_HEADS))[:, None, :]
    head_of = jnp.arange(NA_W) // NA_DH
    bd = (head_of[:, None] == head_of[None, :]).astype(F32) / NA_DH
    cos_t, sin_t = _rope_tables(T, Lc)
    mats_lat = tuple(bf(m) for m in _dft_mats(T))
    mats_ctx = tuple(bf(m) for m in _dft_mats(Lc))
    cch, sch = _dft_mats(FNET_GROUP_CH)
    eye_g = jnp.eye(FNET_GROUPS, dtype=F32)
    mats_ch = (bf(jnp.kron(eye_g, cch)), bf(jnp.kron(eye_g, sch)))

    rows = -(-(B + 1) // 8) * 8
    cs = jnp.zeros((rows, D), F32).at[:B].set(c).at[B].set(c_ctx)
    mod = _modulation(cs, w_mod16, b_mod[:, None, :])
    mod_lat = mod[:, :B].reshape(DEPTH, B, 1, 6, D)
    mod_ctx = jnp.broadcast_to(mod[:, B].reshape(DEPTH, 1, 1, 6, D), (DEPTH, B, 1, 6, D))
    modv = jnp.concatenate([mod_lat, mod_ctx], axis=2)

    xs = jnp.concatenate([x, ctx], axis=1)
    for i in range(DEPTH):
        last = i == DEPTH - 1
        qkv, z, ub, naqkv, gr, gb = _input_proj(xs, modv[i], norm1_w[i][None], w_in_r[i], al[i], dtb[i],
                                                n_lat_blk)
        qkvp = _gdn_prep(qkv, conv_w[i], cos_t, sin_t, T)
        o_f, o_b = _gdn_scan(*_gdn_chunk(qkvp, gb), T)
        fb = _fnet(ub, mats_lat, mats_ctx, mats_ch, T)
        qn, kn = _na_prep(naqkv, qnw[i], knw[i], bd)
        oc = _na(qn, kn, naqkv, _na_bias_table(na_rpb[i], T // GRID_W), T)
        x1, h2, aff = _merge(o_f, o_b, z, fb, oc, gr, xs, modv[i], gdn_norm_w[i][None], norm2_w[i][None],
                             wa16[i], wb16[i], wc16[i], wo16[i], wr16[i], n_lat_blk)
        pos, waff, lo, cnt = _select(aff, T)
        xs = _moe(h2, pos, waff, lo, cnt, wg16[i], wu16[i], wd16[i], x1, modv[i], n_lat_blk,
                  n_lat_blk if last else Tt // TOK_BLK)
    return xs
```

```python
import functools
import math

import jax
import jax.numpy as jnp
from jax import lax
from jax.experimental import pallas as pl
from jax.experimental.pallas import tpu as pltpu

D_MODEL = 1024
DEPTH = 4
GRID_W = 64
RMS_EPS = 1e-6
ROPE_BASE = 10000.0

GDN_HEADS = 4
GDN_DK = 128
GDN_DV = 128
GDN_QK_W = GDN_HEADS * GDN_DK
GDN_V_W = GDN_HEADS * GDN_DV
CONV_W = 5
CHUNK = 64
N_DIR = 2

FNET_GROUPS = 4
FNET_GROUP_CH = 64
FNET_W = FNET_GROUPS * FNET_GROUP_CH

NA_HEADS = 4
NA_DH = 64
NA_W = NA_HEADS * NA_DH
WIN_R = 8
WIN_C = 16

N_BRANCH = 3
N_EXPERTS = 16
EC_CAPACITY_FACTOR = 2
D_EXPERT = 1024

LANES = 128
BF16_SUBLANES = 16
VMEM_LIMIT_BIG = 56 * 1024 * 1024

TOK_BLK = 256
QKV_W = 2 * GDN_QK_W + GDN_V_W
NAQKV_W = 3 * NA_W
GATE_W = N_BRANCH * D_MODEL
AB_PAD = LANES
N_IN_PAD = QKV_W + GDN_V_W + FNET_W + NAQKV_W + GATE_W + AB_PAD

F32 = jnp.float32
BF16 = jnp.bfloat16
HI = lax.Precision.HIGHEST
NEG_BIG = -1e30

_NT = (((1,), (1,)), ((), ()))
_TN = (((0,), (0,)), ((), ()))


def _dot(a, b, prec=None):
    return jnp.dot(a, b, preferred_element_type=F32, precision=prec)


def _dg(a, b, dims, prec=None):
    return lax.dot_general(a, b, dims, preferred_element_type=F32, precision=prec)


def _silu(v):
    return v * jax.nn.sigmoid(v)


def _cparams(sem, vmem=None):
    return pltpu.CompilerParams(dimension_semantics=sem, vmem_limit_bytes=vmem)


def _mod_kernel(c_ref, w_ref, b_ref, o_ref):
    s = _silu(c_ref[...]).astype(BF16)
    o_ref[0] = _dot(s, w_ref[0]) + b_ref[0]


def _modulation(cs, w_mod, b_mod):
    L, D, N = w_mod.shape
    R = cs.shape[0]
    tn = 1536
    return pl.pallas_call(
        _mod_kernel,
        grid=(L, N // tn),
        in_specs=[
            pl.BlockSpec((R, D), lambda l, n: (0, 0)),
            pl.BlockSpec((1, D, tn), lambda l, n: (l, 0, n)),
            pl.BlockSpec((1, 1, tn), lambda l, n: (l, 0, n)),
        ],
        out_specs=pl.BlockSpec((1, R, tn), lambda l, n: (l, 0, n)),
        out_shape=jax.ShapeDtypeStruct((L, R, N), F32),
        compiler_params=_cparams(("parallel", "parallel")),
        name="modulation",
    )(cs, w_mod, b_mod)


def _rms_mod(x, nw, shift, scale):
    ms = jnp.mean(x * x, axis=-1, keepdims=True)
    return (x * lax.rsqrt(ms + RMS_EPS) * nw) * (1.0 + scale) + shift


def _proj_kernel(x_ref, mod_ref, nw_ref, w_ref, al_ref, dt_ref, qw_ref, kw_ref, bd_ref,
                 qkv_ref, z_ref, ub_ref, naqkv_ref, gr_ref, gb_ref):
    h = _rms_mod(x_ref[0], nw_ref[...], mod_ref[0:1, :], mod_ref[1:2, :]).astype(BF16)

    def mm(lo, width):
        return _dot(h, w_ref[:, lo:lo + width])

    col = 0
    for ref, width in ((qkv_ref, QKV_W), (z_ref, GDN_V_W), (ub_ref, FNET_W)):
        step = 512 if width % 512 == 0 else 256
        for j in range(width // step):
            ref[0, :, j * step:(j + 1) * step] = mm(col + j * step, step).astype(ref.dtype)
        col += width

    def head_rms(a, w):
        sq = a * a
        hi = sq.astype(BF16)
        lo = (sq - hi.astype(F32)).astype(BF16)
        ms = _dot(hi, bd_ref[...]) + _dot(lo, bd_ref[...])
        return a * lax.rsqrt(ms + RMS_EPS) * w

    naqkv_ref[0, :, 0:NA_W] = (head_rms(mm(col, NA_W), qw_ref[...]) * (NA_DH ** -0.5)).astype(BF16)
    naqkv_ref[0, :, NA_W:2 * NA_W] = head_rms(mm(col + NA_W, NA_W), kw_ref[...]).astype(BF16)
    naqkv_ref[0, :, 2 * NA_W:3 * NA_W] = mm(col + 2 * NA_W, NA_W).astype(BF16)
    col += NAQKV_W
    for j in range(GATE_W // 512):
        gr_ref[0, :, j * 512:(j + 1) * 512] = mm(col + j * 512, 512).astype(BF16)
    col += GATE_W
    ab = mm(col, AB_PAD)
    sp_in = ab + dt_ref[...]
    softplus = jnp.maximum(sp_in, 0.0) + jnp.log(1.0 + jnp.exp(-jnp.abs(sp_in)))
    g = -jnp.exp(al_ref[...]) * softplus
    lane = lax.broadcasted_iota(jnp.int32, ab.shape, 1)
    gb_ref[0] = jnp.where(lane < N_DIR * GDN_HEADS, g, jax.nn.sigmoid(ab))


def _input_proj(xs, modv, nw, w_in, al, dtb, qw, kw, bd, n_lat_blk):
    B, Tt, D = xs.shape
    nblk = Tt // TOK_BLK
    tok = lambda w: pl.BlockSpec((1, TOK_BLK, w), lambda b, t: (b, t, 0))
    widths = (QKV_W, GDN_V_W, FNET_W, NAQKV_W, GATE_W)
    return pl.pallas_call(
        _proj_kernel,
        grid=(B, nblk),
        in_specs=[
            tok(D),
            pl.BlockSpec((None, None, 6, D), lambda b, t: (b, t // n_lat_blk, 0, 0)),
            pl.BlockSpec((1, D), lambda b, t: (0, 0)),
            pl.BlockSpec((D, N_IN_PAD), lambda b, t: (0, 0)),
            pl.BlockSpec((1, AB_PAD), lambda b, t: (0, 0)),
            pl.BlockSpec((1, AB_PAD), lambda b, t: (0, 0)),
            pl.BlockSpec((1, NA_W), lambda b, t: (0, 0)),
            pl.BlockSpec((1, NA_W), lambda b, t: (0, 0)),
            pl.BlockSpec((NA_W, NA_W), lambda b, t: (0, 0)),
        ],
        out_specs=[tok(w) for w in widths] + [tok(AB_PAD)],
        out_shape=[jax.ShapeDtypeStruct((B, Tt, w), BF16) for w in widths]
        + [jax.ShapeDtypeStruct((B, Tt, AB_PAD), F32)],
        compiler_params=_cparams(("parallel", "parallel"), VMEM_LIMIT_BIG),
        name="input_proj",
    )(xs, modv, nw, w_in, al, dtb, qw, kw, bd)


def _gdn_prep_kernel(u_ref, cw_ref, cos_ref, sin_ref, o_ref, *, T, Tt):
    j = pl.program_id(1)
    w = cw_ref[...]
    scale = jnp.where(j < GDN_HEADS, GDN_DK ** -0.5, 1.0).astype(F32)
    halo = BF16_SUBLANES
    lane = lax.broadcasted_iota(jnp.int32, (TOK_BLK, LANES), 1)
    exti = lax.broadcasted_iota(jnp.int32, (TOK_BLK + 2 * halo, LANES), 0)

    def conv_silu(start):
        ps = pl.multiple_of(jnp.maximum(start - halo, 0), halo)
        ns = pl.multiple_of(jnp.minimum(start + TOK_BLK, Tt - halo), halo)
        ext = jnp.concatenate([
            u_ref[0, pl.ds(ps, halo), :].astype(F32),
            u_ref[0, pl.ds(start, TOK_BLK), :].astype(F32),
            u_ref[0, pl.ds(ns, halo), :].astype(F32)], axis=0)
        erow = start - halo + exti
        seg_lo = jnp.where(start < T, 0, T)
        seg_hi = jnp.where(start < T, T, Tt)
        ext = jnp.where((erow >= seg_lo) & (erow < seg_hi), ext, 0.0)
        acc = jnp.zeros((TOK_BLK, LANES), F32)
        for d in range(-(CONV_W // 2), CONV_W // 2 + 1):
            acc = acc + ext[halo + d:halo + d + TOK_BLK, :] * w[d + CONV_W // 2:d + CONV_W // 2 + 1, :]
        return _silu(acc)

    def qk_body(c, carry):
        start = pl.multiple_of(c * TOK_BLK, TOK_BLK)
        y = conv_silu(start)
        yn = y * lax.rsqrt(jnp.sum(y * y, axis=-1, keepdims=True) + RMS_EPS)
        partner = jnp.where((lane & 32) == 0, pltpu.roll(yn, LANES - 32, 1), pltpu.roll(yn, 32, 1))
        rows = pl.ds(start, TOK_BLK)
        o_ref[0, rows, :] = (yn * cos_ref[rows, :] + partner * sin_ref[rows, :]) * scale
        return carry

    def v_body(c, carry):
        start = pl.multiple_of(c * TOK_BLK, TOK_BLK)
        o_ref[0, pl.ds(start, TOK_BLK), :] = conv_silu(start)
        return carry

    @pl.when(j < 2 * GDN_HEADS)
    def _():
        lax.fori_loop(0, Tt // TOK_BLK, qk_body, 0)

    @pl.when(j >= 2 * GDN_HEADS)
    def _():
        lax.fori_loop(0, Tt // TOK_BLK, v_body, 0)


def _gdn_prep(qkv, conv_w, cos_t, sin_t, T):
    B, Tt, W = qkv.shape
    return pl.pallas_call(
        functools.partial(_gdn_prep_kernel, T=T, Tt=Tt),
        grid=(B, W // LANES),
        in_specs=[
            pl.BlockSpec((1, Tt, LANES), lambda b, j: (b, 0, j)),
            pl.BlockSpec((CONV_W, LANES), lambda b, j: (0, j)),
            pl.BlockSpec((Tt, LANES), lambda b, j: (0, 0)),
            pl.BlockSpec((Tt, LANES), lambda b, j: (0, 0)),
        ],
        out_specs=pl.BlockSpec((1, Tt, LANES), lambda b, j: (b, 0, j)),
        out_shape=jax.ShapeDtypeStruct((B, Tt, W), F32),
        compiler_params=_cparams(("parallel", "parallel")),
        name="gdn_prep",
    )(qkv, conv_w, cos_t, sin_t)


N_CHAIN = N_DIR * GDN_HEADS


def _unit_tri_inverses(ms, eye, blk16, blk32):
    b = lambda a: a.astype(BF16)
    n1 = [-jnp.where(blk16, m, 0.0) for m in ms]
    n1b = [b(n) for n in n1]
    n2b = [b(_dot(n, n)) for n in n1b]
    n4b = [b(_dot(n, n)) for n in n2b]
    n8b = [b(_dot(n, n)) for n in n4b]
    ps = [eye + n for n in n1]
    for powers in (n2b, n4b, n8b):
        ps = [p + _dot(b(p), n) for p, n in zip(ps, powers)]
    for pick in (lambda m: jnp.where(blk32 & jnp.logical_not(blk16), m, 0.0), lambda m: jnp.where(blk32, 0.0, m)):
        pbs = [b(p) for p in ps]
        ts = [b(_dot(pb, b(pick(m)))) for pb, m in zip(pbs, ms)]
        ps = [p - _dot(t, pb) for p, t, pb in zip(ps, ts, pbs)]
    return ps


def _gdn_chunk_kernel(q_ref, k_ref, v_ref, g_ref, wq_ref, kg_ref, u_ref, qkm_ref, gl_ref):
    HB = GDN_HEADS * CHUNK
    ii = lax.broadcasted_iota(jnp.int32, (HB, HB), 0)
    jj = lax.broadcasted_iota(jnp.int32, (HB, HB), 1)
    eye = (ii == jj).astype(F32)
    blk16 = (ii >> 4) == (jj >> 4)
    blk32 = (ii >> 5) == (jj >> 5)
    blk64 = (ii >> 6) == (jj >> 6)
    masks = ((blk64 & (ii >= jj), blk64 & (ii > jj)), (blk64 & (ii <= jj), blk64 & (ii < jj)))
    row = lax.broadcasted_iota(jnp.int32, (CHUNK, LANES), 0)

    def cumsum_rows(x, backward):
        sh = 1
        while sh < CHUNK:
            if backward:
                x = x + jnp.where(row < CHUNK - sh, pltpu.roll(x, CHUNK - sh, 0), 0.0)
            else:
                x = x + jnp.where(row >= sh, pltpu.roll(x, sh, 0), 0.0)
            sh *= 2
        return x

    ms, rhss, dests = [], [], []
    for j in range(TOK_BLK // CHUNK):
        rows = slice(j * CHUNK, (j + 1) * CHUNK)
        gbv = g_ref[0, rows, :]
        tot_row = jnp.sum(gbv, axis=0, keepdims=True)
        gl_ref[0, rows, :] = jnp.broadcast_to(jnp.exp(tot_row), (CHUNK, LANES))
        stack = lambda ref: jnp.concatenate(
            [ref[0, rows, h * GDN_DK:(h + 1) * GDN_DK] for h in range(GDN_HEADS)], axis=0)
        qst, kst, vst = stack(q_ref), stack(k_ref), stack(v_ref)
        k16 = kst.astype(BF16)
        kk = _dg(k16, k16, _NT)
        qk = _dg(qst.astype(BF16), k16, _NT)
        for d, (le, lt) in enumerate(masks):
            gc_d = cumsum_rows(gbv, d == 1)
            col = lambda a, off: jnp.concatenate(
                [a[:, off + d * GDN_HEADS + h:off + d * GDN_HEADS + h + 1] for h in range(GDN_HEADS)], axis=0)
            gc = col(gc_d, 0)
            beta = col(gbv, N_CHAIN)
            tot = col(jnp.broadcast_to(tot_row, (CHUNK, LANES)), 0)
            gcm = jnp.broadcast_to(gc, (HB, HB))
            dec = jnp.where(le, jnp.exp(jnp.where(le, gcm - gcm.T, 0.0)), 0.0)
            ms.append(jnp.where(lt, beta * kk * dec, 0.0))
            eg = jnp.exp(gc)
            rhss.append(jnp.concatenate([vst * beta, kst * (beta * eg)], axis=1).astype(BF16))
            dests.append((j, d))
            qkm = jnp.where(le, qk * dec, 0.0).astype(BF16)
            qg = (qst * eg).astype(BF16)
            kg = (kst * jnp.exp(tot - gc)).astype(BF16)
            for h in range(GDN_HEADS):
                l = d * GDN_HEADS + h
                hr = slice(h * CHUNK, (h + 1) * CHUNK)
                ls = slice(l * GDN_DK, (l + 1) * GDN_DK)
                wq_ref[0, (2 * j + 1) * CHUNK:(2 * j + 2) * CHUNK, ls] = qg[hr]
                kg_ref[0, rows, ls] = kg[hr]
                qkm_ref[0, rows, l * CHUNK:(l + 1) * CHUNK] = qkm[hr, h * CHUNK:(h + 1) * CHUNK]
    tinvs = _unit_tri_inverses(ms, eye, blk16, blk32)
    sols = [_dot(t.astype(BF16), r) for t, r in zip(tinvs, rhss)]
    for (j, d), sol in zip(dests, sols):
        for h in range(GDN_HEADS):
            hr = slice(h * CHUNK, (h + 1) * CHUNK)
            ls = slice((d * GDN_HEADS + h) * GDN_DK, (d * GDN_HEADS + h + 1) * GDN_DK)
            u_ref[0, j * CHUNK:(j + 1) * CHUNK, ls] = sol[hr, :GDN_DV]
            wq_ref[0, 2 * j * CHUNK:(2 * j + 1) * CHUNK, ls] = sol[hr, GDN_DV:].astype(BF16)


def _gdn_chunk(qkvp, gb):
    B, Tt, _ = qkvp.shape
    cw = N_CHAIN * GDN_DK
    tok = lambda col, w, r=1: pl.BlockSpec((1, r * TOK_BLK, w), lambda b, t: (b, t, col))
    return pl.pallas_call(
        _gdn_chunk_kernel,
        grid=(B, Tt // TOK_BLK),
        in_specs=[tok(0, GDN_QK_W), tok(1, GDN_QK_W), tok(2, GDN_V_W), tok(0, LANES)],
        out_specs=[tok(0, cw, 2), tok(0, cw), tok(0, cw), tok(0, N_CHAIN * CHUNK), tok(0, LANES)],
        out_shape=[jax.ShapeDtypeStruct((B, 2 * Tt, cw), BF16), jax.ShapeDtypeStruct((B, Tt, cw), BF16),
                   jax.ShapeDtypeStruct((B, Tt, cw), F32), jax.ShapeDtypeStruct((B, Tt, N_CHAIN * CHUNK), BF16),
                   jax.ShapeDtypeStruct((B, Tt, LANES), F32)],
        compiler_params=_cparams(("parallel", "parallel")),
        name="gdn_chunk",
    )(qkvp, qkvp, qkvp, gb)


def _gdn_scan_kernel(wqf_ref, kgf_ref, uf_ref, qkmf_ref, glf_ref, wqb_ref, kgb_ref, ub_ref, qkmb_ref, glb_ref,
                     of_ref, ob_ref, s_ref):
    @pl.when(pl.program_id(1) == 0)
    def _():
        s_ref[...] = jnp.zeros_like(s_ref)

    n_sub = TOK_BLK // CHUNK
    pw = 2 * GDN_DK
    same_head = (lax.broadcasted_iota(jnp.int32, (pw, pw), 0) >> 7) == (lax.broadcasted_iota(jnp.int32, (pw, pw), 1) >> 7)
    first = lax.broadcasted_iota(jnp.int32, (CHUNK, pw), 1) < GDN_DV
    first_row = first[0:1, :]
    dirs = ((wqf_ref, kgf_ref, uf_ref, qkmf_ref, glf_ref, of_ref), (wqb_ref, kgb_ref, ub_ref, qkmb_ref, glb_ref, ob_ref))
    n_pair = GDN_HEADS // 2
    for step in range(n_sub):
        work = []
        for d, refs in enumerate(dirs):
            j = step if d == 0 else n_sub - 1 - step
            work += [(d, p, j, slice(j * CHUNK, (j + 1) * CHUNK), slice(p * pw, (p + 1) * pw)) + refs
                     for p in range(n_pair)]
        ss = [s_ref[d * n_pair + p] for d, p, *_ in work]
        wss = [_dot(wq_ref[0, 2 * j * CHUNK:2 * (j + 1) * CHUNK, ps], s.astype(BF16))
               for s, (d, p, j, rows, ps, wq_ref, *_) in zip(ss, work)]
        vns = [u_ref[0, rows, ps] - ws[:CHUNK]
               for ws, (d, p, j, rows, ps, wq_ref, kg_ref, u_ref, *_) in zip(wss, work)]
        upds = [_dg(kg_ref[0, rows, ps], vn.astype(BF16), _TN)
                for vn, (d, p, j, rows, ps, wq_ref, kg_ref, *_) in zip(vns, work)]
        for s, ws, vn, upd, (d, p, j, rows, ps, wq_ref, kg_ref, u_ref, qkm_ref, gl_ref, o_ref) in zip(
                ss, wss, vns, upds, work):
            vn_bd = jnp.concatenate([jnp.where(first, vn, 0.0), jnp.where(first, 0.0, vn)], axis=0).astype(BF16)
            o_ref[0, rows, ps] = ws[CHUNK:] + _dot(qkm_ref[0, rows, p * 2 * CHUNK:(p + 1) * 2 * CHUNK], vn_bd)
            gl = gl_ref[0, j * CHUNK:j * CHUNK + 1, :]
            l0 = d * GDN_HEADS + 2 * p
            glv = jnp.where(first_row, gl[:, l0:l0 + 1], gl[:, l0 + 1:l0 + 2])
            s_ref[d * n_pair + p] = s * glv + jnp.where(same_head, upd, 0.0)


def _gdn_scan(wq, kg, u, qkm, gl, T):
    B, Tt, _ = kg.shape
    n_lat = T // TOK_BLK
    n_all = Tt // TOK_BLK
    n_ctx = n_all - n_lat

    def fwd(c):
        return jnp.where(c < n_ctx, n_lat + c, c - n_ctx)

    def bwd(c):
        return n_all - 1 - c

    hw = GDN_HEADS * GDN_DK

    def specs(order, d):
        blk = lambda r, w, col: pl.BlockSpec((1, r * TOK_BLK, w), lambda b, c: (b, order(c), col))
        return [blk(2, hw, d), blk(1, hw, d), blk(1, hw, d), blk(1, GDN_HEADS * CHUNK, d), blk(1, LANES, 0)]

    return pl.pallas_call(
        _gdn_scan_kernel,
        grid=(B, n_all),
        in_specs=specs(fwd, 0) + specs(bwd, 1),
        out_specs=[pl.BlockSpec((1, TOK_BLK, hw), lambda b, c: (b, fwd(c), 0)),
                   pl.BlockSpec((1, TOK_BLK, hw), lambda b, c: (b, bwd(c), 0))],
        out_shape=[jax.ShapeDtypeStruct((B, Tt, GDN_V_W), F32)] * 2,
        scratch_shapes=[pltpu.VMEM((N_CHAIN // 2, 2 * GDN_DK, 2 * GDN_DV), F32)],
        compiler_params=_cparams(("parallel", "arbitrary")),
        name="gdn_scan",
    )(wq, kg, u, qkm, gl, wq, kg, u, qkm, gl)


def _fnet_dense_kernel(c_ref, s_ref, u_ref, cc_ref, sc_ref, o_ref):
    u = u_ref[0]
    p = _dot(c_ref[...], u).astype(BF16)
    q = _dot(s_ref[...], u).astype(BF16)
    o_ref[0] = (_dot(p, cc_ref[...]) - _dot(q, sc_ref[...])).astype(o_ref.dtype)


def _fnet_split_kernel(c_ref, s_ref, tc_ref, ts_ref, ue_ref, uo_ref, cc_ref, sc_ref, o_ref):
    c, s = c_ref[...], s_ref[...]
    ue, uo = ue_ref[0], uo_ref[0]
    pe, qe, po, qo = _dot(c, ue), _dot(s, ue), _dot(c, uo), _dot(s, uo)
    tc, ts = tc_ref[...], ts_ref[...]
    a = tc * po - ts * qo
    bq = tc * qo + ts * po
    for half, (p, q) in enumerate(((pe + a, qe + bq), (pe - a, qe - bq))):
        o_ref[0, half] = (_dot(p.astype(BF16), cc_ref[...]) - _dot(q.astype(BF16), sc_ref[...])).astype(o_ref.dtype)


def _dft_mats(n, norm):
    j = jnp.arange(n, dtype=jnp.int32)
    ang = ((j[:, None] * j[None, :]) % n).astype(F32) * (2.0 * math.pi / n)
    return jnp.cos(ang) * norm, jnp.sin(ang) * norm


def _fnet(ub, mats_half, twiddles, mats_ctx, mats_ch, T):
    B, Tt, W = ub.shape
    Lc = Tt - T
    H = T // 2
    cc, sc = mats_ch
    tm = min(512, H)
    const = lambda shape: pl.BlockSpec(shape, lambda m, b: (0, 0))
    rowblk = lambda w: pl.BlockSpec((tm, w), lambda m, b: (m, 0))
    seq = pl.BlockSpec((1, H, W), lambda m, b: (b, 0, 0))
    lat = pl.pallas_call(
        _fnet_split_kernel,
        grid=(H // tm, B),
        in_specs=[rowblk(H), rowblk(H), rowblk(W), rowblk(W), seq, seq, const((W, W)), const((W, W))],
        out_specs=pl.BlockSpec((1, 2, tm, W), lambda m, b: (b, 0, m, 0)),
        out_shape=jax.ShapeDtypeStruct((B, 2, H, W), BF16),
        compiler_params=_cparams(("parallel", "parallel"), VMEM_LIMIT_BIG),
        name="fnet_latent",
    )(mats_half[0], mats_half[1], twiddles[0], twiddles[1], ub[:, 0:T:2], ub[:, 1:T:2], cc, sc)
    cblk = T // Lc
    ctx = pl.pallas_call(
        _fnet_dense_kernel,
        grid=(1, B),
        in_specs=[const((Lc, Lc)), const((Lc, Lc)), pl.BlockSpec((1, Lc, W), lambda m, b: (b, cblk, 0)),
                  const((W, W)), const((W, W))],
        out_specs=pl.BlockSpec((1, Lc, W), lambda m, b: (b, 0, 0)),
        out_shape=jax.ShapeDtypeStruct((B, Lc, W), BF16),
        compiler_params=_cparams(("parallel", "parallel")),
        name="fnet_context",
    )(mats_ctx[0], mats_ctx[1], ub, cc, sc)
    return jnp.concatenate([lat.reshape(B, T, W), ctx], axis=1)


def _attend(q, keys, vals, biases, o_ref, orow):
    for h in range(NA_HEADS):
        hs = slice(h * NA_DH, (h + 1) * NA_DH)
        qh = q[:, hs]
        ss = []
        for kk, bias in zip(keys, biases):
            s = _dg(qh, kk[:, hs], _NT)
            if bias is not None:
                s = s + bias[h]
            ss.append(s)
        m = functools.reduce(jnp.maximum, [jnp.max(s, axis=-1, keepdims=True) for s in ss])
        ps = [jnp.exp(s - m) for s in ss]
        den = sum(jnp.sum(p, axis=-1, keepdims=True) for p in ps)
        o = sum(_dot(p.astype(BF16), vv[:, hs]) for p, vv in zip(ps, vals))
        o_ref[0, orow, hs] = (o / den).astype(o_ref.dtype)


NA_ROWS = 4
NA_KROWS = WIN_R + NA_ROWS - 1


def _na_window_row(g, rows):
    return jnp.clip(g * NA_ROWS - WIN_R // 2, 0, rows - NA_KROWS)


def _na_kernel(q_ref, k_ref, v_ref, bias_ref, o_ref, *, T, Lc, rows):
    g = pl.program_id(1)
    n_groups = rows // NA_ROWS
    kc = k_ref[0, T:T + Lc, :]
    vc = v_ref[0, T:T + Lc, :]

    @pl.when(g < n_groups)
    def _():
        start = pl.multiple_of(_na_window_row(g, rows) * GRID_W, GRID_W)
        nwin = NA_KROWS * GRID_W
        _attend(q_ref[0], [k_ref[0, pl.ds(start, nwin), :], kc], [v_ref[0, pl.ds(start, nwin), :], vc],
                [bias_ref, None], o_ref, slice(None))

    @pl.when(g == n_groups)
    def _():
        _attend(q_ref[0], [kc], [vc], [None], o_ref, slice(None))


def _na(naqkv, bias_tbl, T):
    B, Tt, _ = naqkv.shape
    Lc = Tt - T
    rows = T // GRID_W
    n_groups = rows // NA_ROWS
    assert NA_ROWS * GRID_W == Lc
    uniq, _ = _na_group_offsets(rows)

    def table_of(g):
        g = jnp.minimum(g, n_groups - 1)
        off = g * NA_ROWS - _na_window_row(g, rows)
        return sum(jnp.where(off > u, 1, 0) for u in uniq)

    return pl.pallas_call(
        functools.partial(_na_kernel, T=T, Lc=Lc, rows=rows),
        grid=(B, n_groups + 1),
        in_specs=[
            pl.BlockSpec((1, NA_ROWS * GRID_W, NA_W), lambda b, g: (b, g, 0)),
            pl.BlockSpec((1, Tt, NA_W), lambda b, g: (b, 0, 1)),
            pl.BlockSpec((1, Tt, NA_W), lambda b, g: (b, 0, 2)),
            pl.BlockSpec((None, NA_HEADS, NA_ROWS * GRID_W, NA_KROWS * GRID_W), lambda b, g: (table_of(g), 0, 0, 0)),
        ],
        out_specs=pl.BlockSpec((1, NA_ROWS * GRID_W, NA_W), lambda b, g: (b, g, 0)),
        out_shape=jax.ShapeDtypeStruct((B, Tt, NA_W), BF16),
        compiler_params=_cparams(("parallel", "arbitrary")),
        name="na",
    )(naqkv, naqkv, naqkv, bias_tbl)


def _na_group_offsets(rows):
    offs = [g * NA_ROWS - min(max(g * NA_ROWS - WIN_R // 2, 0), rows - NA_KROWS) for g in range(rows // NA_ROWS)]
    return sorted(set(offs)), offs


def _na_bias_table(rpb, rows):
    uniq, offs = _na_group_offsets(rows)
    qc = jnp.arange(GRID_W)
    c_start = jnp.clip(qc - WIN_C // 2, 0, GRID_W - WIN_C)
    kc = jnp.arange(GRID_W)
    dc = kc[None, :] - qc[:, None] + (WIN_C - 1)
    col_ok = (kc[None, :] >= c_start[:, None]) & (kc[None, :] < c_start[:, None] + WIN_C)
    tables = []
    for off in uniq:
        g = offs.index(off)
        ws = g * NA_ROWS - off
        r = g * NA_ROWS + jnp.arange(NA_ROWS)
        rs = jnp.clip(r - WIN_R // 2, 0, rows - WIN_R)
        krow = ws + jnp.arange(NA_KROWS)
        row_ok = (krow[None, :] >= rs[:, None]) & (krow[None, :] < rs[:, None] + WIN_R)
        dr = jnp.clip(krow[None, :] - r[:, None] + (WIN_R - 1), 0, 2 * WIN_R - 2)
        tbl = rpb[:, dr][:, :, :, jnp.clip(dc, 0, 2 * WIN_C - 2)]
        ok = row_ok[None, :, :, None, None] & col_ok[None, None, None]
        tbl = jnp.where(ok, tbl.astype(F32), NEG_BIG)
        tables.append(jnp.transpose(tbl, (0, 1, 3, 2, 4)).reshape(NA_HEADS, NA_ROWS * GRID_W, NA_KROWS * GRID_W))
    return jnp.stack(tables)


def _merge_kernel(of_ref, ob_ref, z_ref, fb_ref, oc_ref, gr_ref, x_ref, mod_ref, gnw_ref, n2w_ref,
                  wa_ref, wb_ref, wc_ref, wo_ref, wr_ref, x1_ref, h2_ref, aff_ref):
    o = of_ref[0] + ob_ref[0]
    z = z_ref[0].astype(F32)
    ya = jnp.zeros((TOK_BLK, D_MODEL), F32)
    for h in range(GDN_HEADS):
        hs = slice(h * GDN_DV, (h + 1) * GDN_DV)
        oh = o[:, hs]
        on = oh * lax.rsqrt(jnp.mean(oh * oh, axis=-1, keepdims=True) + RMS_EPS) * gnw_ref[...]
        ya = ya + _dot((on * _silu(z[:, hs])).astype(BF16), wa_ref[hs, :])
    yb = _dot(fb_ref[0], wb_ref[...])
    yc = _dot(oc_ref[0], wc_ref[...])
    gate = lambda j: jax.nn.sigmoid(gr_ref[0, :, j * D_MODEL:(j + 1) * D_MODEL].astype(F32))
    y = gate(0) * ya + gate(1) * yb + gate(2) * yc
    x1 = x_ref[0] + mod_ref[2:3, :] * _dot(y.astype(BF16), wo_ref[...])
    x1_ref[0] = x1
    h2 = _rms_mod(x1, n2w_ref[...], mod_ref[3:4, :], mod_ref[4:5, :]).astype(BF16)
    h2_ref[0] = h2
    logits = _dg(wr_ref[...], h2, _NT)
    ex = jnp.exp(logits - jnp.max(logits, axis=0, keepdims=True))
    aff_ref[0] = ex / jnp.sum(ex, axis=0, keepdims=True)


def _merge(o_f, o_b, z, fb, oc, gr, xs, modv, gnw, n2w, wa, wb, wc, wo, wr, n_lat_blk):
    B, Tt, D = xs.shape
    tok = lambda w: pl.BlockSpec((1, TOK_BLK, w), lambda b, t: (b, t, 0))
    const = lambda shape: pl.BlockSpec(shape, lambda b, t: (0, 0))
    return pl.pallas_call(
        _merge_kernel,
        grid=(B, Tt // TOK_BLK),
        in_specs=[
            tok(GDN_V_W), tok(GDN_V_W), tok(GDN_V_W), tok(FNET_W), tok(NA_W), tok(GATE_W), tok(D),
            pl.BlockSpec((None, None, 6, D), lambda b, t: (b, t // n_lat_blk, 0, 0)),
            const((1, GDN_DV)), const((1, D)),
            const((GDN_V_W, D)), const((FNET_W, D)), const((NA_W, D)), const((D, D)),
            const((N_EXPERTS, D)),
        ],
        out_specs=[tok(D), tok(D), pl.BlockSpec((1, N_EXPERTS, TOK_BLK), lambda b, t: (b, 0, t))],
        out_shape=[jax.ShapeDtypeStruct((B, Tt, D), F32), jax.ShapeDtypeStruct((B, Tt, D), BF16),
                   jax.ShapeDtypeStruct((B, N_EXPERTS, Tt), F32)],
        compiler_params=_cparams(("parallel", "parallel"), VMEM_LIMIT_BIG),
        name="merge",
    )(o_f, o_b, z, fb, oc, gr, xs, modv, gnw, n2w, wa, wb, wc, wo, wr)


def _select_kernel(a_ref, pos_ref, wa_ref, lo_ref, cnt_ref, *, T, Lc):
    ii = lax.broadcasted_iota(jnp.int32, (LANES, LANES), 0)
    jj = lax.broadcasted_iota(jnp.int32, (LANES, LANES), 1)
    tri = (ii < jj).astype(BF16)
    cap_lat = EC_CAPACITY_FACTOR * T // N_EXPERTS
    cap_ctx = EC_CAPACITY_FACTOR * Lc // N_EXPERTS
    blk_lane = lax.broadcasted_iota(jnp.int32, (N_EXPERTS, LANES), 1)
    lo_all = jnp.zeros((N_EXPERTS, LANES), F32)
    cnt_all = jnp.zeros((N_EXPERTS, LANES), F32)
    for s0, n, cap, poff in ((0, T, cap_lat, 0), (T, Lc, cap_ctx, cap_lat)):
        a = a_ref[0, :, s0:s0 + n]
        bits = lax.bitcast_convert_type(a, jnp.int32)

        def count(mask):
            return jnp.sum(jnp.where(mask, 1.0, 0.0), axis=1, keepdims=True)

        def radix(i, pref):
            cand = pref | jnp.left_shift(jnp.int32(1), 30 - i)
            return jnp.where(count(bits >= cand) >= cap, cand, pref)

        thr = lax.fori_loop(0, 31, radix, jnp.zeros((N_EXPERTS, 1), jnp.int32))
        gt = bits > thr
        eq = bits == thr
        need = cap - count(gt)
        idx = lax.broadcasted_iota(jnp.int32, (N_EXPERTS, n), 1)
        nbits = max(1, (n - 1).bit_length())

        def tie(i, ans):
            cand = ans | jnp.left_shift(jnp.int32(1), nbits - 1 - i)
            return jnp.where(count(eq & (idx < cand)) < need, cand, ans)

        last = lax.fori_loop(0, nbits, tie, jnp.zeros((N_EXPERTS, 1), jnp.int32))
        sel = gt | (eq & (idx <= last))
        selb = jnp.where(sel, 1.0, 0.0).astype(BF16)
        ti = lax.broadcasted_iota(jnp.int32, (n, LANES), 0)
        tj = lax.broadcasted_iota(jnp.int32, (n, LANES), 1)
        seg_tot = _dot(selb, (ti // LANES == tj).astype(BF16))
        seg_off = _dot(seg_tot.astype(BF16), tri)
        for s in range(n // LANES):
            ls = slice(s * LANES, (s + 1) * LANES)
            within = _dot(selb[:, ls], tri)
            p = (within + seg_off[:, s:s + 1]).astype(jnp.int32) + poff
            pos_ref[0, :, s0 + s * LANES:s0 + (s + 1) * LANES] = jnp.where(sel[:, ls], p, -1)
        wa_ref[0, :, s0:s0 + n] = jnp.where(sel, a, 0.0)
        blk_tot = _dot(selb, ((ti + s0) // TOK_BLK == tj).astype(BF16))
        blk_off = _dot(blk_tot.astype(BF16), tri) + poff
        mine = (blk_lane >= s0 // TOK_BLK) & (blk_lane < (s0 + n) // TOK_BLK)
        lo_all = jnp.where(mine, blk_off, lo_all)
        cnt_all = jnp.where(mine, blk_tot, cnt_all)
    lo_ref[0] = lo_all.astype(jnp.int32)
    cnt_ref[0] = cnt_all.astype(jnp.int32)


def _select(aff, T):
    B, E, Tt = aff.shape
    blk = pl.BlockSpec((1, E, Tt), lambda b: (b, 0, 0))
    meta = pl.BlockSpec((1, E, LANES), lambda b: (b, 0, 0))
    return pl.pallas_call(
        functools.partial(_select_kernel, T=T, Lc=Tt - T),
        grid=(B,),
        in_specs=[blk],
        out_specs=[blk, blk, meta, meta],
        out_shape=[jax.ShapeDtypeStruct((B, E, Tt), jnp.int32), jax.ShapeDtypeStruct((B, E, Tt), F32),
                   jax.ShapeDtypeStruct((B, E, LANES), jnp.int32), jax.ShapeDtypeStruct((B, E, LANES), jnp.int32)],
        compiler_params=_cparams(("parallel",)),
        name="ec_select",
    )(aff)


MOE_WIN = 64


def _moe_windows(lo_ref, cnt_ref, nch):
    b, c = pl.program_id(0), pl.program_id(1)
    w0s = []
    nmax = jnp.int32(0)
    for e in range(N_EXPERTS):
        i = (b * N_EXPERTS + e) * nch + c
        lo, cnt = lo_ref[i], cnt_ref[i]
        w0 = (lo // BF16_SUBLANES) * BF16_SUBLANES
        w0s.append(w0)
        nmax = jnp.maximum(nmax, jnp.where(cnt > 0, (lo - w0 + cnt + MOE_WIN - 1) // MOE_WIN, 0))
    return w0s, nmax


def _window_starts(w0s, i, R):
    return [pl.multiple_of(jnp.minimum(w0 + i * MOE_WIN, R), BF16_SUBLANES) for w0 in w0s]


def _moe_gather_kernel(lo_ref, cnt_ref, h_ref, pos_ref, xe_ref, *, R, nch):
    @pl.when(pl.program_id(1) == 0)
    def _():
        xe_ref[...] = jnp.zeros_like(xe_ref)

    w0s, nmax = _moe_windows(lo_ref, cnt_ref, nch)
    slot0 = lax.broadcasted_iota(jnp.int32, (MOE_WIN, TOK_BLK), 0)

    def win(i, carry):
        r0s = _window_starts(w0s, i, R)
        onehot = jnp.concatenate(
            [jnp.where(pos_ref[e] == slot0 + r0s[e], 1.0, 0.0).astype(BF16) for e in range(N_EXPERTS)], axis=0)
        rows = _dot(onehot, h_ref[0]).astype(BF16)
        for e in range(N_EXPERTS):
            xe_ref[e, pl.ds(r0s[e], MOE_WIN), :] += rows[e * MOE_WIN:(e + 1) * MOE_WIN]
        return carry

    lax.fori_loop(0, nmax, win, 0)


def _moe_ffn_kernel(xe_ref, wgf_ref, wuf_ref, wdf_ref, ye_ref, wg_ref, wu_ref, wd_ref, *, R):
    @pl.when(pl.program_id(1) == 0)
    def _():
        wg_ref[...] = wgf_ref[...].astype(BF16)
        wu_ref[...] = wuf_ref[...].astype(BF16)
        wd_ref[...] = wdf_ref[...].astype(BF16)

    xe = xe_ref[0:R, :]
    fstep = 512
    ye = jnp.zeros((R, D_MODEL), F32)
    for f in range(D_EXPERT // fstep):
        fs = slice(f * fstep, (f + 1) * fstep)
        hid = _silu(_dot(xe, wg_ref[:, fs])) * _dot(xe, wu_ref[:, fs])
        ye = ye + _dot(hid.astype(BF16), wd_ref[fs, :])
    ye_ref[0:R, :] = ye.astype(BF16)
    ye_ref[R:R + MOE_WIN, :] = jnp.zeros((MOE_WIN, D_MODEL), BF16)


def _moe_scatter_kernel(lo_ref, cnt_ref, ye_ref, pos_ref, wa_ref, x_ref, mod_ref, o_ref, *, R, nch):
    w0s, nmax = _moe_windows(lo_ref, cnt_ref, nch)
    slot0 = lax.broadcasted_iota(jnp.int32, (MOE_WIN, TOK_BLK), 0)

    def win(i, acc):
        r0s = _window_starts(w0s, i, R)
        weighted = jnp.concatenate(
            [jnp.where(pos_ref[e] == slot0 + r0s[e], wa_ref[e], 0.0).astype(BF16) for e in range(N_EXPERTS)], axis=0)
        ye = jnp.concatenate([ye_ref[e, pl.ds(r0s[e], MOE_WIN), :] for e in range(N_EXPERTS)], axis=0)
        return acc + _dg(weighted, ye, _TN)

    acc = lax.fori_loop(0, nmax, win, jnp.zeros((TOK_BLK, D_MODEL), F32))
    o_ref[0] = x_ref[0] + mod_ref[5:6, :] * acc


def _moe(h2, pos, waff, lo, cnt, wg, wu, wd, x1, modv, n_lat_blk, n_out_blk):
    B, Tt, D = h2.shape
    E = N_EXPERTS
    nch = Tt // TOK_BLK
    R = EC_CAPACITY_FACTOR * Tt // N_EXPERTS
    RP = R + MOE_WIN
    assert R % BF16_SUBLANES == 0
    pos = pos.reshape(B, E, nch, 1, TOK_BLK)
    waff = waff.reshape(B, E, nch, 1, TOK_BLK)
    lo = lo[:, :, :nch].reshape(-1)
    cnt = cnt[:, :, :nch].reshape(-1)
    one = pl.Buffered(1)
    rowspec = pl.BlockSpec((None, E, None, 1, TOK_BLK), lambda b, c, *_: (b, 0, c, 0, 0))
    tok = pl.BlockSpec((1, TOK_BLK, D), lambda b, c, *_: (b, c, 0))
    slots = pl.BlockSpec((None, E, RP, D), lambda b, c, *_: (b, 0, 0, 0), pipeline_mode=one)
    xe = pl.pallas_call(
        functools.partial(_moe_gather_kernel, R=R, nch=nch),
        grid_spec=pltpu.PrefetchScalarGridSpec(
            num_scalar_prefetch=2, grid=(B, nch), in_specs=[tok, rowspec], out_specs=slots),
        out_shape=jax.ShapeDtypeStruct((B, E, RP, D), BF16),
        compiler_params=_cparams(("parallel", "arbitrary"), VMEM_LIMIT_BIG),
        name="ec_gather",
    )(lo, cnt, h2, pos)
    wspec = lambda r, c: pl.BlockSpec((None, r, c), lambda e, b: (e, 0, 0))
    slot1 = pl.BlockSpec((None, None, RP, D), lambda e, b: (b, e, 0, 0))
    ye = pl.pallas_call(
        functools.partial(_moe_ffn_kernel, R=R),
        grid=(E, B),
        in_specs=[slot1, wspec(D, D_EXPERT), wspec(D, D_EXPERT), wspec(D_EXPERT, D)],
        out_specs=slot1,
        out_shape=jax.ShapeDtypeStruct((B, E, RP, D), BF16),
        scratch_shapes=[pltpu.VMEM((D, D_EXPERT), BF16), pltpu.VMEM((D, D_EXPERT), BF16),
                        pltpu.VMEM((D_EXPERT, D), BF16)],
        compiler_params=_cparams(("parallel", "arbitrary"), VMEM_LIMIT_BIG),
        name="ec_ffn",
    )(xe, wg, wu, wd)
    return pl.pallas_call(
        functools.partial(_moe_scatter_kernel, R=R, nch=nch),
        grid_spec=pltpu.PrefetchScalarGridSpec(
            num_scalar_prefetch=2, grid=(B, n_out_blk),
            in_specs=[slots, rowspec, rowspec, tok,
                      pl.BlockSpec((None, None, 6, D), lambda b, c, *_: (b, c // n_lat_blk, 0, 0))],
            out_specs=tok),
        out_shape=jax.ShapeDtypeStruct((B, n_out_blk * TOK_BLK, D), F32),
        compiler_params=_cparams(("parallel", "arbitrary"), VMEM_LIMIT_BIG),
        name="ec_scatter",
    )(lo, cnt, ye, pos, waff, x1, modv)


def _rope_tables(T, Lc):
    nf = GDN_DK // 4
    inv = ROPE_BASE ** (-jnp.arange(nf, dtype=F32) / nf)
    t = jnp.arange(T)
    ang_r = (t // GRID_W).astype(F32)[:, None] * inv
    ang_c = (t % GRID_W).astype(F32)[:, None] * inv
    cos = jnp.concatenate([jnp.cos(ang_r)] * 2 + [jnp.cos(ang_c)] * 2, axis=-1)
    sin = jnp.concatenate([-jnp.sin(ang_r), jnp.sin(ang_r), -jnp.sin(ang_c), jnp.sin(ang_c)], axis=-1)
    cos = jnp.concatenate([cos, jnp.ones((Lc, GDN_DK), F32)], axis=0)
    sin = jnp.concatenate([sin, jnp.zeros((Lc, GDN_DK), F32)], axis=0)
    return cos, sin


def _reorder_w_in(w_in):
    qa, ka, va, za, aa, ba, ub, qn, kn, vn, gr = jnp.split(
        w_in, [512, 1024, 1536, 2048, 2056, 2064, 2320, 2576, 2832, 3088], axis=-1)
    pad = jnp.zeros(w_in.shape[:-1] + (AB_PAD - 2 * N_DIR * GDN_HEADS,), w_in.dtype)
    return jnp.concatenate([qa, ka, va, za, ub, qn, kn, vn, gr, aa, ba, pad], axis=-1).astype(BF16)


def kernel(x, c, ctx, c_ctx, w_mod, b_mod, norm1_w, norm2_w, w_in, conv_w, a_log, dt_bias, gdn_norm_w,
           na_qn_w, na_kn_w, na_rpb, w_br_a, w_br_b, w_br_c, w_out, w_router, w_e_gate, w_e_up, w_e_down):
    B, T, D = x.shape
    Lc = ctx.shape[1]
    Tt = T + Lc
    assert D == D_MODEL and T % TOK_BLK == 0 and Lc == TOK_BLK and T % Lc == 0
    assert T // GRID_W >= NA_KROWS and (T // GRID_W) % NA_ROWS == 0
    n_lat_blk = T // TOK_BLK

    w_in_r = _reorder_w_in(w_in)
    bf = lambda a: a.astype(BF16)
    w_mod16, wa16, wb16, wc16, wo16 = bf(w_mod), bf(w_br_a), bf(w_br_b), bf(w_br_c), bf(w_out)
    wr16 = bf(jnp.swapaxes(w_router, 1, 2))
    pad8 = AB_PAD - N_DIR * GDN_HEADS
    al = jnp.pad(a_log.reshape(DEPTH, 1, -1), ((0, 0), (0, 0), (0, pad8)))
    dtb = jnp.pad(dt_bias.reshape(DEPTH, 1, -1), ((0, 0), (0, 0), (0, pad8)))
    qnw = jnp.tile(na_qn_w, (1, NA_HEADS))[:, None, :]
    knw = jnp.tile(na_kn_w, (1, NA_HEADS))[:, None, :]
    head_of = jnp.arange(NA_W) // NA_DH
    bd = ((head_of[:, None] == head_of[None, :]).astype(F32) / NA_DH).astype(BF16)
    cos_t, sin_t = _rope_tables(T, Lc)
    mats_half = tuple(bf(m) for m in _dft_mats(T // 2, T ** -0.5))
    kk = jnp.arange(T // 2, dtype=F32)[:, None] * (2.0 * math.pi / T)
    twiddles = (jnp.broadcast_to(jnp.cos(kk), (T // 2, FNET_W)), jnp.broadcast_to(jnp.sin(kk), (T // 2, FNET_W)))
    mats_ctx = tuple(bf(m) for m in _dft_mats(Lc, Lc ** -0.5))
    cch, sch = _dft_mats(FNET_GROUP_CH, FNET_GROUP_CH ** -0.5)
    eye_g = jnp.eye(FNET_GROUPS, dtype=F32)
    mats_ch = (bf(jnp.kron(eye_g, cch)), bf(jnp.kron(eye_g, sch)))

    rows = -(-(B + 1) // 8) * 8
    cs = jnp.zeros((rows, D), F32).at[:B].set(c).at[B].set(c_ctx)
    mod = _modulation(cs, w_mod16, b_mod[:, None, :])
    mod_lat = mod[:, :B].reshape(DEPTH, B, 1, 6, D)
    mod_ctx = jnp.broadcast_to(mod[:, B].reshape(DEPTH, 1, 1, 6, D), (DEPTH, B, 1, 6, D))
    modv = jnp.concatenate([mod_lat, mod_ctx], axis=2)

    xs = jnp.concatenate([x, ctx], axis=1)
    for i in range(DEPTH):
        last = i == DEPTH - 1
        qkv, z, ub, naqkv, gr, gb = _input_proj(xs, modv[i], norm1_w[i][None], w_in_r[i], al[i], dtb[i],
                                                qnw[i], knw[i], bd, n_lat_blk)
        qkvp = _gdn_prep(qkv, conv_w[i], cos_t, sin_t, T)
        o_f, o_b = _gdn_scan(*_gdn_chunk(qkvp, gb), T)
        fb = _fnet(ub, mats_half, twiddles, mats_ctx, mats_ch, T)
        oc = _na(naqkv, _na_bias_table(na_rpb[i], T // GRID_W), T)
        x1, h2, aff = _merge(o_f, o_b, z, fb, oc, gr, xs, modv[i], gdn_norm_w[i][None], norm2_w[i][None],
                             wa16[i], wb16[i], wc16[i], wo16[i], wr16[i], n_lat_blk)
        pos, waff, lo, cnt = _select(aff, T)
        xs = _moe(h2, pos, waff, lo, cnt, w_e_gate[i], w_e_up[i], w_e_down[i], x1, modv[i], n_lat_blk,
                  n_lat_blk if last else Tt // TOK_BLK)
    return xs
```

```python
import functools
import math

import jax
import jax.numpy as jnp
from jax import lax
from jax.experimental import pallas as pl
from jax.experimental.pallas import tpu as pltpu

D_MODEL = 1024
DEPTH = 4
GRID_W = 64
RMS_EPS = 1e-6
ROPE_BASE = 10000.0

GDN_HEADS = 4
GDN_DK = 128
GDN_DV = 128
GDN_QK_W = GDN_HEADS * GDN_DK
GDN_V_W = GDN_HEADS * GDN_DV
CONV_W = 5
CHUNK = 64
N_DIR = 2

FNET_GROUPS = 4
FNET_GROUP_CH = 64
FNET_W = FNET_GROUPS * FNET_GROUP_CH

NA_HEADS = 4
NA_DH = 64
NA_W = NA_HEADS * NA_DH
WIN_R = 8
WIN_C = 16

N_BRANCH = 3
N_EXPERTS = 16
EC_CAPACITY_FACTOR = 2
D_EXPERT = 1024

LANES = 128
BF16_SUBLANES = 16
VMEM_LIMIT_BIG = 56 * 1024 * 1024

TOK_BLK = 256
QKV_W = 2 * GDN_QK_W + GDN_V_W
NAQKV_W = 3 * NA_W
GATE_W = N_BRANCH * D_MODEL
AB_PAD = LANES
N_IN_PAD = QKV_W + GDN_V_W + FNET_W + NAQKV_W + GATE_W + AB_PAD

F32 = jnp.float32
BF16 = jnp.bfloat16
HI = lax.Precision.HIGHEST
NEG_BIG = -1e30

_NT = (((1,), (1,)), ((), ()))
_TN = (((0,), (0,)), ((), ()))


def _dot(a, b, prec=None):
    return jnp.dot(a, b, preferred_element_type=F32, precision=prec)


def _dg(a, b, dims, prec=None):
    return lax.dot_general(a, b, dims, preferred_element_type=F32, precision=prec)


def _silu(v):
    return v * jax.nn.sigmoid(v)


def _cparams(sem, vmem=None):
    return pltpu.CompilerParams(dimension_semantics=sem, vmem_limit_bytes=vmem)


def _mod_kernel(c_ref, w_ref, b_ref, o_ref):
    s = _silu(c_ref[...]).astype(BF16)
    o_ref[0] = _dot(s, w_ref[0]) + b_ref[0]


def _modulation(cs, w_mod, b_mod):
    L, D, N = w_mod.shape
    R = cs.shape[0]
    tn = 1536
    return pl.pallas_call(
        _mod_kernel,
        grid=(L, N // tn),
        in_specs=[
            pl.BlockSpec((R, D), lambda l, n: (0, 0)),
            pl.BlockSpec((1, D, tn), lambda l, n: (l, 0, n)),
            pl.BlockSpec((1, 1, tn), lambda l, n: (l, 0, n)),
        ],
        out_specs=pl.BlockSpec((1, R, tn), lambda l, n: (l, 0, n)),
        out_shape=jax.ShapeDtypeStruct((L, R, N), F32),
        compiler_params=_cparams(("parallel", "parallel")),
        name="modulation",
    )(cs, w_mod, b_mod)


def _rms_mod(x, nw, shift, scale):
    ms = jnp.mean(x * x, axis=-1, keepdims=True)
    return (x * lax.rsqrt(ms + RMS_EPS) * nw) * (1.0 + scale) + shift


def _proj_kernel(x_ref, mod_ref, nw_ref, w_ref, al_ref, dt_ref, qw_ref, kw_ref, bd_ref,
                 qkv_ref, z_ref, ub_ref, naqkv_ref, gr_ref, gb_ref):
    h = _rms_mod(x_ref[0], nw_ref[...], mod_ref[0:1, :], mod_ref[1:2, :]).astype(BF16)

    def mm(lo, width):
        return _dot(h, w_ref[:, lo:lo + width])

    col = 0
    for ref, width in ((qkv_ref, QKV_W), (z_ref, GDN_V_W), (ub_ref, FNET_W)):
        step = 512 if width % 512 == 0 else 256
        for j in range(width // step):
            ref[0, :, j * step:(j + 1) * step] = mm(col + j * step, step).astype(ref.dtype)
        col += width

    def head_rms(a, w):
        sq = a * a
        hi = sq.astype(BF16)
        lo = (sq - hi.astype(F32)).astype(BF16)
        ms = _dot(hi, bd_ref[...]) + _dot(lo, bd_ref[...])
        return a * lax.rsqrt(ms + RMS_EPS) * w

    naqkv_ref[0, :, 0:NA_W] = (head_rms(mm(col, NA_W), qw_ref[...]) * (NA_DH ** -0.5)).astype(BF16)
    naqkv_ref[0, :, NA_W:2 * NA_W] = head_rms(mm(col + NA_W, NA_W), kw_ref[...]).astype(BF16)
    naqkv_ref[0, :, 2 * NA_W:3 * NA_W] = mm(col + 2 * NA_W, NA_W).astype(BF16)
    col += NAQKV_W
    for j in range(GATE_W // 512):
        gr_ref[0, :, j * 512:(j + 1) * 512] = mm(col + j * 512, 512).astype(BF16)
    col += GATE_W
    ab = mm(col, AB_PAD)
    sp_in = ab + dt_ref[...]
    softplus = jnp.maximum(sp_in, 0.0) + jnp.log(1.0 + jnp.exp(-jnp.abs(sp_in)))
    g = -jnp.exp(al_ref[...]) * softplus
    lane = lax.broadcasted_iota(jnp.int32, ab.shape, 1)
    gb_ref[0] = jnp.where(lane < N_DIR * GDN_HEADS, g, jax.nn.sigmoid(ab))


def _input_proj(xs, modv, nw, w_in, al, dtb, qw, kw, bd, n_lat_blk):
    B, Tt, D = xs.shape
    nblk = Tt // TOK_BLK
    tok = lambda w: pl.BlockSpec((1, TOK_BLK, w), lambda b, t: (b, t, 0))
    widths = (QKV_W, GDN_V_W, FNET_W, NAQKV_W, GATE_W)
    return pl.pallas_call(
        _proj_kernel,
        grid=(B, nblk),
        in_specs=[
            tok(D),
            pl.BlockSpec((None, None, 6, D), lambda b, t: (b, t // n_lat_blk, 0, 0)),
            pl.BlockSpec((1, D), lambda b, t: (0, 0)),
            pl.BlockSpec((D, N_IN_PAD), lambda b, t: (0, 0)),
            pl.BlockSpec((1, AB_PAD), lambda b, t: (0, 0)),
            pl.BlockSpec((1, AB_PAD), lambda b, t: (0, 0)),
            pl.BlockSpec((1, NA_W), lambda b, t: (0, 0)),
            pl.BlockSpec((1, NA_W), lambda b, t: (0, 0)),
            pl.BlockSpec((NA_W, NA_W), lambda b, t: (0, 0)),
        ],
        out_specs=[tok(w) for w in widths] + [tok(AB_PAD)],
        out_shape=[jax.ShapeDtypeStruct((B, Tt, w), BF16) for w in widths]
        + [jax.ShapeDtypeStruct((B, Tt, AB_PAD), F32)],
        compiler_params=_cparams(("parallel", "parallel"), VMEM_LIMIT_BIG),
        name="input_proj",
    )(xs, modv, nw, w_in, al, dtb, qw, kw, bd)


def _gdn_prep_kernel(u_ref, cw_ref, cos_ref, sin_ref, o_ref, *, T, Tt):
    j = pl.program_id(1)
    w = cw_ref[...]
    scale = jnp.where(j < GDN_HEADS, GDN_DK ** -0.5, 1.0).astype(F32)
    halo = BF16_SUBLANES
    lane = lax.broadcasted_iota(jnp.int32, (TOK_BLK, LANES), 1)
    exti = lax.broadcasted_iota(jnp.int32, (TOK_BLK + 2 * halo, LANES), 0)

    def conv_silu(start):
        ps = pl.multiple_of(jnp.maximum(start - halo, 0), halo)
        ns = pl.multiple_of(jnp.minimum(start + TOK_BLK, Tt - halo), halo)
        ext = jnp.concatenate([
            u_ref[0, pl.ds(ps, halo), :].astype(F32),
            u_ref[0, pl.ds(start, TOK_BLK), :].astype(F32),
            u_ref[0, pl.ds(ns, halo), :].astype(F32)], axis=0)
        erow = start - halo + exti
        seg_lo = jnp.where(start < T, 0, T)
        seg_hi = jnp.where(start < T, T, Tt)
        ext = jnp.where((erow >= seg_lo) & (erow < seg_hi), ext, 0.0)
        acc = jnp.zeros((TOK_BLK, LANES), F32)
        for d in range(-(CONV_W // 2), CONV_W // 2 + 1):
            acc = acc + ext[halo + d:halo + d + TOK_BLK, :] * w[d + CONV_W // 2:d + CONV_W // 2 + 1, :]
        return _silu(acc)

    def qk_body(c, carry):
        start = pl.multiple_of(c * TOK_BLK, TOK_BLK)
        y = conv_silu(start)
        yn = y * lax.rsqrt(jnp.sum(y * y, axis=-1, keepdims=True) + RMS_EPS)
        partner = jnp.where((lane & 32) == 0, pltpu.roll(yn, LANES - 32, 1), pltpu.roll(yn, 32, 1))
        rows = pl.ds(start, TOK_BLK)
        o_ref[0, rows, :] = (yn * cos_ref[rows, :] + partner * sin_ref[rows, :]) * scale
        return carry

    def v_body(c, carry):
        start = pl.multiple_of(c * TOK_BLK, TOK_BLK)
        o_ref[0, pl.ds(start, TOK_BLK), :] = conv_silu(start)
        return carry

    @pl.when(j < 2 * GDN_HEADS)
    def _():
        lax.fori_loop(0, Tt // TOK_BLK, qk_body, 0)

    @pl.when(j >= 2 * GDN_HEADS)
    def _():
        lax.fori_loop(0, Tt // TOK_BLK, v_body, 0)


def _gdn_prep(qkv, conv_w, cos_t, sin_t, T):
    B, Tt, W = qkv.shape
    return pl.pallas_call(
        functools.partial(_gdn_prep_kernel, T=T, Tt=Tt),
        grid=(B, W // LANES),
        in_specs=[
            pl.BlockSpec((1, Tt, LANES), lambda b, j: (b, 0, j)),
            pl.BlockSpec((CONV_W, LANES), lambda b, j: (0, j)),
            pl.BlockSpec((Tt, LANES), lambda b, j: (0, 0)),
            pl.BlockSpec((Tt, LANES), lambda b, j: (0, 0)),
        ],
        out_specs=pl.BlockSpec((1, Tt, LANES), lambda b, j: (b, 0, j)),
        out_shape=jax.ShapeDtypeStruct((B, Tt, W), F32),
        compiler_params=_cparams(("parallel", "parallel")),
        name="gdn_prep",
    )(qkv, conv_w, cos_t, sin_t)


N_CHAIN = N_DIR * GDN_HEADS


def _unit_tri_inverses(ms, eye, blk16, blk32):
    b = lambda a: a.astype(BF16)
    n1 = [-jnp.where(blk16, m, 0.0) for m in ms]
    n1b = [b(n) for n in n1]
    n2b = [b(_dot(n, n)) for n in n1b]
    n4b = [b(_dot(n, n)) for n in n2b]
    n8b = [b(_dot(n, n)) for n in n4b]
    ps = [eye + n for n in n1]
    for powers in (n2b, n4b, n8b):
        ps = [p + _dot(b(p), n) for p, n in zip(ps, powers)]
    for pick in (lambda m: jnp.where(blk32 & jnp.logical_not(blk16), m, 0.0), lambda m: jnp.where(blk32, 0.0, m)):
        pbs = [b(p) for p in ps]
        ts = [b(_dot(pb, b(pick(m)))) for pb, m in zip(pbs, ms)]
        ps = [p - _dot(t, pb) for p, t, pb in zip(ps, ts, pbs)]
    return ps


def _gdn_chunk_kernel(q_ref, k_ref, v_ref, g_ref, wq_ref, kg_ref, u_ref, qkm_ref, gl_ref):
    HB = GDN_HEADS * CHUNK
    ii = lax.broadcasted_iota(jnp.int32, (HB, HB), 0)
    jj = lax.broadcasted_iota(jnp.int32, (HB, HB), 1)
    eye = (ii == jj).astype(F32)
    blk16 = (ii >> 4) == (jj >> 4)
    blk32 = (ii >> 5) == (jj >> 5)
    blk64 = (ii >> 6) == (jj >> 6)
    masks = ((blk64 & (ii >= jj), blk64 & (ii > jj)), (blk64 & (ii <= jj), blk64 & (ii < jj)))
    row = lax.broadcasted_iota(jnp.int32, (CHUNK, LANES), 0)

    def cumsum_rows(x, backward):
        sh = 1
        while sh < CHUNK:
            if backward:
                x = x + jnp.where(row < CHUNK - sh, pltpu.roll(x, CHUNK - sh, 0), 0.0)
            else:
                x = x + jnp.where(row >= sh, pltpu.roll(x, sh, 0), 0.0)
            sh *= 2
        return x

    ms, rhss, dests = [], [], []
    for j in range(TOK_BLK // CHUNK):
        rows = slice(j * CHUNK, (j + 1) * CHUNK)
        gbv = g_ref[0, rows, :]
        tot_row = jnp.sum(gbv, axis=0, keepdims=True)
        gl_ref[0, rows, :] = jnp.broadcast_to(jnp.exp(tot_row), (CHUNK, LANES))
        stack = lambda ref: jnp.concatenate(
            [ref[0, rows, h * GDN_DK:(h + 1) * GDN_DK] for h in range(GDN_HEADS)], axis=0)
        qst, kst, vst = stack(q_ref), stack(k_ref), stack(v_ref)
        k16 = kst.astype(BF16)
        kk = _dg(k16, k16, _NT)
        qk = _dg(qst.astype(BF16), k16, _NT)
        for d, (le, lt) in enumerate(masks):
            gc_d = cumsum_rows(gbv, d == 1)
            col = lambda a, off: jnp.concatenate(
                [a[:, off + d * GDN_HEADS + h:off + d * GDN_HEADS + h + 1] for h in range(GDN_HEADS)], axis=0)
            gc = col(gc_d, 0)
            beta = col(gbv, N_CHAIN)
            tot = col(jnp.broadcast_to(tot_row, (CHUNK, LANES)), 0)
            gcm = jnp.broadcast_to(gc, (HB, HB))
            dec = jnp.where(le, jnp.exp(jnp.where(le, gcm - gcm.T, 0.0)), 0.0)
            ms.append(jnp.where(lt, beta * kk * dec, 0.0))
            eg = jnp.exp(gc)
            rhss.append(jnp.concatenate([vst * beta, kst * (beta * eg)], axis=1).astype(BF16))
            dests.append((j, d))
            qkm = jnp.where(le, qk * dec, 0.0).astype(BF16)
            qg = (qst * eg).astype(BF16)
            kg = (kst * jnp.exp(tot - gc)).astype(BF16)
            for h in range(GDN_HEADS):
                l = d * GDN_HEADS + h
                hr = slice(h * CHUNK, (h + 1) * CHUNK)
                ls = slice(l * GDN_DK, (l + 1) * GDN_DK)
                wq_ref[0, (2 * j + 1) * CHUNK:(2 * j + 2) * CHUNK, ls] = qg[hr]
                kg_ref[0, rows, ls] = kg[hr]
                qkm_ref[0, rows, l * CHUNK:(l + 1) * CHUNK] = qkm[hr, h * CHUNK:(h + 1) * CHUNK]
    tinvs = _unit_tri_inverses(ms, eye, blk16, blk32)
    sols = [_dot(t.astype(BF16), r) for t, r in zip(tinvs, rhss)]
    for (j, d), sol in zip(dests, sols):
        for h in range(GDN_HEADS):
            hr = slice(h * CHUNK, (h + 1) * CHUNK)
            ls = slice((d * GDN_HEADS + h) * GDN_DK, (d * GDN_HEADS + h + 1) * GDN_DK)
            u_ref[0, j * CHUNK:(j + 1) * CHUNK, ls] = sol[hr, :GDN_DV]
            wq_ref[0, 2 * j * CHUNK:(2 * j + 1) * CHUNK, ls] = sol[hr, GDN_DV:].astype(BF16)


def _gdn_chunk(qkvp, gb):
    B, Tt, _ = qkvp.shape
    cw = N_CHAIN * GDN_DK
    tok = lambda col, w, r=1: pl.BlockSpec((1, r * TOK_BLK, w), lambda b, t: (b, t, col))
    return pl.pallas_call(
        _gdn_chunk_kernel,
        grid=(B, Tt // TOK_BLK),
        in_specs=[tok(0, GDN_QK_W), tok(1, GDN_QK_W), tok(2, GDN_V_W), tok(0, LANES)],
        out_specs=[tok(0, cw, 2), tok(0, cw), tok(0, cw), tok(0, N_CHAIN * CHUNK), tok(0, LANES)],
        out_shape=[jax.ShapeDtypeStruct((B, 2 * Tt, cw), BF16), jax.ShapeDtypeStruct((B, Tt, cw), BF16),
                   jax.ShapeDtypeStruct((B, Tt, cw), F32), jax.ShapeDtypeStruct((B, Tt, N_CHAIN * CHUNK), BF16),
                   jax.ShapeDtypeStruct((B, Tt, LANES), F32)],
        compiler_params=_cparams(("parallel", "parallel")),
        name="gdn_chunk",
    )(qkvp, qkvp, qkvp, gb)


def _gdn_scan_kernel(wqf_ref, kgf_ref, uf_ref, qkmf_ref, glf_ref, wqb_ref, kgb_ref, ub_ref, qkmb_ref, glb_ref,
                     of_ref, ob_ref, s_ref):
    @pl.when(pl.program_id(1) == 0)
    def _():
        s_ref[...] = jnp.zeros_like(s_ref)

    n_sub = TOK_BLK // CHUNK
    pw = 2 * GDN_DK
    same_head = (lax.broadcasted_iota(jnp.int32, (pw, pw), 0) >> 7) == (lax.broadcasted_iota(jnp.int32, (pw, pw), 1) >> 7)
    first = lax.broadcasted_iota(jnp.int32, (CHUNK, pw), 1) < GDN_DV
    first_row = first[0:1, :]
    dirs = ((wqf_ref, kgf_ref, uf_ref, qkmf_ref, glf_ref, of_ref), (wqb_ref, kgb_ref, ub_ref, qkmb_ref, glb_ref, ob_ref))
    n_pair = GDN_HEADS // 2
    for step in range(n_sub):
        work = []
        for d, refs in enumerate(dirs):
            j = step if d == 0 else n_sub - 1 - step
            work += [(d, p, j, slice(j * CHUNK, (j + 1) * CHUNK), slice(p * pw, (p + 1) * pw)) + refs
                     for p in range(n_pair)]
        ss = [s_ref[d * n_pair + p] for d, p, *_ in work]
        wss = [_dot(wq_ref[0, 2 * j * CHUNK:2 * (j + 1) * CHUNK, ps], s.astype(BF16))
               for s, (d, p, j, rows, ps, wq_ref, *_) in zip(ss, work)]
        vns = [u_ref[0, rows, ps] - ws[:CHUNK]
               for ws, (d, p, j, rows, ps, wq_ref, kg_ref, u_ref, *_) in zip(wss, work)]
        upds = [_dg(kg_ref[0, rows, ps], vn.astype(BF16), _TN)
                for vn, (d, p, j, rows, ps, wq_ref, kg_ref, *_) in zip(vns, work)]
        for s, ws, vn, upd, (d, p, j, rows, ps, wq_ref, kg_ref, u_ref, qkm_ref, gl_ref, o_ref) in zip(
                ss, wss, vns, upds, work):
            vn_bd = jnp.concatenate([jnp.where(first, vn, 0.0), jnp.where(first, 0.0, vn)], axis=0).astype(BF16)
            o_ref[0, rows, ps] = ws[CHUNK:] + _dot(qkm_ref[0, rows, p * 2 * CHUNK:(p + 1) * 2 * CHUNK], vn_bd)
            gl = gl_ref[0, j * CHUNK:j * CHUNK + 1, :]
            l0 = d * GDN_HEADS + 2 * p
            glv = jnp.where(first_row, gl[:, l0:l0 + 1], gl[:, l0 + 1:l0 + 2])
            s_ref[d * n_pair + p] = s * glv + jnp.where(same_head, upd, 0.0)


def _gdn_scan(wq, kg, u, qkm, gl, T):
    B, Tt, _ = kg.shape
    n_lat = T // TOK_BLK
    n_all = Tt // TOK_BLK
    n_ctx = n_all - n_lat

    def fwd(c):
        return jnp.where(c < n_ctx, n_lat + c, c - n_ctx)

    def bwd(c):
        return n_all - 1 - c

    hw = GDN_HEADS * GDN_DK

    def specs(order, d):
        blk = lambda r, w, col: pl.BlockSpec((1, r * TOK_BLK, w), lambda b, c: (b, order(c), col))
        return [blk(2, hw, d), blk(1, hw, d), blk(1, hw, d), blk(1, GDN_HEADS * CHUNK, d), blk(1, LANES, 0)]

    return pl.pallas_call(
        _gdn_scan_kernel,
        grid=(B, n_all),
        in_specs=specs(fwd, 0) + specs(bwd, 1),
        out_specs=[pl.BlockSpec((1, TOK_BLK, hw), lambda b, c: (b, fwd(c), 0)),
                   pl.BlockSpec((1, TOK_BLK, hw), lambda b, c: (b, bwd(c), 0))],
        out_shape=[jax.ShapeDtypeStruct((B, Tt, GDN_V_W), F32)] * 2,
        scratch_shapes=[pltpu.VMEM((N_CHAIN // 2, 2 * GDN_DK, 2 * GDN_DV), F32)],
        compiler_params=_cparams(("parallel", "arbitrary")),
        name="gdn_scan",
    )(wq, kg, u, qkm, gl, wq, kg, u, qkm, gl)


def _fnet_dense_kernel(c_ref, s_ref, u_ref, cc_ref, sc_ref, o_ref):
    u = u_ref[0]
    p = _dot(c_ref[...], u).astype(BF16)
    q = _dot(s_ref[...], u).astype(BF16)
    o_ref[0] = (_dot(p, cc_ref[...]) - _dot(q, sc_ref[...])).astype(o_ref.dtype)


def _fnet_split_kernel(c_ref, s_ref, tc_ref, ts_ref, u_ref, cc_ref, sc_ref, o_ref):
    w = cc_ref.shape[0]
    pc = _dot(c_ref[...], u_ref[0])
    qs = _dot(s_ref[...], u_ref[0])
    pe, po, qe, qo = pc[:, :w], pc[:, w:], qs[:, :w], qs[:, w:]
    tc, ts = tc_ref[...], ts_ref[...]
    a = tc * po - ts * qo
    bq = tc * qo + ts * po
    for half, (p, q) in enumerate(((pe + a, qe + bq), (pe - a, qe - bq))):
        o_ref[0, half] = (_dot(p.astype(BF16), cc_ref[...]) - _dot(q.astype(BF16), sc_ref[...])).astype(o_ref.dtype)


def _dft_mats(n, norm):
    j = jnp.arange(n, dtype=jnp.int32)
    ang = ((j[:, None] * j[None, :]) % n).astype(F32) * (2.0 * math.pi / n)
    return jnp.cos(ang) * norm, jnp.sin(ang) * norm


def _fnet(ub, mats_half, twiddles, mats_ctx, mats_ch, T):
    B, Tt, W = ub.shape
    Lc = Tt - T
    H = T // 2
    cc, sc = mats_ch
    tm = min(512, H)
    const = lambda shape: pl.BlockSpec(shape, lambda m, b: (0, 0))
    rowblk = lambda w: pl.BlockSpec((tm, w), lambda m, b: (m, 0))
    pairs = ub.reshape(B, Tt // 2, 2 * W)
    lat = pl.pallas_call(
        _fnet_split_kernel,
        grid=(H // tm, B),
        in_specs=[rowblk(H), rowblk(H), rowblk(W), rowblk(W), pl.BlockSpec((1, H, 2 * W), lambda m, b: (b, 0, 0)),
                  const((W, W)), const((W, W))],
        out_specs=pl.BlockSpec((1, 2, tm, W), lambda m, b: (b, 0, m, 0)),
        out_shape=jax.ShapeDtypeStruct((B, 2, H, W), BF16),
        compiler_params=_cparams(("parallel", "parallel"), VMEM_LIMIT_BIG),
        name="fnet_latent",
    )(mats_half[0], mats_half[1], twiddles[0], twiddles[1], pairs, cc, sc)
    cblk = T // Lc
    ctx = pl.pallas_call(
        _fnet_dense_kernel,
        grid=(1, B),
        in_specs=[const((Lc, Lc)), const((Lc, Lc)), pl.BlockSpec((1, Lc, W), lambda m, b: (b, cblk, 0)),
                  const((W, W)), const((W, W))],
        out_specs=pl.BlockSpec((1, Lc, W), lambda m, b: (b, 0, 0)),
        out_shape=jax.ShapeDtypeStruct((B, Lc, W), BF16),
        compiler_params=_cparams(("parallel", "parallel")),
        name="fnet_context",
    )(mats_ctx[0], mats_ctx[1], ub, cc, sc)
    return jnp.concatenate([lat.reshape(B, T, W), ctx], axis=1)


def _attend(q, keys, vals, biases, o_ref, orow):
    for h in range(NA_HEADS):
        hs = slice(h * NA_DH, (h + 1) * NA_DH)
        qh = q[:, hs]
        ss = []
        for kk, bias in zip(keys, biases):
            s = _dg(qh, kk[:, hs], _NT)
            if bias is not None:
                s = s + bias[h]
            ss.append(s)
        m = functools.reduce(jnp.maximum, [jnp.max(s, axis=-1, keepdims=True) for s in ss])
        ps = [jnp.exp(s - m) for s in ss]
        den = sum(jnp.sum(p, axis=-1, keepdims=True) for p in ps)
        o = sum(_dot(p.astype(BF16), vv[:, hs]) for p, vv in zip(ps, vals))
        o_ref[0, orow, hs] = (o / den).astype(o_ref.dtype)


NA_ROWS = 4
NA_KROWS = WIN_R + NA_ROWS - 1


def _na_window_row(g, rows):
    return jnp.clip(g * NA_ROWS - WIN_R // 2, 0, rows - NA_KROWS)


def _na_kernel(q_ref, k_ref, v_ref, bias_ref, o_ref, *, T, Lc, rows):
    g = pl.program_id(1)
    n_groups = rows // NA_ROWS
    kc = k_ref[0, T:T + Lc, :]
    vc = v_ref[0, T:T + Lc, :]

    @pl.when(g < n_groups)
    def _():
        start = pl.multiple_of(_na_window_row(g, rows) * GRID_W, GRID_W)
        nwin = NA_KROWS * GRID_W
        _attend(q_ref[0], [k_ref[0, pl.ds(start, nwin), :], kc], [v_ref[0, pl.ds(start, nwin), :], vc],
                [bias_ref, None], o_ref, slice(None))

    @pl.when(g == n_groups)
    def _():
        _attend(q_ref[0], [kc], [vc], [None], o_ref, slice(None))


def _na(naqkv, bias_tbl, T):
    B, Tt, _ = naqkv.shape
    Lc = Tt - T
    rows = T // GRID_W
    n_groups = rows // NA_ROWS
    assert NA_ROWS * GRID_W == Lc
    uniq, _ = _na_group_offsets(rows)

    def table_of(g):
        g = jnp.minimum(g, n_groups - 1)
        off = g * NA_ROWS - _na_window_row(g, rows)
        return sum(jnp.where(off > u, 1, 0) for u in uniq)

    return pl.pallas_call(
        functools.partial(_na_kernel, T=T, Lc=Lc, rows=rows),
        grid=(B, n_groups + 1),
        in_specs=[
            pl.BlockSpec((1, NA_ROWS * GRID_W, NA_W), lambda b, g: (b, g, 0)),
            pl.BlockSpec((1, Tt, NA_W), lambda b, g: (b, 0, 1)),
            pl.BlockSpec((1, Tt, NA_W), lambda b, g: (b, 0, 2)),
            pl.BlockSpec((None, NA_HEADS, NA_ROWS * GRID_W, NA_KROWS * GRID_W), lambda b, g: (table_of(g), 0, 0, 0)),
        ],
        out_specs=pl.BlockSpec((1, NA_ROWS * GRID_W, NA_W), lambda b, g: (b, g, 0)),
        out_shape=jax.ShapeDtypeStruct((B, Tt, NA_W), BF16),
        compiler_params=_cparams(("parallel", "arbitrary")),
        name="na",
    )(naqkv, naqkv, naqkv, bias_tbl)


def _na_group_offsets(rows):
    offs = [g * NA_ROWS - min(max(g * NA_ROWS - WIN_R // 2, 0), rows - NA_KROWS) for g in range(rows // NA_ROWS)]
    return sorted(set(offs)), offs


def _na_bias_tables(rpb, rows):
    L = rpb.shape[0]
    uniq, offs = _na_group_offsets(rows)
    qc = jnp.arange(GRID_W)
    c_start = jnp.clip(qc - WIN_C // 2, 0, GRID_W - WIN_C)
    kc = jnp.arange(GRID_W)
    dc = kc[None, :] - qc[:, None] + (WIN_C - 1)
    col_ok = (kc[None, :] >= c_start[:, None]) & (kc[None, :] < c_start[:, None] + WIN_C)
    onehot = (dc[None] == jnp.arange(2 * WIN_C - 1)[:, None, None]).astype(F32).reshape(2 * WIN_C - 1, -1)
    toep = jnp.dot(rpb.astype(F32).reshape(-1, 2 * WIN_C - 1), onehot, precision=HI)
    toep = toep.reshape(L, NA_HEADS, 2 * WIN_R - 1, GRID_W, GRID_W)
    tables = []
    for off in uniq:
        g = offs.index(off)
        ws = g * NA_ROWS - off
        rs = [min(max(g * NA_ROWS + i - WIN_R // 2, 0), rows - WIN_R) for i in range(NA_ROWS)]
        blocks = []
        for i in range(NA_ROWS):
            r = g * NA_ROWS + i
            per_key_row = []
            for a in range(NA_KROWS):
                inside = rs[i] <= ws + a < rs[i] + WIN_R
                dr = ws + a - r + (WIN_R - 1)
                per_key_row.append(jnp.where(col_ok, toep[:, :, dr], NEG_BIG) if inside
                                   else jnp.full((L, NA_HEADS, GRID_W, GRID_W), NEG_BIG, F32))
            blocks.append(jnp.concatenate(per_key_row, axis=-1))
        tables.append(jnp.concatenate(blocks, axis=-2))
    return jnp.stack(tables, axis=1)


def _merge_kernel(of_ref, ob_ref, z_ref, fb_ref, oc_ref, gr_ref, x_ref, mod_ref, gnw_ref, n2w_ref,
                  wa_ref, wb_ref, wc_ref, wo_ref, wr_ref, x1_ref, h2_ref, aff_ref):
    o = of_ref[0] + ob_ref[0]
    z = z_ref[0].astype(F32)
    ya = jnp.zeros((TOK_BLK, D_MODEL), F32)
    for h in range(GDN_HEADS):
        hs = slice(h * GDN_DV, (h + 1) * GDN_DV)
        oh = o[:, hs]
        on = oh * lax.rsqrt(jnp.mean(oh * oh, axis=-1, keepdims=True) + RMS_EPS) * gnw_ref[...]
        ya = ya + _dot((on * _silu(z[:, hs])).astype(BF16), wa_ref[hs, :])
    yb = _dot(fb_ref[0], wb_ref[...])
    yc = _dot(oc_ref[0], wc_ref[...])
    gate = lambda j: jax.nn.sigmoid(gr_ref[0, :, j * D_MODEL:(j + 1) * D_MODEL].astype(F32))
    y = gate(0) * ya + gate(1) * yb + gate(2) * yc
    x1 = x_ref[0] + mod_ref[2:3, :] * _dot(y.astype(BF16), wo_ref[...])
    x1_ref[0] = x1
    h2 = _rms_mod(x1, n2w_ref[...], mod_ref[3:4, :], mod_ref[4:5, :]).astype(BF16)
    h2_ref[0] = h2
    logits = _dg(wr_ref[...], h2, _NT)
    ex = jnp.exp(logits - jnp.max(logits, axis=0, keepdims=True))
    aff_ref[0] = ex / jnp.sum(ex, axis=0, keepdims=True)


def _merge(o_f, o_b, z, fb, oc, gr, xs, modv, gnw, n2w, wa, wb, wc, wo, wr, n_lat_blk):
    B, Tt, D = xs.shape
    tok = lambda w: pl.BlockSpec((1, TOK_BLK, w), lambda b, t: (b, t, 0))
    const = lambda shape: pl.BlockSpec(shape, lambda b, t: (0, 0))
    return pl.pallas_call(
        _merge_kernel,
        grid=(B, Tt // TOK_BLK),
        in_specs=[
            tok(GDN_V_W), tok(GDN_V_W), tok(GDN_V_W), tok(FNET_W), tok(NA_W), tok(GATE_W), tok(D),
            pl.BlockSpec((None, None, 6, D), lambda b, t: (b, t // n_lat_blk, 0, 0)),
            const((1, GDN_DV)), const((1, D)),
            const((GDN_V_W, D)), const((FNET_W, D)), const((NA_W, D)), const((D, D)),
            const((N_EXPERTS, D)),
        ],
        out_specs=[tok(D), tok(D), pl.BlockSpec((1, N_EXPERTS, TOK_BLK), lambda b, t: (b, 0, t))],
        out_shape=[jax.ShapeDtypeStruct((B, Tt, D), F32), jax.ShapeDtypeStruct((B, Tt, D), BF16),
                   jax.ShapeDtypeStruct((B, N_EXPERTS, Tt), F32)],
        compiler_params=_cparams(("parallel", "parallel"), VMEM_LIMIT_BIG),
        name="merge",
    )(o_f, o_b, z, fb, oc, gr, xs, modv, gnw, n2w, wa, wb, wc, wo, wr)


def _select_kernel(a_ref, pos_ref, wa_ref, lo_ref, cnt_ref, *, T, Lc):
    ii = lax.broadcasted_iota(jnp.int32, (LANES, LANES), 0)
    jj = lax.broadcasted_iota(jnp.int32, (LANES, LANES), 1)
    tri = (ii < jj).astype(BF16)
    cap_lat = EC_CAPACITY_FACTOR * T // N_EXPERTS
    cap_ctx = EC_CAPACITY_FACTOR * Lc // N_EXPERTS
    blk_lane = lax.broadcasted_iota(jnp.int32, (N_EXPERTS, LANES), 1)
    lo_all = jnp.zeros((N_EXPERTS, LANES), F32)
    cnt_all = jnp.zeros((N_EXPERTS, LANES), F32)
    for s0, n, cap, poff in ((0, T, cap_lat, 0), (T, Lc, cap_ctx, cap_lat)):
        a = a_ref[0, :, s0:s0 + n]
        bits = lax.bitcast_convert_type(a, jnp.int32)

        def count(mask):
            return jnp.sum(jnp.where(mask, 1.0, 0.0), axis=1, keepdims=True)

        def radix(i, pref):
            cand = pref | jnp.left_shift(jnp.int32(1), 30 - i)
            return jnp.where(count(bits >= cand) >= cap, cand, pref)

        thr = lax.fori_loop(0, 31, radix, jnp.zeros((N_EXPERTS, 1), jnp.int32))
        gt = bits > thr
        eq = bits == thr
        need = cap - count(gt)
        idx = lax.broadcasted_iota(jnp.int32, (N_EXPERTS, n), 1)
        nbits = max(1, (n - 1).bit_length())

        def tie(i, ans):
            cand = ans | jnp.left_shift(jnp.int32(1), nbits - 1 - i)
            return jnp.where(count(eq & (idx < cand)) < need, cand, ans)

        last = lax.fori_loop(0, nbits, tie, jnp.zeros((N_EXPERTS, 1), jnp.int32))
        sel = gt | (eq & (idx <= last))
        selb = jnp.where(sel, 1.0, 0.0).astype(BF16)
        ti = lax.broadcasted_iota(jnp.int32, (n, LANES), 0)
        tj = lax.broadcasted_iota(jnp.int32, (n, LANES), 1)
        seg_tot = _dot(selb, (ti // LANES == tj).astype(BF16))
        seg_off = _dot(seg_tot.astype(BF16), tri)
        for s in range(n // LANES):
            ls = slice(s * LANES, (s + 1) * LANES)
            within = _dot(selb[:, ls], tri)
            p = (within + seg_off[:, s:s + 1]).astype(jnp.int32) + poff
            pos_ref[0, :, s0 + s * LANES:s0 + (s + 1) * LANES] = jnp.where(sel[:, ls], p, -1)
        wa_ref[0, :, s0:s0 + n] = jnp.where(sel, a, 0.0)
        blk_tot = _dot(selb, ((ti + s0) // TOK_BLK == tj).astype(BF16))
        blk_off = _dot(blk_tot.astype(BF16), tri) + poff
        mine = (blk_lane >= s0 // TOK_BLK) & (blk_lane < (s0 + n) // TOK_BLK)
        lo_all = jnp.where(mine, blk_off, lo_all)
        cnt_all = jnp.where(mine, blk_tot, cnt_all)
    lo_ref[0] = lo_all.astype(jnp.int32)
    cnt_ref[0] = cnt_all.astype(jnp.int32)


def _select(aff, T):
    B, E, Tt = aff.shape
    blk = pl.BlockSpec((1, E, Tt), lambda b: (b, 0, 0))
    meta = pl.BlockSpec((1, E, LANES), lambda b: (b, 0, 0))
    return pl.pallas_call(
        functools.partial(_select_kernel, T=T, Lc=Tt - T),
        grid=(B,),
        in_specs=[blk],
        out_specs=[blk, blk, meta, meta],
        out_shape=[jax.ShapeDtypeStruct((B, E, Tt), jnp.int32), jax.ShapeDtypeStruct((B, E, Tt), F32),
                   jax.ShapeDtypeStruct((B, E, LANES), jnp.int32), jax.ShapeDtypeStruct((B, E, LANES), jnp.int32)],
        compiler_params=_cparams(("parallel",)),
        name="ec_select",
    )(aff)


MOE_WIN = 64


def _moe_windows(lo_ref, cnt_ref, nch):
    b, c = pl.program_id(0), pl.program_id(1)
    w0s = []
    nmax = jnp.int32(0)
    for e in range(N_EXPERTS):
        i = (b * N_EXPERTS + e) * nch + c
        lo, cnt = lo_ref[i], cnt_ref[i]
        w0 = (lo // BF16_SUBLANES) * BF16_SUBLANES
        w0s.append(w0)
        nmax = jnp.maximum(nmax, jnp.where(cnt > 0, (lo - w0 + cnt + MOE_WIN - 1) // MOE_WIN, 0))
    return w0s, nmax


def _window_starts(w0s, i, R):
    return [pl.multiple_of(jnp.minimum(w0 + i * MOE_WIN, R), BF16_SUBLANES) for w0 in w0s]


def _moe_gather_kernel(lo_ref, cnt_ref, h_ref, pos_ref, xe_ref, *, R, nch):
    @pl.when(pl.program_id(1) == 0)
    def _():
        xe_ref[...] = jnp.zeros_like(xe_ref)

    w0s, nmax = _moe_windows(lo_ref, cnt_ref, nch)
    slot0 = lax.broadcasted_iota(jnp.int32, (MOE_WIN, TOK_BLK), 0)

    def win(i, carry):
        r0s = _window_starts(w0s, i, R)
        onehot = jnp.concatenate(
            [jnp.where(pos_ref[e] == slot0 + r0s[e], 1.0, 0.0).astype(BF16) for e in range(N_EXPERTS)], axis=0)
        rows = _dot(onehot, h_ref[0]).astype(BF16)
        for e in range(N_EXPERTS):
            xe_ref[e, pl.ds(r0s[e], MOE_WIN), :] += rows[e * MOE_WIN:(e + 1) * MOE_WIN]
        return carry

    lax.fori_loop(0, nmax, win, 0)


def _moe_ffn_kernel(xe_ref, wgf_ref, wuf_ref, wdf_ref, ye_ref, wg_ref, wu_ref, wd_ref, *, R):
    @pl.when(pl.program_id(1) == 0)
    def _():
        wg_ref[...] = wgf_ref[...].astype(BF16)
        wu_ref[...] = wuf_ref[...].astype(BF16)
        wd_ref[...] = wdf_ref[...].astype(BF16)

    xe = xe_ref[0:R, :]
    fstep = 512
    ye = jnp.zeros((R, D_MODEL), F32)
    for f in range(D_EXPERT // fstep):
        fs = slice(f * fstep, (f + 1) * fstep)
        hid = _silu(_dot(xe, wg_ref[:, fs])) * _dot(xe, wu_ref[:, fs])
        ye = ye + _dot(hid.astype(BF16), wd_ref[fs, :])
    ye_ref[0:R, :] = ye.astype(BF16)
    ye_ref[R:R + MOE_WIN, :] = jnp.zeros((MOE_WIN, D_MODEL), BF16)


def _moe_scatter_kernel(lo_ref, cnt_ref, ye_ref, pos_ref, wa_ref, x_ref, mod_ref, o_ref, *, R, nch):
    w0s, nmax = _moe_windows(lo_ref, cnt_ref, nch)
    slot0 = lax.broadcasted_iota(jnp.int32, (MOE_WIN, TOK_BLK), 0)

    def win(i, acc):
        r0s = _window_starts(w0s, i, R)
        weighted = jnp.concatenate(
            [jnp.where(pos_ref[e] == slot0 + r0s[e], wa_ref[e], 0.0).astype(BF16) for e in range(N_EXPERTS)], axis=0)
        ye = jnp.concatenate([ye_ref[e, pl.ds(r0s[e], MOE_WIN), :] for e in range(N_EXPERTS)], axis=0)
        return acc + _dg(weighted, ye, _TN)

    acc = lax.fori_loop(0, nmax, win, jnp.zeros((TOK_BLK, D_MODEL), F32))
    o_ref[0] = x_ref[0] + mod_ref[5:6, :] * acc


def _moe(h2, pos, waff, lo, cnt, wg, wu, wd, layer, x1, modv, n_lat_blk, n_out_blk):
    B, Tt, D = h2.shape
    E = N_EXPERTS
    nch = Tt // TOK_BLK
    R = EC_CAPACITY_FACTOR * Tt // N_EXPERTS
    RP = R + MOE_WIN
    assert R % BF16_SUBLANES == 0
    pos = pos.reshape(B, E, nch, 1, TOK_BLK)
    waff = waff.reshape(B, E, nch, 1, TOK_BLK)
    lo = lo[:, :, :nch].reshape(-1)
    cnt = cnt[:, :, :nch].reshape(-1)
    one = pl.Buffered(1)
    rowspec = pl.BlockSpec((None, E, None, 1, TOK_BLK), lambda b, c, *_: (b, 0, c, 0, 0))
    tok = pl.BlockSpec((1, TOK_BLK, D), lambda b, c, *_: (b, c, 0))
    slots = pl.BlockSpec((None, E, RP, D), lambda b, c, *_: (b, 0, 0, 0), pipeline_mode=one)
    xe = pl.pallas_call(
        functools.partial(_moe_gather_kernel, R=R, nch=nch),
        grid_spec=pltpu.PrefetchScalarGridSpec(
            num_scalar_prefetch=2, grid=(B, nch), in_specs=[tok, rowspec], out_specs=slots),
        out_shape=jax.ShapeDtypeStruct((B, E, RP, D), BF16),
        compiler_params=_cparams(("parallel", "arbitrary"), VMEM_LIMIT_BIG),
        name="ec_gather",
    )(lo, cnt, h2, pos)
    wspec = lambda r, c: pl.BlockSpec((None, None, r, c), lambda e, b: (layer, e, 0, 0))
    slot1 = pl.BlockSpec((None, None, RP, D), lambda e, b: (b, e, 0, 0))
    ye = pl.pallas_call(
        functools.partial(_moe_ffn_kernel, R=R),
        grid=(E, B),
        in_specs=[slot1, wspec(D, D_EXPERT), wspec(D, D_EXPERT), wspec(D_EXPERT, D)],
        out_specs=slot1,
        out_shape=jax.ShapeDtypeStruct((B, E, RP, D), BF16),
        scratch_shapes=[pltpu.VMEM((D, D_EXPERT), BF16), pltpu.VMEM((D, D_EXPERT), BF16),
                        pltpu.VMEM((D_EXPERT, D), BF16)],
        compiler_params=_cparams(("parallel", "arbitrary"), VMEM_LIMIT_BIG),
        name="ec_ffn",
    )(xe, wg, wu, wd)
    return pl.pallas_call(
        functools.partial(_moe_scatter_kernel, R=R, nch=nch),
        grid_spec=pltpu.PrefetchScalarGridSpec(
            num_scalar_prefetch=2, grid=(B, n_out_blk),
            in_specs=[slots, rowspec, rowspec, tok,
                      pl.BlockSpec((None, None, 6, D), lambda b, c, *_: (b, c // n_lat_blk, 0, 0))],
            out_specs=tok),
        out_shape=jax.ShapeDtypeStruct((B, n_out_blk * TOK_BLK, D), F32),
        compiler_params=_cparams(("parallel", "arbitrary"), VMEM_LIMIT_BIG),
        name="ec_scatter",
    )(lo, cnt, ye, pos, waff, x1, modv)


def _rope_tables(T, Lc):
    nf = GDN_DK // 4
    inv = ROPE_BASE ** (-jnp.arange(nf, dtype=F32) / nf)
    t = jnp.arange(T)
    ang_r = (t // GRID_W).astype(F32)[:, None] * inv
    ang_c = (t % GRID_W).astype(F32)[:, None] * inv
    cos = jnp.concatenate([jnp.cos(ang_r)] * 2 + [jnp.cos(ang_c)] * 2, axis=-1)
    sin = jnp.concatenate([-jnp.sin(ang_r), jnp.sin(ang_r), -jnp.sin(ang_c), jnp.sin(ang_c)], axis=-1)
    cos = jnp.concatenate([cos, jnp.ones((Lc, GDN_DK), F32)], axis=0)
    sin = jnp.concatenate([sin, jnp.zeros((Lc, GDN_DK), F32)], axis=0)
    return cos, sin


def _reorder_w_in(w_in):
    qa, ka, va, za, aa, ba, ub, qn, kn, vn, gr = jnp.split(
        w_in, [512, 1024, 1536, 2048, 2056, 2064, 2320, 2576, 2832, 3088], axis=-1)
    pad = jnp.zeros(w_in.shape[:-1] + (AB_PAD - 2 * N_DIR * GDN_HEADS,), w_in.dtype)
    return jnp.concatenate([qa, ka, va, za, ub, qn, kn, vn, gr, aa, ba, pad], axis=-1).astype(BF16)


def kernel(x, c, ctx, c_ctx, w_mod, b_mod, norm1_w, norm2_w, w_in, conv_w, a_log, dt_bias, gdn_norm_w,
           na_qn_w, na_kn_w, na_rpb, w_br_a, w_br_b, w_br_c, w_out, w_router, w_e_gate, w_e_up, w_e_down):
    B, T, D = x.shape
    Lc = ctx.shape[1]
    Tt = T + Lc
    assert D == D_MODEL and T % TOK_BLK == 0 and Lc == TOK_BLK and T % Lc == 0
    assert T // GRID_W >= NA_KROWS and (T // GRID_W) % NA_ROWS == 0
    n_lat_blk = T // TOK_BLK

    w_in_r = _reorder_w_in(w_in)
    bf = lambda a: a.astype(BF16)
    w_mod16, wa16, wb16, wc16, wo16 = bf(w_mod), bf(w_br_a), bf(w_br_b), bf(w_br_c), bf(w_out)
    wr16 = bf(jnp.swapaxes(w_router, 1, 2))
    pad8 = AB_PAD - N_DIR * GDN_HEADS
    al = jnp.pad(a_log.reshape(DEPTH, 1, -1), ((0, 0), (0, 0), (0, pad8)))
    dtb = jnp.pad(dt_bias.reshape(DEPTH, 1, -1), ((0, 0), (0, 0), (0, pad8)))
    qnw = jnp.tile(na_qn_w, (1, NA_HEADS))[:, None, :]
    knw = jnp.tile(na_kn_w, (1, NA_HEADS))[:, None, :]
    head_of = jnp.arange(NA_W) // NA_DH
    bd = ((head_of[:, None] == head_of[None, :]).astype(F32) / NA_DH).astype(BF16)
    cos_t, sin_t = _rope_tables(T, Lc)
    mats_half = tuple(bf(m) for m in _dft_mats(T // 2, T ** -0.5))
    kk = jnp.arange(T // 2, dtype=F32)[:, None] * (2.0 * math.pi / T)
    twiddles = (jnp.broadcast_to(jnp.cos(kk), (T // 2, FNET_W)), jnp.broadcast_to(jnp.sin(kk), (T // 2, FNET_W)))
    mats_ctx = tuple(bf(m) for m in _dft_mats(Lc, Lc ** -0.5))
    cch, sch = _dft_mats(FNET_GROUP_CH, FNET_GROUP_CH ** -0.5)
    eye_g = jnp.eye(FNET_GROUPS, dtype=F32)
    mats_ch = (bf(jnp.kron(eye_g, cch)), bf(jnp.kron(eye_g, sch)))

    rows = -(-(B + 1) // 8) * 8
    cs = jnp.zeros((rows, D), F32).at[:B].set(c).at[B].set(c_ctx)
    mod = _modulation(cs, w_mod16, b_mod[:, None, :])
    mod_lat = mod[:, :B].reshape(DEPTH, B, 1, 6, D)
    mod_ctx = jnp.broadcast_to(mod[:, B].reshape(DEPTH, 1, 1, 6, D), (DEPTH, B, 1, 6, D))
    modv = jnp.concatenate([mod_lat, mod_ctx], axis=2)

    bias_tbls = _na_bias_tables(na_rpb, T // GRID_W)
    xs = jnp.concatenate([x, ctx], axis=1)
    for i in range(DEPTH):
        last = i == DEPTH - 1
        qkv, z, ub, naqkv, gr, gb = _input_proj(xs, modv[i], norm1_w[i][None], w_in_r[i], al[i], dtb[i],
                                                qnw[i], knw[i], bd, n_lat_blk)
        qkvp = _gdn_prep(qkv, conv_w[i], cos_t, sin_t, T)
        o_f, o_b = _gdn_scan(*_gdn_chunk(qkvp, gb), T)
        fb = _fnet(ub, mats_half, twiddles, mats_ctx, mats_ch, T)
        oc = _na(naqkv, bias_tbls[i], T)
        x1, h2, aff = _merge(o_f, o_b, z, fb, oc, gr, xs, modv[i], gdn_norm_w[i][None], norm2_w[i][None],
                             wa16[i], wb16[i], wc16[i], wo16[i], wr16[i], n_lat_blk)
        pos, waff, lo, cnt = _select(aff, T)
        xs = _moe(h2, pos, waff, lo, cnt, w_e_gate, w_e_up, w_e_down, i, x1, modv[i], n_lat_blk,
                  n_lat_blk if last else Tt // TOK_BLK)
    return xs
```

```python
import functools
import math

import jax
import jax.numpy as jnp
from jax import lax
from jax.experimental import pallas as pl
from jax.experimental.pallas import tpu as pltpu

D_MODEL = 1024
DEPTH = 4
GRID_W = 64
RMS_EPS = 1e-6
ROPE_BASE = 10000.0

GDN_HEADS = 4
GDN_DK = 128
GDN_DV = 128
GDN_QK_W = GDN_HEADS * GDN_DK
GDN_V_W = GDN_HEADS * GDN_DV
CONV_W = 5
CHUNK = 64
N_DIR = 2

FNET_GROUPS = 4
FNET_GROUP_CH = 64
FNET_W = FNET_GROUPS * FNET_GROUP_CH

NA_HEADS = 4
NA_DH = 64
NA_W = NA_HEADS * NA_DH
WIN_R = 8
WIN_C = 16

N_BRANCH = 3
N_EXPERTS = 16
EC_CAPACITY_FACTOR = 2
D_EXPERT = 1024

LANES = 128
BF16_SUBLANES = 16
VMEM_LIMIT_BIG = 56 * 1024 * 1024

TOK_BLK = 256
QKV_W = 2 * GDN_QK_W + GDN_V_W
NAQKV_W = 3 * NA_W
GATE_W = N_BRANCH * D_MODEL
AB_PAD = LANES
N_IN_PAD = QKV_W + GDN_V_W + FNET_W + NAQKV_W + GATE_W + AB_PAD

F32 = jnp.float32
BF16 = jnp.bfloat16
HI = lax.Precision.HIGHEST
NEG_BIG = -1e30

_NT = (((1,), (1,)), ((), ()))
_TN = (((0,), (0,)), ((), ()))


def _dot(a, b, prec=None):
    return jnp.dot(a, b, preferred_element_type=F32, precision=prec)


def _dg(a, b, dims, prec=None):
    return lax.dot_general(a, b, dims, preferred_element_type=F32, precision=prec)


def _silu(v):
    return v * jax.nn.sigmoid(v)


def _cparams(sem, vmem=None):
    return pltpu.CompilerParams(dimension_semantics=sem, vmem_limit_bytes=vmem)


def _mod_kernel(c_ref, w_ref, b_ref, o_ref):
    s = _silu(c_ref[...]).astype(BF16)
    o_ref[0] = _dot(s, w_ref[0]) + b_ref[0]


def _modulation(cs, w_mod, b_mod):
    L, D, N = w_mod.shape
    R = cs.shape[0]
    tn = 1536
    return pl.pallas_call(
        _mod_kernel,
        grid=(L, N // tn),
        in_specs=[
            pl.BlockSpec((R, D), lambda l, n: (0, 0)),
            pl.BlockSpec((1, D, tn), lambda l, n: (l, 0, n)),
            pl.BlockSpec((1, 1, tn), lambda l, n: (l, 0, n)),
        ],
        out_specs=pl.BlockSpec((1, R, tn), lambda l, n: (l, 0, n)),
        out_shape=jax.ShapeDtypeStruct((L, R, N), F32),
        compiler_params=_cparams(("parallel", "parallel")),
        name="modulation",
    )(cs, w_mod, b_mod)


def _rms_mod(x, nw, shift, scale):
    ms = jnp.mean(x * x, axis=-1, keepdims=True)
    return (x * lax.rsqrt(ms + RMS_EPS) * nw) * (1.0 + scale) + shift


def _proj_kernel(x_ref, mod_ref, nw_ref, w_ref, al_ref, dt_ref, qw_ref, kw_ref, bd_ref,
                 qkv_ref, z_ref, ub_ref, naqkv_ref, gr_ref, gb_ref):
    h = _rms_mod(x_ref[0], nw_ref[...], mod_ref[0:1, :], mod_ref[1:2, :]).astype(BF16)

    def mm(lo, width):
        return _dot(h, w_ref[:, lo:lo + width])

    col = 0
    for ref, width in ((qkv_ref, QKV_W), (z_ref, GDN_V_W), (ub_ref, FNET_W)):
        step = 512 if width % 512 == 0 else 256
        for j in range(width // step):
            ref[0, :, j * step:(j + 1) * step] = mm(col + j * step, step).astype(ref.dtype)
        col += width

    def head_rms(a, w):
        sq = a * a
        hi = sq.astype(BF16)
        lo = (sq - hi.astype(F32)).astype(BF16)
        ms = _dot(hi, bd_ref[...]) + _dot(lo, bd_ref[...])
        return a * lax.rsqrt(ms + RMS_EPS) * w

    naqkv_ref[0, :, 0:NA_W] = (head_rms(mm(col, NA_W), qw_ref[...]) * (NA_DH ** -0.5)).astype(BF16)
    naqkv_ref[0, :, NA_W:2 * NA_W] = head_rms(mm(col + NA_W, NA_W), kw_ref[...]).astype(BF16)
    naqkv_ref[0, :, 2 * NA_W:3 * NA_W] = mm(col + 2 * NA_W, NA_W).astype(BF16)
    col += NAQKV_W
    for j in range(GATE_W // 512):
        gr_ref[0, :, j * 512:(j + 1) * 512] = mm(col + j * 512, 512).astype(BF16)
    col += GATE_W
    ab = mm(col, AB_PAD)
    sp_in = ab + dt_ref[...]
    softplus = jnp.maximum(sp_in, 0.0) + jnp.log(1.0 + jnp.exp(-jnp.abs(sp_in)))
    g = -jnp.exp(al_ref[...]) * softplus
    lane = lax.broadcasted_iota(jnp.int32, ab.shape, 1)
    gb_ref[0] = jnp.where(lane < N_DIR * GDN_HEADS, g, jax.nn.sigmoid(ab))


def _input_proj(xs, modv, nw, w_in, al, dtb, qw, kw, bd, n_lat_blk):
    B, Tt, D = xs.shape
    nblk = Tt // TOK_BLK
    tok = lambda w: pl.BlockSpec((1, TOK_BLK, w), lambda b, t: (b, t, 0))
    widths = (QKV_W, GDN_V_W, FNET_W, NAQKV_W, GATE_W)
    return pl.pallas_call(
        _proj_kernel,
        grid=(B, nblk),
        in_specs=[
            tok(D),
            pl.BlockSpec((None, None, 6, D), lambda b, t: (b, t // n_lat_blk, 0, 0)),
            pl.BlockSpec((1, D), lambda b, t: (0, 0)),
            pl.BlockSpec((D, N_IN_PAD), lambda b, t: (0, 0)),
            pl.BlockSpec((1, AB_PAD), lambda b, t: (0, 0)),
            pl.BlockSpec((1, AB_PAD), lambda b, t: (0, 0)),
            pl.BlockSpec((1, NA_W), lambda b, t: (0, 0)),
            pl.BlockSpec((1, NA_W), lambda b, t: (0, 0)),
            pl.BlockSpec((NA_W, NA_W), lambda b, t: (0, 0)),
        ],
        out_specs=[tok(w) for w in widths] + [tok(AB_PAD)],
        out_shape=[jax.ShapeDtypeStruct((B, Tt, w), BF16) for w in widths]
        + [jax.ShapeDtypeStruct((B, Tt, AB_PAD), F32)],
        compiler_params=_cparams(("parallel", "parallel"), VMEM_LIMIT_BIG),
        name="input_proj",
    )(xs, modv, nw, w_in, al, dtb, qw, kw, bd)


def _gdn_prep_kernel(u_ref, cw_ref, cos_ref, sin_ref, o_ref, *, T, Tt):
    j = pl.program_id(1)
    w = cw_ref[...]
    scale = jnp.where(j < GDN_HEADS, GDN_DK ** -0.5, 1.0).astype(F32)
    halo = BF16_SUBLANES
    lane = lax.broadcasted_iota(jnp.int32, (TOK_BLK, LANES), 1)
    exti = lax.broadcasted_iota(jnp.int32, (TOK_BLK + 2 * halo, LANES), 0)

    def conv_silu(start):
        ps = pl.multiple_of(jnp.maximum(start - halo, 0), halo)
        ns = pl.multiple_of(jnp.minimum(start + TOK_BLK, Tt - halo), halo)
        ext = jnp.concatenate([
            u_ref[0, pl.ds(ps, halo), :].astype(F32),
            u_ref[0, pl.ds(start, TOK_BLK), :].astype(F32),
            u_ref[0, pl.ds(ns, halo), :].astype(F32)], axis=0)
        erow = start - halo + exti
        seg_lo = jnp.where(start < T, 0, T)
        seg_hi = jnp.where(start < T, T, Tt)
        ext = jnp.where((erow >= seg_lo) & (erow < seg_hi), ext, 0.0)
        acc = jnp.zeros((TOK_BLK, LANES), F32)
        for d in range(-(CONV_W // 2), CONV_W // 2 + 1):
            acc = acc + ext[halo + d:halo + d + TOK_BLK, :] * w[d + CONV_W // 2:d + CONV_W // 2 + 1, :]
        return _silu(acc)

    def qk_body(c, carry):
        start = pl.multiple_of(c * TOK_BLK, TOK_BLK)
        y = conv_silu(start)
        yn = y * lax.rsqrt(jnp.sum(y * y, axis=-1, keepdims=True) + RMS_EPS)
        partner = jnp.where((lane & 32) == 0, pltpu.roll(yn, LANES - 32, 1), pltpu.roll(yn, 32, 1))
        rows = pl.ds(start, TOK_BLK)
        o_ref[0, rows, :] = (yn * cos_ref[rows, :] + partner * sin_ref[rows, :]) * scale
        return carry

    def v_body(c, carry):
        start = pl.multiple_of(c * TOK_BLK, TOK_BLK)
        o_ref[0, pl.ds(start, TOK_BLK), :] = conv_silu(start)
        return carry

    @pl.when(j < 2 * GDN_HEADS)
    def _():
        lax.fori_loop(0, Tt // TOK_BLK, qk_body, 0)

    @pl.when(j >= 2 * GDN_HEADS)
    def _():
        lax.fori_loop(0, Tt // TOK_BLK, v_body, 0)


def _gdn_prep(qkv, conv_w, cos_t, sin_t, T):
    B, Tt, W = qkv.shape
    return pl.pallas_call(
        functools.partial(_gdn_prep_kernel, T=T, Tt=Tt),
        grid=(B, W // LANES),
        in_specs=[
            pl.BlockSpec((1, Tt, LANES), lambda b, j: (b, 0, j)),
            pl.BlockSpec((CONV_W, LANES), lambda b, j: (0, j)),
            pl.BlockSpec((Tt, LANES), lambda b, j: (0, 0)),
            pl.BlockSpec((Tt, LANES), lambda b, j: (0, 0)),
        ],
        out_specs=pl.BlockSpec((1, Tt, LANES), lambda b, j: (b, 0, j)),
        out_shape=jax.ShapeDtypeStruct((B, Tt, W), F32),
        compiler_params=_cparams(("parallel", "parallel")),
        name="gdn_prep",
    )(qkv, conv_w, cos_t, sin_t)


N_CHAIN = N_DIR * GDN_HEADS


def _unit_tri_inverses(ms, eye, blk16, blk32):
    b = lambda a: a.astype(BF16)
    n1 = [-jnp.where(blk16, m, 0.0) for m in ms]
    n1b = [b(n) for n in n1]
    n2b = [b(_dot(n, n)) for n in n1b]
    n4b = [b(_dot(n, n)) for n in n2b]
    n8b = [b(_dot(n, n)) for n in n4b]
    ps = [eye + n for n in n1]
    for powers in (n2b, n4b, n8b):
        ps = [p + _dot(b(p), n) for p, n in zip(ps, powers)]
    for pick in (lambda m: jnp.where(blk32 & jnp.logical_not(blk16), m, 0.0), lambda m: jnp.where(blk32, 0.0, m)):
        pbs = [b(p) for p in ps]
        ts = [b(_dot(pb, b(pick(m)))) for pb, m in zip(pbs, ms)]
        ps = [p - _dot(t, pb) for p, t, pb in zip(ps, ts, pbs)]
    return ps


def _gdn_chunk_kernel(q_ref, k_ref, v_ref, g_ref, wq_ref, kg_ref, u_ref, qkm_ref, gl_ref):
    HB = GDN_HEADS * CHUNK
    ii = lax.broadcasted_iota(jnp.int32, (HB, HB), 0)
    jj = lax.broadcasted_iota(jnp.int32, (HB, HB), 1)
    eye = (ii == jj).astype(F32)
    blk16 = (ii >> 4) == (jj >> 4)
    blk32 = (ii >> 5) == (jj >> 5)
    blk64 = (ii >> 6) == (jj >> 6)
    masks = ((blk64 & (ii >= jj), blk64 & (ii > jj)), (blk64 & (ii <= jj), blk64 & (ii < jj)))
    row = lax.broadcasted_iota(jnp.int32, (CHUNK, LANES), 0)

    def cumsum_rows(x, backward):
        sh = 1
        while sh < CHUNK:
            if backward:
                x = x + jnp.where(row < CHUNK - sh, pltpu.roll(x, CHUNK - sh, 0), 0.0)
            else:
                x = x + jnp.where(row >= sh, pltpu.roll(x, sh, 0), 0.0)
            sh *= 2
        return x

    ms, rhss, dests = [], [], []
    for j in range(TOK_BLK // CHUNK):
        rows = slice(j * CHUNK, (j + 1) * CHUNK)
        gbv = g_ref[0, rows, :]
        tot_row = jnp.sum(gbv, axis=0, keepdims=True)
        gl_ref[0, rows, :] = jnp.broadcast_to(jnp.exp(tot_row), (CHUNK, LANES))
        stack = lambda ref: jnp.concatenate(
            [ref[0, rows, h * GDN_DK:(h + 1) * GDN_DK] for h in range(GDN_HEADS)], axis=0)
        qst, kst, vst = stack(q_ref), stack(k_ref), stack(v_ref)
        k16 = kst.astype(BF16)
        kk = _dg(k16, k16, _NT)
        qk = _dg(qst.astype(BF16), k16, _NT)
        for d, (le, lt) in enumerate(masks):
            gc_d = cumsum_rows(gbv, d == 1)
            col = lambda a, off: jnp.concatenate(
                [a[:, off + d * GDN_HEADS + h:off + d * GDN_HEADS + h + 1] for h in range(GDN_HEADS)], axis=0)
            gc = col(gc_d, 0)
            beta = col(gbv, N_CHAIN)
            tot = col(jnp.broadcast_to(tot_row, (CHUNK, LANES)), 0)
            gcm = jnp.broadcast_to(gc, (HB, HB))
            dec = jnp.where(le, jnp.exp(jnp.where(le, gcm - gcm.T, 0.0)), 0.0)
            ms.append(jnp.where(lt, beta * kk * dec, 0.0))
            eg = jnp.exp(gc)
            rhss.append(jnp.concatenate([vst * beta, kst * (beta * eg)], axis=1).astype(BF16))
            dests.append((j, d))
            qkm = jnp.where(le, qk * dec, 0.0).astype(BF16)
            qg = (qst * eg).astype(BF16)
            kg = (kst * jnp.exp(tot - gc)).astype(BF16)
            for h in range(GDN_HEADS):
                l = d * GDN_HEADS + h
                hr = slice(h * CHUNK, (h + 1) * CHUNK)
                ls = slice(l * GDN_DK, (l + 1) * GDN_DK)
                wq_ref[0, (2 * j + 1) * CHUNK:(2 * j + 2) * CHUNK, ls] = qg[hr]
                kg_ref[0, rows, ls] = kg[hr]
                qkm_ref[0, rows, l * CHUNK:(l + 1) * CHUNK] = qkm[hr, h * CHUNK:(h + 1) * CHUNK]
    tinvs = _unit_tri_inverses(ms, eye, blk16, blk32)
    sols = [_dot(t.astype(BF16), r) for t, r in zip(tinvs, rhss)]
    for (j, d), sol in zip(dests, sols):
        for h in range(GDN_HEADS):
            hr = slice(h * CHUNK, (h + 1) * CHUNK)
            ls = slice((d * GDN_HEADS + h) * GDN_DK, (d * GDN_HEADS + h + 1) * GDN_DK)
            u_ref[0, j * CHUNK:(j + 1) * CHUNK, ls] = sol[hr, :GDN_DV]
            wq_ref[0, 2 * j * CHUNK:(2 * j + 1) * CHUNK, ls] = sol[hr, GDN_DV:].astype(BF16)


def _gdn_chunk(qkvp, gb):
    B, Tt, _ = qkvp.shape
    cw = N_CHAIN * GDN_DK
    tok = lambda col, w, r=1: pl.BlockSpec((1, r * TOK_BLK, w), lambda b, t: (b, t, col))
    return pl.pallas_call(
        _gdn_chunk_kernel,
        grid=(B, Tt // TOK_BLK),
        in_specs=[tok(0, GDN_QK_W), tok(1, GDN_QK_W), tok(2, GDN_V_W), tok(0, LANES)],
        out_specs=[tok(0, cw, 2), tok(0, cw), tok(0, cw), tok(0, N_CHAIN * CHUNK), tok(0, LANES)],
        out_shape=[jax.ShapeDtypeStruct((B, 2 * Tt, cw), BF16), jax.ShapeDtypeStruct((B, Tt, cw), BF16),
                   jax.ShapeDtypeStruct((B, Tt, cw), F32), jax.ShapeDtypeStruct((B, Tt, N_CHAIN * CHUNK), BF16),
                   jax.ShapeDtypeStruct((B, Tt, LANES), F32)],
        compiler_params=_cparams(("parallel", "parallel")),
        name="gdn_chunk",
    )(qkvp, qkvp, qkvp, gb)


def _gdn_scan_kernel(wqf_ref, kgf_ref, uf_ref, qkmf_ref, glf_ref, wqb_ref, kgb_ref, ub_ref, qkmb_ref, glb_ref,
                     of_ref, ob_ref, s_ref):
    @pl.when(pl.program_id(1) == 0)
    def _():
        s_ref[...] = jnp.zeros_like(s_ref)

    n_sub = TOK_BLK // CHUNK
    pw = 2 * GDN_DK
    same_head = (lax.broadcasted_iota(jnp.int32, (pw, pw), 0) >> 7) == (lax.broadcasted_iota(jnp.int32, (pw, pw), 1) >> 7)
    first = lax.broadcasted_iota(jnp.int32, (CHUNK, pw), 1) < GDN_DV
    first_row = first[0:1, :]
    dirs = ((wqf_ref, kgf_ref, uf_ref, qkmf_ref, glf_ref, of_ref), (wqb_ref, kgb_ref, ub_ref, qkmb_ref, glb_ref, ob_ref))
    n_pair = GDN_HEADS // 2
    for step in range(n_sub):
        work = []
        for d, refs in enumerate(dirs):
            j = step if d == 0 else n_sub - 1 - step
            work += [(d, p, j, slice(j * CHUNK, (j + 1) * CHUNK), slice(p * pw, (p + 1) * pw)) + refs
                     for p in range(n_pair)]
        ss = [s_ref[d * n_pair + p] for d, p, *_ in work]
        wss = [_dot(wq_ref[0, 2 * j * CHUNK:2 * (j + 1) * CHUNK, ps], s.astype(BF16))
               for s, (d, p, j, rows, ps, wq_ref, *_) in zip(ss, work)]
        vns = [u_ref[0, rows, ps] - ws[:CHUNK]
               for ws, (d, p, j, rows, ps, wq_ref, kg_ref, u_ref, *_) in zip(wss, work)]
        upds = [_dg(kg_ref[0, rows, ps], vn.astype(BF16), _TN)
                for vn, (d, p, j, rows, ps, wq_ref, kg_ref, *_) in zip(vns, work)]
        for s, ws, vn, upd, (d, p, j, rows, ps, wq_ref, kg_ref, u_ref, qkm_ref, gl_ref, o_ref) in zip(
                ss, wss, vns, upds, work):
            vn_bd = jnp.concatenate([jnp.where(first, vn, 0.0), jnp.where(first, 0.0, vn)], axis=0).astype(BF16)
            o_ref[0, rows, ps] = ws[CHUNK:] + _dot(qkm_ref[0, rows, p * 2 * CHUNK:(p + 1) * 2 * CHUNK], vn_bd)
            gl = gl_ref[0, j * CHUNK:j * CHUNK + 1, :]
            l0 = d * GDN_HEADS + 2 * p
            glv = jnp.where(first_row, gl[:, l0:l0 + 1], gl[:, l0 + 1:l0 + 2])
            s_ref[d * n_pair + p] = s * glv + jnp.where(same_head, upd, 0.0)


def _gdn_scan(wq, kg, u, qkm, gl, T):
    B, Tt, _ = kg.shape
    n_lat = T // TOK_BLK
    n_all = Tt // TOK_BLK
    n_ctx = n_all - n_lat

    def fwd(c):
        return jnp.where(c < n_ctx, n_lat + c, c - n_ctx)

    def bwd(c):
        return n_all - 1 - c

    hw = GDN_HEADS * GDN_DK

    def specs(order, d):
        blk = lambda r, w, col: pl.BlockSpec((1, r * TOK_BLK, w), lambda b, c: (b, order(c), col))
        return [blk(2, hw, d), blk(1, hw, d), blk(1, hw, d), blk(1, GDN_HEADS * CHUNK, d), blk(1, LANES, 0)]

    return pl.pallas_call(
        _gdn_scan_kernel,
        grid=(B, n_all),
        in_specs=specs(fwd, 0) + specs(bwd, 1),
        out_specs=[pl.BlockSpec((1, TOK_BLK, hw), lambda b, c: (b, fwd(c), 0)),
                   pl.BlockSpec((1, TOK_BLK, hw), lambda b, c: (b, bwd(c), 0))],
        out_shape=[jax.ShapeDtypeStruct((B, Tt, GDN_V_W), F32)] * 2,
        scratch_shapes=[pltpu.VMEM((N_CHAIN // 2, 2 * GDN_DK, 2 * GDN_DV), F32)],
        compiler_params=_cparams(("parallel", "arbitrary")),
        name="gdn_scan",
    )(wq, kg, u, qkm, gl, wq, kg, u, qkm, gl)


def _fnet_dense_kernel(c_ref, s_ref, u_ref, cc_ref, sc_ref, o_ref):
    u = u_ref[0]
    p = _dot(c_ref[...], u).astype(BF16)
    q = _dot(s_ref[...], u).astype(BF16)
    o_ref[0] = (_dot(p, cc_ref[...]) - _dot(q, sc_ref[...])).astype(o_ref.dtype)


def _fnet_split_kernel(c_ref, s_ref, tc_ref, ts_ref, u_ref, cc_ref, sc_ref, o_ref):
    w = cc_ref.shape[0]
    pc = _dot(c_ref[...], u_ref[0])
    qs = _dot(s_ref[...], u_ref[0])
    pe, po, qe, qo = pc[:, :w], pc[:, w:], qs[:, :w], qs[:, w:]
    tc, ts = tc_ref[...], ts_ref[...]
    a = tc * po - ts * qo
    bq = tc * qo + ts * po
    for half, (p, q) in enumerate(((pe + a, qe + bq), (pe - a, qe - bq))):
        o_ref[0, half] = (_dot(p.astype(BF16), cc_ref[...]) - _dot(q.astype(BF16), sc_ref[...])).astype(o_ref.dtype)


def _dft_mats(n, norm):
    j = jnp.arange(n, dtype=jnp.int32)
    ang = ((j[:, None] * j[None, :]) % n).astype(F32) * (2.0 * math.pi / n)
    return jnp.cos(ang) * norm, jnp.sin(ang) * norm


def _fnet(ub, mats_half, twiddles, mats_ctx, mats_ch, T):
    B, Tt, W = ub.shape
    Lc = Tt - T
    H = T // 2
    cc, sc = mats_ch
    tm = min(512, H)
    const = lambda shape: pl.BlockSpec(shape, lambda m, b: (0, 0))
    rowblk = lambda w: pl.BlockSpec((tm, w), lambda m, b: (m, 0))
    pairs = ub.reshape(B, Tt // 2, 2 * W)
    lat = pl.pallas_call(
        _fnet_split_kernel,
        grid=(H // tm, B),
        in_specs=[rowblk(H), rowblk(H), rowblk(W), rowblk(W), pl.BlockSpec((1, H, 2 * W), lambda m, b: (b, 0, 0)),
                  const((W, W)), const((W, W))],
        out_specs=pl.BlockSpec((1, 2, tm, W), lambda m, b: (b, 0, m, 0)),
        out_shape=jax.ShapeDtypeStruct((B, 2, H, W), BF16),
        compiler_params=_cparams(("parallel", "parallel"), VMEM_LIMIT_BIG),
        name="fnet_latent",
    )(mats_half[0], mats_half[1], twiddles[0], twiddles[1], pairs, cc, sc)
    cblk = T // Lc
    ctx = pl.pallas_call(
        _fnet_dense_kernel,
        grid=(1, B),
        in_specs=[const((Lc, Lc)), const((Lc, Lc)), pl.BlockSpec((1, Lc, W), lambda m, b: (b, cblk, 0)),
                  const((W, W)), const((W, W))],
        out_specs=pl.BlockSpec((1, Lc, W), lambda m, b: (b, 0, 0)),
        out_shape=jax.ShapeDtypeStruct((B, Lc, W), BF16),
        compiler_params=_cparams(("parallel", "parallel")),
        name="fnet_context",
    )(mats_ctx[0], mats_ctx[1], ub, cc, sc)
    return jnp.concatenate([lat.reshape(B, T, W), ctx], axis=1)


def _attend(q, keys, vals, biases, o_ref, orow):
    for h in range(NA_HEADS):
        hs = slice(h * NA_DH, (h + 1) * NA_DH)
        qh = q[:, hs]
        ss = []
        for kk, bias in zip(keys, biases):
            s = _dg(qh, kk[:, hs], _NT)
            if bias is not None:
                s = s + bias[h]
            ss.append(s)
        m = functools.reduce(jnp.maximum, [jnp.max(s, axis=-1, keepdims=True) for s in ss])
        ps = [jnp.exp(s - m) for s in ss]
        den = sum(jnp.sum(p, axis=-1, keepdims=True) for p in ps)
        o = sum(_dot(p.astype(BF16), vv[:, hs]) for p, vv in zip(ps, vals))
        o_ref[0, orow, hs] = (o / den).astype(o_ref.dtype)


NA_ROWS = 4
NA_KROWS = WIN_R + NA_ROWS - 1


def _na_window_row(g, rows):
    return jnp.clip(g * NA_ROWS - WIN_R // 2, 0, rows - NA_KROWS)


def _na_kernel(q_ref, k_ref, v_ref, bias_ref, o_ref, *, T, Lc, rows):
    g = pl.program_id(1)
    n_groups = rows // NA_ROWS
    kc = k_ref[0, T:T + Lc, :]
    vc = v_ref[0, T:T + Lc, :]

    @pl.when(g < n_groups)
    def _():
        start = pl.multiple_of(_na_window_row(g, rows) * GRID_W, GRID_W)
        nwin = NA_KROWS * GRID_W
        _attend(q_ref[0], [k_ref[0, pl.ds(start, nwin), :], kc], [v_ref[0, pl.ds(start, nwin), :], vc],
                [bias_ref, None], o_ref, slice(None))

    @pl.when(g == n_groups)
    def _():
        _attend(q_ref[0], [kc], [vc], [None], o_ref, slice(None))


def _na(naqkv, bias_tbl, T):
    B, Tt, _ = naqkv.shape
    Lc = Tt - T
    rows = T // GRID_W
    n_groups = rows // NA_ROWS
    assert NA_ROWS * GRID_W == Lc
    uniq, _ = _na_group_offsets(rows)

    def table_of(g):
        g = jnp.minimum(g, n_groups - 1)
        off = g * NA_ROWS - _na_window_row(g, rows)
        return sum(jnp.where(off > u, 1, 0) for u in uniq)

    return pl.pallas_call(
        functools.partial(_na_kernel, T=T, Lc=Lc, rows=rows),
        grid=(B, n_groups + 1),
        in_specs=[
            pl.BlockSpec((1, NA_ROWS * GRID_W, NA_W), lambda b, g: (b, g, 0)),
            pl.BlockSpec((1, Tt, NA_W), lambda b, g: (b, 0, 1)),
            pl.BlockSpec((1, Tt, NA_W), lambda b, g: (b, 0, 2)),
            pl.BlockSpec((None, NA_HEADS, NA_ROWS * GRID_W, NA_KROWS * GRID_W), lambda b, g: (table_of(g), 0, 0, 0)),
        ],
        out_specs=pl.BlockSpec((1, NA_ROWS * GRID_W, NA_W), lambda b, g: (b, g, 0)),
        out_shape=jax.ShapeDtypeStruct((B, Tt, NA_W), BF16),
        compiler_params=_cparams(("parallel", "arbitrary")),
        name="na",
    )(naqkv, naqkv, naqkv, bias_tbl)


def _na_group_offsets(rows):
    offs = [g * NA_ROWS - min(max(g * NA_ROWS - WIN_R // 2, 0), rows - NA_KROWS) for g in range(rows // NA_ROWS)]
    return sorted(set(offs)), offs


def _na_bias_tables(rpb, rows):
    L = rpb.shape[0]
    uniq, offs = _na_group_offsets(rows)
    qc = jnp.arange(GRID_W)
    c_start = jnp.clip(qc - WIN_C // 2, 0, GRID_W - WIN_C)
    kc = jnp.arange(GRID_W)
    dc = kc[None, :] - qc[:, None] + (WIN_C - 1)
    col_ok = (kc[None, :] >= c_start[:, None]) & (kc[None, :] < c_start[:, None] + WIN_C)
    onehot = (dc[None] == jnp.arange(2 * WIN_C - 1)[:, None, None]).astype(F32).reshape(2 * WIN_C - 1, -1)
    toep = jnp.dot(rpb.astype(F32).reshape(-1, 2 * WIN_C - 1), onehot, precision=HI)
    toep = toep.reshape(L, NA_HEADS, 2 * WIN_R - 1, GRID_W, GRID_W)
    tables = []
    for off in uniq:
        g = offs.index(off)
        ws = g * NA_ROWS - off
        rs = [min(max(g * NA_ROWS + i - WIN_R // 2, 0), rows - WIN_R) for i in range(NA_ROWS)]
        blocks = []
        for i in range(NA_ROWS):
            r = g * NA_ROWS + i
            per_key_row = []
            for a in range(NA_KROWS):
                inside = rs[i] <= ws + a < rs[i] + WIN_R
                dr = ws + a - r + (WIN_R - 1)
                per_key_row.append(jnp.where(col_ok, toep[:, :, dr], NEG_BIG) if inside
                                   else jnp.full((L, NA_HEADS, GRID_W, GRID_W), NEG_BIG, F32))
            blocks.append(jnp.concatenate(per_key_row, axis=-1))
        tables.append(jnp.concatenate(blocks, axis=-2))
    return jnp.stack(tables, axis=1)


def _merge_kernel(of_ref, ob_ref, z_ref, fb_ref, oc_ref, gr_ref, x_ref, mod_ref, gnw_ref, n2w_ref,
                  wa_ref, wb_ref, wc_ref, wo_ref, wr_ref, x1_ref, h2_ref, aff_ref):
    o = of_ref[0] + ob_ref[0]
    z = z_ref[0].astype(F32)
    ya = jnp.zeros((TOK_BLK, D_MODEL), F32)
    for h in range(GDN_HEADS):
        hs = slice(h * GDN_DV, (h + 1) * GDN_DV)
        oh = o[:, hs]
        on = oh * lax.rsqrt(jnp.mean(oh * oh, axis=-1, keepdims=True) + RMS_EPS) * gnw_ref[...]
        ya = ya + _dot((on * _silu(z[:, hs])).astype(BF16), wa_ref[hs, :])
    yb = _dot(fb_ref[0], wb_ref[...])
    yc = _dot(oc_ref[0], wc_ref[...])
    gate = lambda j: jax.nn.sigmoid(gr_ref[0, :, j * D_MODEL:(j + 1) * D_MODEL].astype(F32))
    y = gate(0) * ya + gate(1) * yb + gate(2) * yc
    x1 = x_ref[0] + mod_ref[2:3, :] * _dot(y.astype(BF16), wo_ref[...])
    x1_ref[0] = x1
    h2 = _rms_mod(x1, n2w_ref[...], mod_ref[3:4, :], mod_ref[4:5, :]).astype(BF16)
    h2_ref[0] = h2
    logits = _dg(wr_ref[...], h2, _NT)
    ex = jnp.exp(logits - jnp.max(logits, axis=0, keepdims=True))
    aff_ref[0] = ex / jnp.sum(ex, axis=0, keepdims=True)


def _merge(o_f, o_b, z, fb, oc, gr, xs, modv, gnw, n2w, wa, wb, wc, wo, wr, n_lat_blk):
    B, Tt, D = xs.shape
    tok = lambda w: pl.BlockSpec((1, TOK_BLK, w), lambda b, t: (b, t, 0))
    const = lambda shape: pl.BlockSpec(shape, lambda b, t: (0, 0))
    return pl.pallas_call(
        _merge_kernel,
        grid=(B, Tt // TOK_BLK),
        in_specs=[
            tok(GDN_V_W), tok(GDN_V_W), tok(GDN_V_W), tok(FNET_W), tok(NA_W), tok(GATE_W), tok(D),
            pl.BlockSpec((None, None, 6, D), lambda b, t: (b, t // n_lat_blk, 0, 0)),
            const((1, GDN_DV)), const((1, D)),
            const((GDN_V_W, D)), const((FNET_W, D)), const((NA_W, D)), const((D, D)),
            const((N_EXPERTS, D)),
        ],
        out_specs=[tok(D), tok(D), pl.BlockSpec((1, N_EXPERTS, TOK_BLK), lambda b, t: (b, 0, t))],
        out_shape=[jax.ShapeDtypeStruct((B, Tt, D), F32), jax.ShapeDtypeStruct((B, Tt, D), BF16),
                   jax.ShapeDtypeStruct((B, N_EXPERTS, Tt), F32)],
        compiler_params=_cparams(("parallel", "parallel"), VMEM_LIMIT_BIG),
        name="merge",
    )(o_f, o_b, z, fb, oc, gr, xs, modv, gnw, n2w, wa, wb, wc, wo, wr)


def _select_kernel(a_ref, pos_ref, wa_ref, lo_ref, cnt_ref, *, T, Lc):
    ii = lax.broadcasted_iota(jnp.int32, (LANES, LANES), 0)
    jj = lax.broadcasted_iota(jnp.int32, (LANES, LANES), 1)
    tri = (ii < jj).astype(BF16)
    cap_lat = EC_CAPACITY_FACTOR * T // N_EXPERTS
    cap_ctx = EC_CAPACITY_FACTOR * Lc // N_EXPERTS
    blk_lane = lax.broadcasted_iota(jnp.int32, (N_EXPERTS, LANES), 1)
    lo_all = jnp.zeros((N_EXPERTS, LANES), F32)
    cnt_all = jnp.zeros((N_EXPERTS, LANES), F32)
    for s0, n, cap, poff in ((0, T, cap_lat, 0), (T, Lc, cap_ctx, cap_lat)):
        a = a_ref[0, :, s0:s0 + n]
        bits = lax.bitcast_convert_type(a, jnp.int32)

        def count(mask):
            return jnp.sum(jnp.where(mask, 1.0, 0.0), axis=1, keepdims=True)

        def radix(i, pref):
            cand = pref | jnp.left_shift(jnp.int32(1), 30 - i)
            return jnp.where(count(bits >= cand) >= cap, cand, pref)

        thr = lax.fori_loop(0, 31, radix, jnp.zeros((N_EXPERTS, 1), jnp.int32))
        gt = bits > thr
        eq = bits == thr
        need = cap - count(gt)
        idx = lax.broadcasted_iota(jnp.int32, (N_EXPERTS, n), 1)
        nbits = max(1, (n - 1).bit_length())

        def tie(i, ans):
            cand = ans | jnp.left_shift(jnp.int32(1), nbits - 1 - i)
            return jnp.where(count(eq & (idx < cand)) < need, cand, ans)

        last = lax.fori_loop(0, nbits, tie, jnp.zeros((N_EXPERTS, 1), jnp.int32))
        sel = gt | (eq & (idx <= last))
        selb = jnp.where(sel, 1.0, 0.0).astype(BF16)
        ti = lax.broadcasted_iota(jnp.int32, (n, LANES), 0)
        tj = lax.broadcasted_iota(jnp.int32, (n, LANES), 1)
        seg_tot = _dot(selb, (ti // LANES == tj).astype(BF16))
        seg_off = _dot(seg_tot.astype(BF16), tri)
        for s in range(n // LANES):
            ls = slice(s * LANES, (s + 1) * LANES)
            within = _dot(selb[:, ls], tri)
            p = (within + seg_off[:, s:s + 1]).astype(jnp.int32) + poff
            pos_ref[0, :, s0 + s * LANES:s0 + (s + 1) * LANES] = jnp.where(sel[:, ls], p, -1)
        wa_ref[0, :, s0:s0 + n] = jnp.where(sel, a, 0.0)
        blk_tot = _dot(selb, ((ti + s0) // TOK_BLK == tj).astype(BF16))
        blk_off = _dot(blk_tot.astype(BF16), tri) + poff
        mine = (blk_lane >= s0 // TOK_BLK) & (blk_lane < (s0 + n) // TOK_BLK)
        lo_all = jnp.where(mine, blk_off, lo_all)
        cnt_all = jnp.where(mine, blk_tot, cnt_all)
    lo_ref[0] = lo_all.astype(jnp.int32)
    cnt_ref[0] = cnt_all.astype(jnp.int32)


def _select(aff, T):
    B, E, Tt = aff.shape
    blk = pl.BlockSpec((1, E, Tt), lambda b: (b, 0, 0))
    meta = pl.BlockSpec((1, E, LANES), lambda b: (b, 0, 0))
    return pl.pallas_call(
        functools.partial(_select_kernel, T=T, Lc=Tt - T),
        grid=(B,),
        in_specs=[blk],
        out_specs=[blk, blk, meta, meta],
        out_shape=[jax.ShapeDtypeStruct((B, E, Tt), jnp.int32), jax.ShapeDtypeStruct((B, E, Tt), F32),
                   jax.ShapeDtypeStruct((B, E, LANES), jnp.int32), jax.ShapeDtypeStruct((B, E, LANES), jnp.int32)],
        compiler_params=_cparams(("parallel",)),
        name="ec_select",
    )(aff)


MOE_WIN = 64


def _moe_windows(lo_ref, cnt_ref, nch):
    b, c = pl.program_id(0), pl.program_id(1)
    w0s = []
    nmax = jnp.int32(0)
    for e in range(N_EXPERTS):
        i = (b * N_EXPERTS + e) * nch + c
        lo, cnt = lo_ref[i], cnt_ref[i]
        w0 = (lo // BF16_SUBLANES) * BF16_SUBLANES
        w0s.append(w0)
        nmax = jnp.maximum(nmax, jnp.where(cnt > 0, (lo - w0 + cnt + MOE_WIN - 1) // MOE_WIN, 0))
    return w0s, nmax


def _window_starts(w0s, i, R):
    return [pl.multiple_of(jnp.minimum(w0 + i * MOE_WIN, R), BF16_SUBLANES) for w0 in w0s]


def _moe_gather_kernel(lo_ref, cnt_ref, h_ref, pos_ref, xe_ref, *, R, nch):
    @pl.when(pl.program_id(1) == 0)
    def _():
        xe_ref[...] = jnp.zeros_like(xe_ref)

    w0s, nmax = _moe_windows(lo_ref, cnt_ref, nch)
    slot0 = lax.broadcasted_iota(jnp.int32, (MOE_WIN, TOK_BLK), 0)

    def win(i, carry):
        r0s = _window_starts(w0s, i, R)
        onehot = jnp.concatenate(
            [jnp.where(pos_ref[e] == slot0 + r0s[e], 1.0, 0.0).astype(BF16) for e in range(N_EXPERTS)], axis=0)
        rows = _dot(onehot, h_ref[0]).astype(BF16)
        for e in range(N_EXPERTS):
            xe_ref[e, pl.ds(r0s[e], MOE_WIN), :] += rows[e * MOE_WIN:(e + 1) * MOE_WIN]
        return carry

    lax.fori_loop(0, nmax, win, 0)


def _moe_ffn_kernel(xe_ref, wgf_ref, wuf_ref, wdf_ref, ye_ref, wg_ref, wu_ref, wd_ref, *, R):
    @pl.when(pl.program_id(1) == 0)
    def _():
        wg_ref[...] = wgf_ref[...].astype(BF16)
        wu_ref[...] = wuf_ref[...].astype(BF16)
        wd_ref[...] = wdf_ref[...].astype(BF16)

    xe = xe_ref[0:R, :]
    fstep = 512
    ye = jnp.zeros((R, D_MODEL), F32)
    for f in range(D_EXPERT // fstep):
        fs = slice(f * fstep, (f + 1) * fstep)
        hid = _silu(_dot(xe, wg_ref[:, fs])) * _dot(xe, wu_ref[:, fs])
        ye = ye + _dot(hid.astype(BF16), wd_ref[fs, :])
    ye_ref[0:R, :] = ye.astype(BF16)
    ye_ref[R:R + MOE_WIN, :] = jnp.zeros((MOE_WIN, D_MODEL), BF16)


def _moe_scatter_kernel(lo_ref, cnt_ref, ye_ref, pos_ref, wa_ref, x_ref, mod_ref, o_ref, *, R, nch):
    w0s, nmax = _moe_windows(lo_ref, cnt_ref, nch)
    slot0 = lax.broadcasted_iota(jnp.int32, (MOE_WIN, TOK_BLK), 0)

    def win(i, acc):
        r0s = _window_starts(w0s, i, R)
        weighted = jnp.concatenate(
            [jnp.where(pos_ref[e] == slot0 + r0s[e], wa_ref[e], 0.0).astype(BF16) for e in range(N_EXPERTS)], axis=0)
        ye = jnp.concatenate([ye_ref[e, pl.ds(r0s[e], MOE_WIN), :] for e in range(N_EXPERTS)], axis=0)
        return acc + _dg(weighted, ye, _TN)

    acc = lax.fori_loop(0, nmax, win, jnp.zeros((TOK_BLK, D_MODEL), F32))
    o_ref[0] = x_ref[0] + mod_ref[5:6, :] * acc


def _moe(h2, pos, waff, lo, cnt, wg, wu, wd, layer, x1, modv, n_lat_blk, n_out_blk):
    B, Tt, D = h2.shape
    E = N_EXPERTS
    nch = Tt // TOK_BLK
    R = EC_CAPACITY_FACTOR * Tt // N_EXPERTS
    RP = R + MOE_WIN
    assert R % BF16_SUBLANES == 0
    pos = pos.reshape(B, E, nch, 1, TOK_BLK)
    waff = waff.reshape(B, E, nch, 1, TOK_BLK)
    lo = lo[:, :, :nch].reshape(-1)
    cnt = cnt[:, :, :nch].reshape(-1)
    rowspec = pl.BlockSpec((None, E, None, 1, TOK_BLK), lambda b, c, *_: (b, 0, c, 0, 0))
    tok = pl.BlockSpec((1, TOK_BLK, D), lambda b, c, *_: (b, c, 0))
    slots = pl.BlockSpec((None, E, RP, D), lambda b, c, *_: (b, 0, 0, 0))
    xe = pl.pallas_call(
        functools.partial(_moe_gather_kernel, R=R, nch=nch),
        grid_spec=pltpu.PrefetchScalarGridSpec(
            num_scalar_prefetch=2, grid=(B, nch), in_specs=[tok, rowspec], out_specs=slots),
        out_shape=jax.ShapeDtypeStruct((B, E, RP, D), BF16),
        compiler_params=_cparams(("parallel", "arbitrary"), VMEM_LIMIT_BIG),
        name="ec_gather",
    )(lo, cnt, h2, pos)
    wspec = lambda r, c: pl.BlockSpec((None, None, r, c), lambda e, b: (layer, e, 0, 0))
    slot1 = pl.BlockSpec((None, None, RP, D), lambda e, b: (b, e, 0, 0))
    ye = pl.pallas_call(
        functools.partial(_moe_ffn_kernel, R=R),
        grid=(E, B),
        in_specs=[slot1, wspec(D, D_EXPERT), wspec(D, D_EXPERT), wspec(D_EXPERT, D)],
        out_specs=slot1,
        out_shape=jax.ShapeDtypeStruct((B, E, RP, D), BF16),
        scratch_shapes=[pltpu.VMEM((D, D_EXPERT), BF16), pltpu.VMEM((D, D_EXPERT), BF16),
                        pltpu.VMEM((D_EXPERT, D), BF16)],
        compiler_params=_cparams(("parallel", "arbitrary"), VMEM_LIMIT_BIG),
        name="ec_ffn",
    )(xe, wg, wu, wd)
    return pl.pallas_call(
        functools.partial(_moe_scatter_kernel, R=R, nch=nch),
        grid_spec=pltpu.PrefetchScalarGridSpec(
            num_scalar_prefetch=2, grid=(B, n_out_blk),
            in_specs=[slots, rowspec, rowspec, tok,
                      pl.BlockSpec((None, None, 6, D), lambda b, c, *_: (b, c // n_lat_blk, 0, 0))],
            out_specs=tok),
        out_shape=jax.ShapeDtypeStruct((B, n_out_blk * TOK_BLK, D), F32),
        compiler_params=_cparams(("parallel", "arbitrary"), VMEM_LIMIT_BIG),
        name="ec_scatter",
    )(lo, cnt, ye, pos, waff, x1, modv)


def _rope_tables(T, Lc):
    nf = GDN_DK // 4
    inv = ROPE_BASE ** (-jnp.arange(nf, dtype=F32) / nf)
    t = jnp.arange(T)
    ang_r = (t // GRID_W).astype(F32)[:, None] * inv
    ang_c = (t % GRID_W).astype(F32)[:, None] * inv
    cos = jnp.concatenate([jnp.cos(ang_r)] * 2 + [jnp.cos(ang_c)] * 2, axis=-1)
    sin = jnp.concatenate([-jnp.sin(ang_r), jnp.sin(ang_r), -jnp.sin(ang_c), jnp.sin(ang_c)], axis=-1)
    cos = jnp.concatenate([cos, jnp.ones((Lc, GDN_DK), F32)], axis=0)
    sin = jnp.concatenate([sin, jnp.zeros((Lc, GDN_DK), F32)], axis=0)
    return cos, sin


def _reorder_w_in(w_in):
    qa, ka, va, za, aa, ba, ub, qn, kn, vn, gr = jnp.split(
        w_in, [512, 1024, 1536, 2048, 2056, 2064, 2320, 2576, 2832, 3088], axis=-1)
    pad = jnp.zeros(w_in.shape[:-1] + (AB_PAD - 2 * N_DIR * GDN_HEADS,), w_in.dtype)
    return jnp.concatenate([qa, ka, va, za, ub, qn, kn, vn, gr, aa, ba, pad], axis=-1).astype(BF16)


def kernel(x, c, ctx, c_ctx, w_mod, b_mod, norm1_w, norm2_w, w_in, conv_w, a_log, dt_bias, gdn_norm_w,
           na_qn_w, na_kn_w, na_rpb, w_br_a, w_br_b, w_br_c, w_out, w_router, w_e_gate, w_e_up, w_e_down):
    B, T, D = x.shape
    Lc = ctx.shape[1]
    Tt = T + Lc
    assert D == D_MODEL and T % TOK_BLK == 0 and Lc == TOK_BLK and T % Lc == 0
    assert T // GRID_W >= NA_KROWS and (T // GRID_W) % NA_ROWS == 0
    n_lat_blk = T // TOK_BLK

    w_in_r = _reorder_w_in(w_in)
    bf = lambda a: a.astype(BF16)
    w_mod16, wa16, wb16, wc16, wo16 = bf(w_mod), bf(w_br_a), bf(w_br_b), bf(w_br_c), bf(w_out)
    wr16 = bf(jnp.swapaxes(w_router, 1, 2))
    pad8 = AB_PAD - N_DIR * GDN_HEADS
    al = jnp.pad(a_log.reshape(DEPTH, 1, -1), ((0, 0), (0, 0), (0, pad8)))
    dtb = jnp.pad(dt_bias.reshape(DEPTH, 1, -1), ((0, 0), (0, 0), (0, pad8)))
    qnw = jnp.tile(na_qn_w, (1, NA_HEADS))[:, None, :]
    knw = jnp.tile(na_kn_w, (1, NA_HEADS))[:, None, :]
    head_of = jnp.arange(NA_W) // NA_DH
    bd = ((head_of[:, None] == head_of[None, :]).astype(F32) / NA_DH).astype(BF16)
    cos_t, sin_t = _rope_tables(T, Lc)
    mats_half = tuple(bf(m) for m in _dft_mats(T // 2, T ** -0.5))
    kk = jnp.arange(T // 2, dtype=F32)[:, None] * (2.0 * math.pi / T)
    twiddles = (jnp.broadcast_to(jnp.cos(kk), (T // 2, FNET_W)), jnp.broadcast_to(jnp.sin(kk), (T // 2, FNET_W)))
    mats_ctx = tuple(bf(m) for m in _dft_mats(Lc, Lc ** -0.5))
    cch, sch = _dft_mats(FNET_GROUP_CH, FNET_GROUP_CH ** -0.5)
    eye_g = jnp.eye(FNET_GROUPS, dtype=F32)
    mats_ch = (bf(jnp.kron(eye_g, cch)), bf(jnp.kron(eye_g, sch)))

    rows = -(-(B + 1) // 8) * 8
    cs = jnp.zeros((rows, D), F32).at[:B].set(c).at[B].set(c_ctx)
    mod = _modulation(cs, w_mod16, b_mod[:, None, :])
    mod_lat = mod[:, :B].reshape(DEPTH, B, 1, 6, D)
    mod_ctx = jnp.broadcast_to(mod[:, B].reshape(DEPTH, 1, 1, 6, D), (DEPTH, B, 1, 6, D))
    modv = jnp.concatenate([mod_lat, mod_ctx], axis=2)

    bias_tbls = _na_bias_tables(na_rpb, T // GRID_W)
    xs = jnp.concatenate([x, ctx], axis=1)
    for i in range(DEPTH):
        last = i == DEPTH - 1
        qkv, z, ub, naqkv, gr, gb = _input_proj(xs, modv[i], norm1_w[i][None], w_in_r[i], al[i], dtb[i],
                                                qnw[i], knw[i], bd, n_lat_blk)
        qkvp = _gdn_prep(qkv, conv_w[i], cos_t, sin_t, T)
        o_f, o_b = _gdn_scan(*_gdn_chunk(qkvp, gb), T)
        fb = _fnet(ub, mats_half, twiddles, mats_ctx, mats_ch, T)
        oc = _na(naqkv, bias_tbls[i], T)
        x1, h2, aff = _merge(o_f, o_b, z, fb, oc, gr, xs, modv[i], gdn_norm_w[i][None], norm2_w[i][None],
                             wa16[i], wb16[i], wc16[i], wo16[i], wr16[i], n_lat_blk)
        pos, waff, lo, cnt = _select(aff, T)
        xs = _moe(h2, pos, waff, lo, cnt, w_e_gate, w_e_up, w_e_down, i, x1, modv[i], n_lat_blk,
                  n_lat_blk if last else Tt // TOK_BLK)
    return xs
```

```python
import functools
import math

import jax
import jax.numpy as jnp
from jax import lax
from jax.experimental import pallas as pl
from jax.experimental.pallas import tpu as pltpu

D_MODEL = 1024
DEPTH = 4
GRID_W = 64
RMS_EPS = 1e-6
ROPE_BASE = 10000.0

GDN_HEADS = 4
GDN_DK = 128
GDN_DV = 128
GDN_QK_W = GDN_HEADS * GDN_DK
GDN_V_W = GDN_HEADS * GDN_DV
CONV_W = 5
CHUNK = 64
N_DIR = 2

FNET_GROUPS = 4
FNET_GROUP_CH = 64
FNET_W = FNET_GROUPS * FNET_GROUP_CH

NA_HEADS = 4
NA_DH = 64
NA_W = NA_HEADS * NA_DH
WIN_R = 8
WIN_C = 16

N_BRANCH = 3
N_EXPERTS = 16
EC_CAPACITY_FACTOR = 2
D_EXPERT = 1024

LANES = 128
BF16_SUBLANES = 16
VMEM_LIMIT_BIG = 56 * 1024 * 1024

TOK_BLK = 256
QKV_W = 2 * GDN_QK_W + GDN_V_W
NAQKV_W = 3 * NA_W
GATE_W = N_BRANCH * D_MODEL
AB_PAD = LANES
N_IN_PAD = QKV_W + GDN_V_W + FNET_W + NAQKV_W + GATE_W + AB_PAD

F32 = jnp.float32
BF16 = jnp.bfloat16
HI = lax.Precision.HIGHEST
NEG_BIG = -1e30

_NT = (((1,), (1,)), ((), ()))
_TN = (((0,), (0,)), ((), ()))


def _dot(a, b, prec=None):
    return jnp.dot(a, b, preferred_element_type=F32, precision=prec)


def _dg(a, b, dims, prec=None):
    return lax.dot_general(a, b, dims, preferred_element_type=F32, precision=prec)


def _silu(v):
    return v * jax.nn.sigmoid(v)


def _cparams(sem, vmem=None):
    return pltpu.CompilerParams(dimension_semantics=sem, vmem_limit_bytes=vmem)


def _mod_kernel(c_ref, w_ref, b_ref, o_ref):
    s = _silu(c_ref[...]).astype(BF16)
    o_ref[0] = _dot(s, w_ref[0]) + b_ref[0]


def _modulation(cs, w_mod, b_mod):
    L, D, N = w_mod.shape
    R = cs.shape[0]
    tn = 1536
    return pl.pallas_call(
        _mod_kernel,
        grid=(L, N // tn),
        in_specs=[
            pl.BlockSpec((R, D), lambda l, n: (0, 0)),
            pl.BlockSpec((1, D, tn), lambda l, n: (l, 0, n)),
            pl.BlockSpec((1, 1, tn), lambda l, n: (l, 0, n)),
        ],
        out_specs=pl.BlockSpec((1, R, tn), lambda l, n: (l, 0, n)),
        out_shape=jax.ShapeDtypeStruct((L, R, N), F32),
        compiler_params=_cparams(("parallel", "parallel")),
        name="modulation",
    )(cs, w_mod, b_mod)


def _rms_mod(x, nw, shift, scale):
    ms = jnp.mean(x * x, axis=-1, keepdims=True)
    return (x * lax.rsqrt(ms + RMS_EPS) * nw) * (1.0 + scale) + shift


def _proj_kernel(x_ref, mod_ref, nw_ref, w_ref, al_ref, dt_ref, qw_ref, kw_ref, bd_ref,
                 qkv_ref, z_ref, ub_ref, naqkv_ref, gr_ref, gb_ref):
    h = _rms_mod(x_ref[0], nw_ref[...], mod_ref[0:1, :], mod_ref[1:2, :]).astype(BF16)

    def mm(lo, width):
        return _dot(h, w_ref[:, lo:lo + width])

    col = 0
    for ref, width in ((qkv_ref, QKV_W), (z_ref, GDN_V_W), (ub_ref, FNET_W)):
        step = 512 if width % 512 == 0 else 256
        for j in range(width // step):
            ref[0, :, j * step:(j + 1) * step] = mm(col + j * step, step).astype(ref.dtype)
        col += width

    def head_rms(a, w):
        sq = a * a
        hi = sq.astype(BF16)
        lo = (sq - hi.astype(F32)).astype(BF16)
        ms = _dot(hi, bd_ref[...]) + _dot(lo, bd_ref[...])
        return a * lax.rsqrt(ms + RMS_EPS) * w

    naqkv_ref[0, :, 0:NA_W] = (head_rms(mm(col, NA_W), qw_ref[...]) * (NA_DH ** -0.5)).astype(BF16)
    naqkv_ref[0, :, NA_W:2 * NA_W] = head_rms(mm(col + NA_W, NA_W), kw_ref[...]).astype(BF16)
    naqkv_ref[0, :, 2 * NA_W:3 * NA_W] = mm(col + 2 * NA_W, NA_W).astype(BF16)
    col += NAQKV_W
    for j in range(GATE_W // 512):
        gr_ref[0, :, j * 512:(j + 1) * 512] = mm(col + j * 512, 512).astype(BF16)
    col += GATE_W
    ab = mm(col, AB_PAD)
    sp_in = ab + dt_ref[...]
    softplus = jnp.maximum(sp_in, 0.0) + jnp.log(1.0 + jnp.exp(-jnp.abs(sp_in)))
    g = -jnp.exp(al_ref[...]) * softplus
    lane = lax.broadcasted_iota(jnp.int32, ab.shape, 1)
    gb_ref[0] = jnp.where(lane < N_DIR * GDN_HEADS, g, jax.nn.sigmoid(ab))


def _input_proj(xs, modv, nw, w_in, al, dtb, qw, kw, bd, n_lat_blk):
    B, Tt, D = xs.shape
    nblk = Tt // TOK_BLK
    tok = lambda w: pl.BlockSpec((1, TOK_BLK, w), lambda b, t: (b, t, 0))
    widths = (QKV_W, GDN_V_W, FNET_W, NAQKV_W, GATE_W)
    return pl.pallas_call(
        _proj_kernel,
        grid=(B, nblk),
        in_specs=[
            tok(D),
            pl.BlockSpec((None, None, 6, D), lambda b, t: (b, t // n_lat_blk, 0, 0)),
            pl.BlockSpec((1, D), lambda b, t: (0, 0)),
            pl.BlockSpec((D, N_IN_PAD), lambda b, t: (0, 0)),
            pl.BlockSpec((1, AB_PAD), lambda b, t: (0, 0)),
            pl.BlockSpec((1, AB_PAD), lambda b, t: (0, 0)),
            pl.BlockSpec((1, NA_W), lambda b, t: (0, 0)),
            pl.BlockSpec((1, NA_W), lambda b, t: (0, 0)),
            pl.BlockSpec((NA_W, NA_W), lambda b, t: (0, 0)),
        ],
        out_specs=[tok(w) for w in widths] + [tok(AB_PAD)],
        out_shape=[jax.ShapeDtypeStruct((B, Tt, w), BF16) for w in widths]
        + [jax.ShapeDtypeStruct((B, Tt, AB_PAD), F32)],
        compiler_params=_cparams(("parallel", "parallel"), VMEM_LIMIT_BIG),
        name="input_proj",
    )(xs, modv, nw, w_in, al, dtb, qw, kw, bd)


def _gdn_prep_kernel(u_ref, cw_ref, cos_ref, sin_ref, o_ref, *, T, Tt):
    j = pl.program_id(1)
    w = cw_ref[...]
    scale = jnp.where(j < GDN_HEADS, GDN_DK ** -0.5, 1.0).astype(F32)
    halo = BF16_SUBLANES
    lane = lax.broadcasted_iota(jnp.int32, (TOK_BLK, LANES), 1)
    exti = lax.broadcasted_iota(jnp.int32, (TOK_BLK + 2 * halo, LANES), 0)

    def conv_silu(start):
        ps = pl.multiple_of(jnp.maximum(start - halo, 0), halo)
        ns = pl.multiple_of(jnp.minimum(start + TOK_BLK, Tt - halo), halo)
        ext = jnp.concatenate([
            u_ref[0, pl.ds(ps, halo), :].astype(F32),
            u_ref[0, pl.ds(start, TOK_BLK), :].astype(F32),
            u_ref[0, pl.ds(ns, halo), :].astype(F32)], axis=0)
        erow = start - halo + exti
        seg_lo = jnp.where(start < T, 0, T)
        seg_hi = jnp.where(start < T, T, Tt)
        ext = jnp.where((erow >= seg_lo) & (erow < seg_hi), ext, 0.0)
        acc = jnp.zeros((TOK_BLK, LANES), F32)
        for d in range(-(CONV_W // 2), CONV_W // 2 + 1):
            acc = acc + ext[halo + d:halo + d + TOK_BLK, :] * w[d + CONV_W // 2:d + CONV_W // 2 + 1, :]
        return _silu(acc)

    def qk_body(c, carry):
        start = pl.multiple_of(c * TOK_BLK, TOK_BLK)
        y = conv_silu(start)
        yn = y * lax.rsqrt(jnp.sum(y * y, axis=-1, keepdims=True) + RMS_EPS)
        partner = jnp.where((lane & 32) == 0, pltpu.roll(yn, LANES - 32, 1), pltpu.roll(yn, 32, 1))
        rows = pl.ds(start, TOK_BLK)
        o_ref[0, rows, :] = (yn * cos_ref[rows, :] + partner * sin_ref[rows, :]) * scale
        return carry

    def v_body(c, carry):
        start = pl.multiple_of(c * TOK_BLK, TOK_BLK)
        o_ref[0, pl.ds(start, TOK_BLK), :] = conv_silu(start)
        return carry

    @pl.when(j < 2 * GDN_HEADS)
    def _():
        lax.fori_loop(0, Tt // TOK_BLK, qk_body, 0, unroll=True)

    @pl.when(j >= 2 * GDN_HEADS)
    def _():
        lax.fori_loop(0, Tt // TOK_BLK, v_body, 0, unroll=True)


def _gdn_prep(qkv, conv_w, cos_t, sin_t, T):
    B, Tt, W = qkv.shape
    return pl.pallas_call(
        functools.partial(_gdn_prep_kernel, T=T, Tt=Tt),
        grid=(B, W // LANES),
        in_specs=[
            pl.BlockSpec((1, Tt, LANES), lambda b, j: (b, 0, j)),
            pl.BlockSpec((CONV_W, LANES), lambda b, j: (0, j)),
            pl.BlockSpec((Tt, LANES), lambda b, j: (0, 0)),
            pl.BlockSpec((Tt, LANES), lambda b, j: (0, 0)),
        ],
        out_specs=pl.BlockSpec((1, Tt, LANES), lambda b, j: (b, 0, j)),
        out_shape=jax.ShapeDtypeStruct((B, Tt, W), F32),
        compiler_params=_cparams(("parallel", "parallel")),
        name="gdn_prep",
    )(qkv, conv_w, cos_t, sin_t)


N_CHAIN = N_DIR * GDN_HEADS


def _unit_tri_inverses(ms, eye, blk16, blk32):
    b = lambda a: a.astype(BF16)
    n1 = [-jnp.where(blk16, m, 0.0) for m in ms]
    n1b = [b(n) for n in n1]
    n2b = [b(_dot(n, n)) for n in n1b]
    n4b = [b(_dot(n, n)) for n in n2b]
    n8b = [b(_dot(n, n)) for n in n4b]
    ps = [eye + n for n in n1]
    for powers in (n2b, n4b, n8b):
        ps = [p + _dot(b(p), n) for p, n in zip(ps, powers)]
    for pick in (lambda m: jnp.where(blk32 & jnp.logical_not(blk16), m, 0.0), lambda m: jnp.where(blk32, 0.0, m)):
        pbs = [b(p) for p in ps]
        ts = [b(_dot(pb, b(pick(m)))) for pb, m in zip(pbs, ms)]
        ps = [p - _dot(t, pb) for p, t, pb in zip(ps, ts, pbs)]
    return ps


def _gdn_chunk_kernel(q_ref, k_ref, v_ref, g_ref, wq_ref, kg_ref, u_ref, qkm_ref, gl_ref):
    HB = GDN_HEADS * CHUNK
    ii = lax.broadcasted_iota(jnp.int32, (HB, HB), 0)
    jj = lax.broadcasted_iota(jnp.int32, (HB, HB), 1)
    eye = (ii == jj).astype(F32)
    blk16 = (ii >> 4) == (jj >> 4)
    blk32 = (ii >> 5) == (jj >> 5)
    blk64 = (ii >> 6) == (jj >> 6)
    masks = ((blk64 & (ii >= jj), blk64 & (ii > jj)), (blk64 & (ii <= jj), blk64 & (ii < jj)))
    row = lax.broadcasted_iota(jnp.int32, (CHUNK, LANES), 0)

    def cumsum_rows(x, backward):
        sh = 1
        while sh < CHUNK:
            if backward:
                x = x + jnp.where(row < CHUNK - sh, pltpu.roll(x, CHUNK - sh, 0), 0.0)
            else:
                x = x + jnp.where(row >= sh, pltpu.roll(x, sh, 0), 0.0)
            sh *= 2
        return x

    ms, rhss, dests = [], [], []
    for j in range(TOK_BLK // CHUNK):
        rows = slice(j * CHUNK, (j + 1) * CHUNK)
        gbv = g_ref[0, rows, :]
        tot_row = jnp.sum(gbv, axis=0, keepdims=True)
        gl_ref[0, rows, :] = jnp.broadcast_to(jnp.exp(tot_row), (CHUNK, LANES))
        stack = lambda ref: jnp.concatenate(
            [ref[0, rows, h * GDN_DK:(h + 1) * GDN_DK] for h in range(GDN_HEADS)], axis=0)
        qst, kst, vst = stack(q_ref), stack(k_ref), stack(v_ref)
        k16 = kst.astype(BF16)
        kk = _dg(k16, k16, _NT)
        qk = _dg(qst.astype(BF16), k16, _NT)
        for d, (le, lt) in enumerate(masks):
            gc_d = cumsum_rows(gbv, d == 1)
            col = lambda a, off: jnp.concatenate(
                [a[:, off + d * GDN_HEADS + h:off + d * GDN_HEADS + h + 1] for h in range(GDN_HEADS)], axis=0)
            gc = col(gc_d, 0)
            beta = col(gbv, N_CHAIN)
            tot = col(jnp.broadcast_to(tot_row, (CHUNK, LANES)), 0)
            gcm = jnp.broadcast_to(gc, (HB, HB))
            dec = jnp.where(le, jnp.exp(jnp.where(le, gcm - gcm.T, 0.0)), 0.0)
            ms.append(jnp.where(lt, beta * kk * dec, 0.0))
            eg = jnp.exp(gc)
            rhss.append(jnp.concatenate([vst * beta, kst * (beta * eg)], axis=1).astype(BF16))
            dests.append((j, d))
            qkm = jnp.where(le, qk * dec, 0.0).astype(BF16)
            qg = (qst * eg).astype(BF16)
            kg = (kst * jnp.exp(tot - gc)).astype(BF16)
            for h in range(GDN_HEADS):
                l = d * GDN_HEADS + h
                hr = slice(h * CHUNK, (h + 1) * CHUNK)
                ls = slice(l * GDN_DK, (l + 1) * GDN_DK)
                wq_ref[0, (2 * j + 1) * CHUNK:(2 * j + 2) * CHUNK, ls] = qg[hr]
                kg_ref[0, rows, ls] = kg[hr]
                qkm_ref[0, rows, l * CHUNK:(l + 1) * CHUNK] = qkm[hr, h * CHUNK:(h + 1) * CHUNK]
    tinvs = _unit_tri_inverses(ms, eye, blk16, blk32)
    sols = [_dot(t.astype(BF16), r) for t, r in zip(tinvs, rhss)]
    for (j, d), sol in zip(dests, sols):
        for h in range(GDN_HEADS):
            hr = slice(h * CHUNK, (h + 1) * CHUNK)
            ls = slice((d * GDN_HEADS + h) * GDN_DK, (d * GDN_HEADS + h + 1) * GDN_DK)
            u_ref[0, j * CHUNK:(j + 1) * CHUNK, ls] = sol[hr, :GDN_DV]
            wq_ref[0, 2 * j * CHUNK:(2 * j + 1) * CHUNK, ls] = sol[hr, GDN_DV:].astype(BF16)


def _gdn_chunk(qkvp, gb):
    B, Tt, _ = qkvp.shape
    cw = N_CHAIN * GDN_DK
    tok = lambda col, w, r=1: pl.BlockSpec((1, r * TOK_BLK, w), lambda b, t: (b, t, col))
    return pl.pallas_call(
        _gdn_chunk_kernel,
        grid=(B, Tt // TOK_BLK),
        in_specs=[tok(0, GDN_QK_W), tok(1, GDN_QK_W), tok(2, GDN_V_W), tok(0, LANES)],
        out_specs=[tok(0, cw, 2), tok(0, cw), tok(0, cw), tok(0, N_CHAIN * CHUNK), tok(0, LANES)],
        out_shape=[jax.ShapeDtypeStruct((B, 2 * Tt, cw), BF16), jax.ShapeDtypeStruct((B, Tt, cw), BF16),
                   jax.ShapeDtypeStruct((B, Tt, cw), F32), jax.ShapeDtypeStruct((B, Tt, N_CHAIN * CHUNK), BF16),
                   jax.ShapeDtypeStruct((B, Tt, LANES), F32)],
        compiler_params=_cparams(("parallel", "parallel")),
        name="gdn_chunk",
    )(qkvp, qkvp, qkvp, gb)


def _gdn_scan_kernel(wqf_ref, kgf_ref, uf_ref, qkmf_ref, glf_ref, wqb_ref, kgb_ref, ub_ref, qkmb_ref, glb_ref,
                     of_ref, ob_ref, s_ref):
    @pl.when(pl.program_id(1) == 0)
    def _():
        s_ref[...] = jnp.zeros_like(s_ref)

    n_sub = TOK_BLK // CHUNK
    pw = 2 * GDN_DK
    same_head = (lax.broadcasted_iota(jnp.int32, (pw, pw), 0) >> 7) == (lax.broadcasted_iota(jnp.int32, (pw, pw), 1) >> 7)
    first = lax.broadcasted_iota(jnp.int32, (CHUNK, pw), 1) < GDN_DV
    first_row = first[0:1, :]
    dirs = ((wqf_ref, kgf_ref, uf_ref, qkmf_ref, glf_ref, of_ref), (wqb_ref, kgb_ref, ub_ref, qkmb_ref, glb_ref, ob_ref))
    n_pair = GDN_HEADS // 2
    for step in range(n_sub):
        work = []
        for d, refs in enumerate(dirs):
            j = step if d == 0 else n_sub - 1 - step
            work += [(d, p, j, slice(j * CHUNK, (j + 1) * CHUNK), slice(p * pw, (p + 1) * pw)) + refs
                     for p in range(n_pair)]
        ss = [s_ref[d * n_pair + p] for d, p, *_ in work]
        wss = [_dot(wq_ref[0, 2 * j * CHUNK:2 * (j + 1) * CHUNK, ps], s.astype(BF16))
               for s, (d, p, j, rows, ps, wq_ref, *_) in zip(ss, work)]
        vns = [u_ref[0, rows, ps] - ws[:CHUNK]
               for ws, (d, p, j, rows, ps, wq_ref, kg_ref, u_ref, *_) in zip(wss, work)]
        upds = [_dg(kg_ref[0, rows, ps], vn.astype(BF16), _TN)
                for vn, (d, p, j, rows, ps, wq_ref, kg_ref, *_) in zip(vns, work)]
        for s, ws, vn, upd, (d, p, j, rows, ps, wq_ref, kg_ref, u_ref, qkm_ref, gl_ref, o_ref) in zip(
                ss, wss, vns, upds, work):
            vn_bd = jnp.concatenate([jnp.where(first, vn, 0.0), jnp.where(first, 0.0, vn)], axis=0).astype(BF16)
            o_ref[0, rows, ps] = ws[CHUNK:] + _dot(qkm_ref[0, rows, p * 2 * CHUNK:(p + 1) * 2 * CHUNK], vn_bd)
            gl = gl_ref[0, j * CHUNK:j * CHUNK + 1, :]
            l0 = d * GDN_HEADS + 2 * p
            glv = jnp.where(first_row, gl[:, l0:l0 + 1], gl[:, l0 + 1:l0 + 2])
            s_ref[d * n_pair + p] = s * glv + jnp.where(same_head, upd, 0.0)


def _gdn_scan(wq, kg, u, qkm, gl, T):
    B, Tt, _ = kg.shape
    n_lat = T // TOK_BLK
    n_all = Tt // TOK_BLK
    n_ctx = n_all - n_lat

    def fwd(c):
        return jnp.where(c < n_ctx, n_lat + c, c - n_ctx)

    def bwd(c):
        return n_all - 1 - c

    hw = GDN_HEADS * GDN_DK

    def specs(order, d):
        blk = lambda r, w, col: pl.BlockSpec((1, r * TOK_BLK, w), lambda b, c: (b, order(c), col))
        return [blk(2, hw, d), blk(1, hw, d), blk(1, hw, d), blk(1, GDN_HEADS * CHUNK, d), blk(1, LANES, 0)]

    return pl.pallas_call(
        _gdn_scan_kernel,
        grid=(B, n_all),
        in_specs=specs(fwd, 0) + specs(bwd, 1),
        out_specs=[pl.BlockSpec((1, TOK_BLK, hw), lambda b, c: (b, fwd(c), 0)),
                   pl.BlockSpec((1, TOK_BLK, hw), lambda b, c: (b, bwd(c), 0))],
        out_shape=[jax.ShapeDtypeStruct((B, Tt, GDN_V_W), F32)] * 2,
        scratch_shapes=[pltpu.VMEM((N_CHAIN // 2, 2 * GDN_DK, 2 * GDN_DV), F32)],
        compiler_params=_cparams(("parallel", "arbitrary")),
        name="gdn_scan",
    )(wq, kg, u, qkm, gl, wq, kg, u, qkm, gl)


def _fnet_dense_kernel(c_ref, s_ref, u_ref, cc_ref, sc_ref, o_ref):
    u = u_ref[0]
    p = _dot(c_ref[...], u).astype(BF16)
    q = _dot(s_ref[...], u).astype(BF16)
    o_ref[0] = (_dot(p, cc_ref[...]) - _dot(q, sc_ref[...])).astype(o_ref.dtype)


def _fnet_split_kernel(c_ref, s_ref, tc_ref, ts_ref, u_ref, cc_ref, sc_ref, o_ref):
    w = cc_ref.shape[0]
    pc = _dot(c_ref[...], u_ref[0])
    qs = _dot(s_ref[...], u_ref[0])
    pe, po, qe, qo = pc[:, :w], pc[:, w:], qs[:, :w], qs[:, w:]
    tc, ts = tc_ref[...], ts_ref[...]
    a = tc * po - ts * qo
    bq = tc * qo + ts * po
    for half, (p, q) in enumerate(((pe + a, qe + bq), (pe - a, qe - bq))):
        o_ref[0, half] = (_dot(p.astype(BF16), cc_ref[...]) - _dot(q.astype(BF16), sc_ref[...])).astype(o_ref.dtype)


def _dft_mats(n, norm):
    j = jnp.arange(n, dtype=jnp.int32)
    ang = ((j[:, None] * j[None, :]) % n).astype(F32) * (2.0 * math.pi / n)
    return jnp.cos(ang) * norm, jnp.sin(ang) * norm


def _fnet(ub, mats_half, twiddles, mats_ctx, mats_ch, T):
    B, Tt, W = ub.shape
    Lc = Tt - T
    H = T // 2
    cc, sc = mats_ch
    tm = min(512, H)
    const = lambda shape: pl.BlockSpec(shape, lambda m, b: (0, 0))
    rowblk = lambda w: pl.BlockSpec((tm, w), lambda m, b: (m, 0))
    pairs = ub.reshape(B, Tt // 2, 2 * W)
    lat = pl.pallas_call(
        _fnet_split_kernel,
        grid=(H // tm, B),
        in_specs=[rowblk(H), rowblk(H), rowblk(W), rowblk(W), pl.BlockSpec((1, H, 2 * W), lambda m, b: (b, 0, 0)),
                  const((W, W)), const((W, W))],
        out_specs=pl.BlockSpec((1, 2, tm, W), lambda m, b: (b, 0, m, 0)),
        out_shape=jax.ShapeDtypeStruct((B, 2, H, W), BF16),
        compiler_params=_cparams(("parallel", "parallel"), VMEM_LIMIT_BIG),
        name="fnet_latent",
    )(mats_half[0], mats_half[1], twiddles[0], twiddles[1], pairs, cc, sc)
    cblk = T // Lc
    ctx = pl.pallas_call(
        _fnet_dense_kernel,
        grid=(1, B),
        in_specs=[const((Lc, Lc)), const((Lc, Lc)), pl.BlockSpec((1, Lc, W), lambda m, b: (b, cblk, 0)),
                  const((W, W)), const((W, W))],
        out_specs=pl.BlockSpec((1, Lc, W), lambda m, b: (b, 0, 0)),
        out_shape=jax.ShapeDtypeStruct((B, Lc, W), BF16),
        compiler_params=_cparams(("parallel", "parallel")),
        name="fnet_context",
    )(mats_ctx[0], mats_ctx[1], ub, cc, sc)
    return jnp.concatenate([lat.reshape(B, T, W), ctx], axis=1)


def _attend(q, keys, vals, biases, o_ref, orow):
    for h in range(NA_HEADS):
        hs = slice(h * NA_DH, (h + 1) * NA_DH)
        qh = q[:, hs]
        ss = []
        for kk, bias in zip(keys, biases):
            s = _dg(qh, kk[:, hs], _NT)
            if bias is not None:
                s = s + bias[h]
            ss.append(s)
        m = functools.reduce(jnp.maximum, [jnp.max(s, axis=-1, keepdims=True) for s in ss])
        ps = [jnp.exp(s - m) for s in ss]
        den = sum(jnp.sum(p, axis=-1, keepdims=True) for p in ps)
        o = sum(_dot(p.astype(BF16), vv[:, hs]) for p, vv in zip(ps, vals))
        o_ref[0, orow, hs] = (o / den).astype(o_ref.dtype)


NA_ROWS = 4
NA_KROWS = WIN_R + NA_ROWS - 1


def _na_window_row(g, rows):
    return jnp.clip(g * NA_ROWS - WIN_R // 2, 0, rows - NA_KROWS)


def _na_kernel(q_ref, k_ref, v_ref, bias_ref, o_ref, *, T, Lc, rows):
    g = pl.program_id(1)
    n_groups = rows // NA_ROWS
    kc = k_ref[0, T:T + Lc, :]
    vc = v_ref[0, T:T + Lc, :]

    @pl.when(g < n_groups)
    def _():
        start = pl.multiple_of(_na_window_row(g, rows) * GRID_W, GRID_W)
        nwin = NA_KROWS * GRID_W
        _attend(q_ref[0], [k_ref[0, pl.ds(start, nwin), :], kc], [v_ref[0, pl.ds(start, nwin), :], vc],
                [bias_ref, None], o_ref, slice(None))

    @pl.when(g == n_groups)
    def _():
        _attend(q_ref[0], [kc], [vc], [None], o_ref, slice(None))


def _na(naqkv, bias_tbl, T):
    B, Tt, _ = naqkv.shape
    Lc = Tt - T
    rows = T // GRID_W
    n_groups = rows // NA_ROWS
    assert NA_ROWS * GRID_W == Lc
    uniq, _ = _na_group_offsets(rows)

    def table_of(g):
        g = jnp.minimum(g, n_groups - 1)
        off = g * NA_ROWS - _na_window_row(g, rows)
        return sum(jnp.where(off > u, 1, 0) for u in uniq)

    return pl.pallas_call(
        functools.partial(_na_kernel, T=T, Lc=Lc, rows=rows),
        grid=(B, n_groups + 1),
        in_specs=[
            pl.BlockSpec((1, NA_ROWS * GRID_W, NA_W), lambda b, g: (b, g, 0)),
            pl.BlockSpec((1, Tt, NA_W), lambda b, g: (b, 0, 1)),
            pl.BlockSpec((1, Tt, NA_W), lambda b, g: (b, 0, 2)),
            pl.BlockSpec((None, NA_HEADS, NA_ROWS * GRID_W, NA_KROWS * GRID_W), lambda b, g: (table_of(g), 0, 0, 0)),
        ],
        out_specs=pl.BlockSpec((1, NA_ROWS * GRID_W, NA_W), lambda b, g: (b, g, 0)),
        out_shape=jax.ShapeDtypeStruct((B, Tt, NA_W), BF16),
        compiler_params=_cparams(("parallel", "arbitrary")),
        name="na",
    )(naqkv, naqkv, naqkv, bias_tbl)


def _na_group_offsets(rows):
    offs = [g * NA_ROWS - min(max(g * NA_ROWS - WIN_R // 2, 0), rows - NA_KROWS) for g in range(rows // NA_ROWS)]
    return sorted(set(offs)), offs


def _na_bias_tables(rpb, rows):
    L = rpb.shape[0]
    uniq, offs = _na_group_offsets(rows)
    qc = jnp.arange(GRID_W)
    c_start = jnp.clip(qc - WIN_C // 2, 0, GRID_W - WIN_C)
    kc = jnp.arange(GRID_W)
    dc = kc[None, :] - qc[:, None] + (WIN_C - 1)
    col_ok = (kc[None, :] >= c_start[:, None]) & (kc[None, :] < c_start[:, None] + WIN_C)
    onehot = (dc[None] == jnp.arange(2 * WIN_C - 1)[:, None, None]).astype(F32).reshape(2 * WIN_C - 1, -1)
    toep = jnp.dot(rpb.astype(F32).reshape(-1, 2 * WIN_C - 1), onehot, precision=HI)
    toep = toep.reshape(L, NA_HEADS, 2 * WIN_R - 1, GRID_W, GRID_W)
    tables = []
    for off in uniq:
        g = offs.index(off)
        ws = g * NA_ROWS - off
        rs = [min(max(g * NA_ROWS + i - WIN_R // 2, 0), rows - WIN_R) for i in range(NA_ROWS)]
        blocks = []
        for i in range(NA_ROWS):
            r = g * NA_ROWS + i
            per_key_row = []
            for a in range(NA_KROWS):
                inside = rs[i] <= ws + a < rs[i] + WIN_R
                dr = ws + a - r + (WIN_R - 1)
                per_key_row.append(jnp.where(col_ok, toep[:, :, dr], NEG_BIG) if inside
                                   else jnp.full((L, NA_HEADS, GRID_W, GRID_W), NEG_BIG, F32))
            blocks.append(jnp.concatenate(per_key_row, axis=-1))
        tables.append(jnp.concatenate(blocks, axis=-2))
    return jnp.stack(tables, axis=1)


def _merge_kernel(of_ref, ob_ref, z_ref, fb_ref, oc_ref, gr_ref, x_ref, mod_ref, gnw_ref, n2w_ref,
                  wa_ref, wb_ref, wc_ref, wo_ref, wr_ref, x1_ref, h2_ref, aff_ref):
    o = of_ref[0] + ob_ref[0]
    z = z_ref[0].astype(F32)
    ya = jnp.zeros((TOK_BLK, D_MODEL), F32)
    for h in range(GDN_HEADS):
        hs = slice(h * GDN_DV, (h + 1) * GDN_DV)
        oh = o[:, hs]
        on = oh * lax.rsqrt(jnp.mean(oh * oh, axis=-1, keepdims=True) + RMS_EPS) * gnw_ref[...]
        ya = ya + _dot((on * _silu(z[:, hs])).astype(BF16), wa_ref[hs, :])
    yb = _dot(fb_ref[0], wb_ref[...])
    yc = _dot(oc_ref[0], wc_ref[...])
    gate = lambda j: jax.nn.sigmoid(gr_ref[0, :, j * D_MODEL:(j + 1) * D_MODEL].astype(F32))
    y = gate(0) * ya + gate(1) * yb + gate(2) * yc
    x1 = x_ref[0] + mod_ref[2:3, :] * _dot(y.astype(BF16), wo_ref[...])
    x1_ref[0] = x1
    h2 = _rms_mod(x1, n2w_ref[...], mod_ref[3:4, :], mod_ref[4:5, :]).astype(BF16)
    h2_ref[0] = h2
    logits = _dg(wr_ref[...], h2, _NT)
    ex = jnp.exp(logits - jnp.max(logits, axis=0, keepdims=True))
    aff_ref[0] = ex / jnp.sum(ex, axis=0, keepdims=True)


def _merge(o_f, o_b, z, fb, oc, gr, xs, modv, gnw, n2w, wa, wb, wc, wo, wr, n_lat_blk):
    B, Tt, D = xs.shape
    tok = lambda w: pl.BlockSpec((1, TOK_BLK, w), lambda b, t: (b, t, 0))
    const = lambda shape: pl.BlockSpec(shape, lambda b, t: (0, 0))
    return pl.pallas_call(
        _merge_kernel,
        grid=(B, Tt // TOK_BLK),
        in_specs=[
            tok(GDN_V_W), tok(GDN_V_W), tok(GDN_V_W), tok(FNET_W), tok(NA_W), tok(GATE_W), tok(D),
            pl.BlockSpec((None, None, 6, D), lambda b, t: (b, t // n_lat_blk, 0, 0)),
            const((1, GDN_DV)), const((1, D)),
            const((GDN_V_W, D)), const((FNET_W, D)), const((NA_W, D)), const((D, D)),
            const((N_EXPERTS, D)),
        ],
        out_specs=[tok(D), tok(D), pl.BlockSpec((1, N_EXPERTS, TOK_BLK), lambda b, t: (b, 0, t))],
        out_shape=[jax.ShapeDtypeStruct((B, Tt, D), F32), jax.ShapeDtypeStruct((B, Tt, D), BF16),
                   jax.ShapeDtypeStruct((B, N_EXPERTS, Tt), F32)],
        compiler_params=_cparams(("parallel", "parallel"), VMEM_LIMIT_BIG),
        name="merge",
    )(o_f, o_b, z, fb, oc, gr, xs, modv, gnw, n2w, wa, wb, wc, wo, wr)


def _select_kernel(a_ref, pos_ref, wa_ref, lo_ref, cnt_ref, *, T, Lc):
    ii = lax.broadcasted_iota(jnp.int32, (LANES, LANES), 0)
    jj = lax.broadcasted_iota(jnp.int32, (LANES, LANES), 1)
    tri = (ii < jj).astype(BF16)
    cap_lat = EC_CAPACITY_FACTOR * T // N_EXPERTS
    cap_ctx = EC_CAPACITY_FACTOR * Lc // N_EXPERTS
    blk_lane = lax.broadcasted_iota(jnp.int32, (N_EXPERTS, LANES), 1)
    lo_all = jnp.zeros((N_EXPERTS, LANES), F32)
    cnt_all = jnp.zeros((N_EXPERTS, LANES), F32)
    for s0, n, cap, poff in ((0, T, cap_lat, 0), (T, Lc, cap_ctx, cap_lat)):
        a = a_ref[0, :, s0:s0 + n]
        bits = lax.bitcast_convert_type(a, jnp.int32)

        def count(mask):
            return jnp.sum(jnp.where(mask, 1.0, 0.0), axis=1, keepdims=True)

        def radix(i, pref):
            cand = pref | jnp.left_shift(jnp.int32(1), 30 - i)
            return jnp.where(count(bits >= cand) >= cap, cand, pref)

        thr = lax.fori_loop(0, 31, radix, jnp.zeros((N_EXPERTS, 1), jnp.int32))
        gt = bits > thr
        eq = bits == thr
        need = cap - count(gt)
        idx = lax.broadcasted_iota(jnp.int32, (N_EXPERTS, n), 1)
        nbits = max(1, (n - 1).bit_length())

        def tie(i, ans):
            cand = ans | jnp.left_shift(jnp.int32(1), nbits - 1 - i)
            return jnp.where(count(eq & (idx < cand)) < need, cand, ans)

        last = lax.fori_loop(0, nbits, tie, jnp.zeros((N_EXPERTS, 1), jnp.int32))
        sel = gt | (eq & (idx <= last))
        selb = jnp.where(sel, 1.0, 0.0).astype(BF16)
        ti = lax.broadcasted_iota(jnp.int32, (n, LANES), 0)
        tj = lax.broadcasted_iota(jnp.int32, (n, LANES), 1)
        seg_tot = _dot(selb, (ti // LANES == tj).astype(BF16))
        seg_off = _dot(seg_tot.astype(BF16), tri)
        for s in range(n // LANES):
            ls = slice(s * LANES, (s + 1) * LANES)
            within = _dot(selb[:, ls], tri)
            p = (within + seg_off[:, s:s + 1]).astype(jnp.int32) + poff
            pos_ref[0, :, s0 + s * LANES:s0 + (s + 1) * LANES] = jnp.where(sel[:, ls], p, -1)
        wa_ref[0, :, s0:s0 + n] = jnp.where(sel, a, 0.0)
        blk_tot = _dot(selb, ((ti + s0) // TOK_BLK == tj).astype(BF16))
        blk_off = _dot(blk_tot.astype(BF16), tri) + poff
        mine = (blk_lane >= s0 // TOK_BLK) & (blk_lane < (s0 + n) // TOK_BLK)
        lo_all = jnp.where(mine, blk_off, lo_all)
        cnt_all = jnp.where(mine, blk_tot, cnt_all)
    lo_ref[0] = lo_all.astype(jnp.int32)
    cnt_ref[0] = cnt_all.astype(jnp.int32)


def _select(aff, T):
    B, E, Tt = aff.shape
    blk = pl.BlockSpec((1, E, Tt), lambda b: (b, 0, 0))
    meta = pl.BlockSpec((1, E, LANES), lambda b: (b, 0, 0))
    return pl.pallas_call(
        functools.partial(_select_kernel, T=T, Lc=Tt - T),
        grid=(B,),
        in_specs=[blk],
        out_specs=[blk, blk, meta, meta],
        out_shape=[jax.ShapeDtypeStruct((B, E, Tt), jnp.int32), jax.ShapeDtypeStruct((B, E, Tt), F32),
                   jax.ShapeDtypeStruct((B, E, LANES), jnp.int32), jax.ShapeDtypeStruct((B, E, LANES), jnp.int32)],
        compiler_params=_cparams(("parallel",)),
        name="ec_select",
    )(aff)


MOE_WIN = 64


def _moe_windows(lo_ref, cnt_ref, nch):
    b, c = pl.program_id(0), pl.program_id(1)
    w0s = []
    nmax = jnp.int32(0)
    for e in range(N_EXPERTS):
        i = (b * N_EXPERTS + e) * nch + c
        lo, cnt = lo_ref[i], cnt_ref[i]
        w0 = (lo // BF16_SUBLANES) * BF16_SUBLANES
        w0s.append(w0)
        nmax = jnp.maximum(nmax, jnp.where(cnt > 0, (lo - w0 + cnt + MOE_WIN - 1) // MOE_WIN, 0))
    return w0s, nmax


def _window_starts(w0s, i, R):
    return [pl.multiple_of(jnp.minimum(w0 + i * MOE_WIN, R), BF16_SUBLANES) for w0 in w0s]


def _moe_gather_kernel(lo_ref, cnt_ref, h_ref, pos_ref, xe_ref, *, R, nch):
    @pl.when(pl.program_id(1) == 0)
    def _():
        xe_ref[...] = jnp.zeros_like(xe_ref)

    w0s, nmax = _moe_windows(lo_ref, cnt_ref, nch)
    slot0 = lax.broadcasted_iota(jnp.int32, (MOE_WIN, TOK_BLK), 0)

    def win(i, carry):
        r0s = _window_starts(w0s, i, R)
        onehot = jnp.concatenate(
            [jnp.where(pos_ref[e] == slot0 + r0s[e], 1.0, 0.0).astype(BF16) for e in range(N_EXPERTS)], axis=0)
        rows = _dot(onehot, h_ref[0]).astype(BF16)
        for e in range(N_EXPERTS):
            xe_ref[e, pl.ds(r0s[e], MOE_WIN), :] += rows[e * MOE_WIN:(e + 1) * MOE_WIN]
        return carry

    lax.fori_loop(0, nmax, win, 0)


def _moe_ffn_kernel(xe_ref, wgf_ref, wuf_ref, wdf_ref, ye_ref, wg_ref, wu_ref, wd_ref, *, R):
    @pl.when(pl.program_id(1) == 0)
    def _():
        wg_ref[...] = wgf_ref[...].astype(BF16)
        wu_ref[...] = wuf_ref[...].astype(BF16)
        wd_ref[...] = wdf_ref[...].astype(BF16)

    xe = xe_ref[0:R, :]
    fstep = 512
    ye = jnp.zeros((R, D_MODEL), F32)
    for f in range(D_EXPERT // fstep):
        fs = slice(f * fstep, (f + 1) * fstep)
        hid = _silu(_dot(xe, wg_ref[:, fs])) * _dot(xe, wu_ref[:, fs])
        ye = ye + _dot(hid.astype(BF16), wd_ref[fs, :])
    ye_ref[0:R, :] = ye.astype(BF16)
    ye_ref[R:R + MOE_WIN, :] = jnp.zeros((MOE_WIN, D_MODEL), BF16)


def _moe_scatter_kernel(lo_ref, cnt_ref, ye_ref, pos_ref, wa_ref, x_ref, mod_ref, o_ref, *, R, nch):
    w0s, nmax = _moe_windows(lo_ref, cnt_ref, nch)
    slot0 = lax.broadcasted_iota(jnp.int32, (MOE_WIN, TOK_BLK), 0)

    def win(i, acc):
        r0s = _window_starts(w0s, i, R)
        weighted = jnp.concatenate(
            [jnp.where(pos_ref[e] == slot0 + r0s[e], wa_ref[e], 0.0).astype(BF16) for e in range(N_EXPERTS)], axis=0)
        ye = jnp.concatenate([ye_ref[e, pl.ds(r0s[e], MOE_WIN), :] for e in range(N_EXPERTS)], axis=0)
        return acc + _dg(weighted, ye, _TN)

    acc = lax.fori_loop(0, nmax, win, jnp.zeros((TOK_BLK, D_MODEL), F32))
    o_ref[0] = x_ref[0] + mod_ref[5:6, :] * acc


def _moe(h2, pos, waff, lo, cnt, wg, wu, wd, layer, x1, modv, n_lat_blk, n_out_blk):
    B, Tt, D = h2.shape
    E = N_EXPERTS
    nch = Tt // TOK_BLK
    R = EC_CAPACITY_FACTOR * Tt // N_EXPERTS
    RP = R + MOE_WIN
    assert R % BF16_SUBLANES == 0
    pos = pos.reshape(B, E, nch, 1, TOK_BLK)
    waff = waff.reshape(B, E, nch, 1, TOK_BLK)
    lo = lo[:, :, :nch].reshape(-1)
    cnt = cnt[:, :, :nch].reshape(-1)
    rowspec = pl.BlockSpec((None, E, None, 1, TOK_BLK), lambda b, c, *_: (b, 0, c, 0, 0))
    tok = pl.BlockSpec((1, TOK_BLK, D), lambda b, c, *_: (b, c, 0))
    slots = pl.BlockSpec((None, E, RP, D), lambda b, c, *_: (b, 0, 0, 0))
    xe = pl.pallas_call(
        functools.partial(_moe_gather_kernel, R=R, nch=nch),
        grid_spec=pltpu.PrefetchScalarGridSpec(
            num_scalar_prefetch=2, grid=(B, nch), in_specs=[tok, rowspec], out_specs=slots),
        out_shape=jax.ShapeDtypeStruct((B, E, RP, D), BF16),
        compiler_params=_cparams(("parallel", "arbitrary"), VMEM_LIMIT_BIG),
        name="ec_gather",
    )(lo, cnt, h2, pos)
    wspec = lambda r, c: pl.BlockSpec((None, None, r, c), lambda e, b: (layer, e, 0, 0))
    slot1 = pl.BlockSpec((None, None, RP, D), lambda e, b: (b, e, 0, 0))
    ye = pl.pallas_call(
        functools.partial(_moe_ffn_kernel, R=R),
        grid=(E, B),
        in_specs=[slot1, wspec(D, D_EXPERT), wspec(D, D_EXPERT), wspec(D_EXPERT, D)],
        out_specs=slot1,
        out_shape=jax.ShapeDtypeStruct((B, E, RP, D), BF16),
        scratch_shapes=[pltpu.VMEM((D, D_EXPERT), BF16), pltpu.VMEM((D, D_EXPERT), BF16),
                        pltpu.VMEM((D_EXPERT, D), BF16)],
        compiler_params=_cparams(("parallel", "arbitrary"), VMEM_LIMIT_BIG),
        name="ec_ffn",
    )(xe, wg, wu, wd)
    return pl.pallas_call(
        functools.partial(_moe_scatter_kernel, R=R, nch=nch),
        grid_spec=pltpu.PrefetchScalarGridSpec(
            num_scalar_prefetch=2, grid=(B, n_out_blk),
            in_specs=[slots, rowspec, rowspec, tok,
                      pl.BlockSpec((None, None, 6, D), lambda b, c, *_: (b, c // n_lat_blk, 0, 0))],
            out_specs=tok),
        out_shape=jax.ShapeDtypeStruct((B, n_out_blk * TOK_BLK, D), F32),
        compiler_params=_cparams(("parallel", "arbitrary"), VMEM_LIMIT_BIG),
        name="ec_scatter",
    )(lo, cnt, ye, pos, waff, x1, modv)


def _rope_tables(T, Lc):
    nf = GDN_DK // 4
    inv = ROPE_BASE ** (-jnp.arange(nf, dtype=F32) / nf)
    t = jnp.arange(T)
    ang_r = (t // GRID_W).astype(F32)[:, None] * inv
    ang_c = (t % GRID_W).astype(F32)[:, None] * inv
    cos = jnp.concatenate([jnp.cos(ang_r)] * 2 + [jnp.cos(ang_c)] * 2, axis=-1)
    sin = jnp.concatenate([-jnp.sin(ang_r), jnp.sin(ang_r), -jnp.sin(ang_c), jnp.sin(ang_c)], axis=-1)
    cos = jnp.concatenate([cos, jnp.ones((Lc, GDN_DK), F32)], axis=0)
    sin = jnp.concatenate([sin, jnp.zeros((Lc, GDN_DK), F32)], axis=0)
    return cos, sin


def _reorder_w_in(w_in):
    qa, ka, va, za, aa, ba, ub, qn, kn, vn, gr = jnp.split(
        w_in, [512, 1024, 1536, 2048, 2056, 2064, 2320, 2576, 2832, 3088], axis=-1)
    pad = jnp.zeros(w_in.shape[:-1] + (AB_PAD - 2 * N_DIR * GDN_HEADS,), w_in.dtype)
    return jnp.concatenate([qa, ka, va, za, ub, qn, kn, vn, gr, aa, ba, pad], axis=-1).astype(BF16)


def kernel(x, c, ctx, c_ctx, w_mod, b_mod, norm1_w, norm2_w, w_in, conv_w, a_log, dt_bias, gdn_norm_w,
           na_qn_w, na_kn_w, na_rpb, w_br_a, w_br_b, w_br_c, w_out, w_router, w_e_gate, w_e_up, w_e_down):
    B, T, D = x.shape
    Lc = ctx.shape[1]
    Tt = T + Lc
    assert D == D_MODEL and T % TOK_BLK == 0 and Lc == TOK_BLK and T % Lc == 0
    assert T // GRID_W >= NA_KROWS and (T // GRID_W) % NA_ROWS == 0
    n_lat_blk = T // TOK_BLK

    w_in_r = _reorder_w_in(w_in)
    bf = lambda a: a.astype(BF16)
    w_mod16, wa16, wb16, wc16, wo16 = bf(w_mod), bf(w_br_a), bf(w_br_b), bf(w_br_c), bf(w_out)
    wr16 = bf(jnp.swapaxes(w_router, 1, 2))
    pad8 = AB_PAD - N_DIR * GDN_HEADS
    al = jnp.pad(a_log.reshape(DEPTH, 1, -1), ((0, 0), (0, 0), (0, pad8)))
    dtb = jnp.pad(dt_bias.reshape(DEPTH, 1, -1), ((0, 0), (0, 0), (0, pad8)))
    qnw = jnp.tile(na_qn_w, (1, NA_HEADS))[:, None, :]
    knw = jnp.tile(na_kn_w, (1, NA_HEADS))[:, None, :]
    head_of = jnp.arange(NA_W) // NA_DH
    bd = ((head_of[:, None] == head_of[None, :]).astype(F32) / NA_DH).astype(BF16)
    cos_t, sin_t = _rope_tables(T, Lc)
    mats_half = tuple(bf(m) for m in _dft_mats(T // 2, T ** -0.5))
    kk = jnp.arange(T // 2, dtype=F32)[:, None] * (2.0 * math.pi / T)
    twiddles = (jnp.broadcast_to(jnp.cos(kk), (T // 2, FNET_W)), jnp.broadcast_to(jnp.sin(kk), (T // 2, FNET_W)))
    mats_ctx = tuple(bf(m) for m in _dft_mats(Lc, Lc ** -0.5))
    cch, sch = _dft_mats(FNET_GROUP_CH, FNET_GROUP_CH ** -0.5)
    eye_g = jnp.eye(FNET_GROUPS, dtype=F32)
    mats_ch = (bf(jnp.kron(eye_g, cch)), bf(jnp.kron(eye_g, sch)))

    rows = -(-(B + 1) // 8) * 8
    cs = jnp.zeros((rows, D), F32).at[:B].set(c).at[B].set(c_ctx)
    mod = _modulation(cs, w_mod16, b_mod[:, None, :])
    mod_lat = mod[:, :B].reshape(DEPTH, B, 1, 6, D)
    mod_ctx = jnp.broadcast_to(mod[:, B].reshape(DEPTH, 1, 1, 6, D), (DEPTH, B, 1, 6, D))
    modv = jnp.concatenate([mod_lat, mod_ctx], axis=2)

    bias_tbls = _na_bias_tables(na_rpb, T // GRID_W)
    xs = jnp.concatenate([x, ctx], axis=1)
    for i in range(DEPTH):
        last = i == DEPTH - 1
        qkv, z, ub, naqkv, gr, gb = _input_proj(xs, modv[i], norm1_w[i][None], w_in_r[i], al[i], dtb[i],
                                                qnw[i], knw[i], bd, n_lat_blk)
        qkvp = _gdn_prep(qkv, conv_w[i], cos_t, sin_t, T)
        o_f, o_b = _gdn_scan(*_gdn_chunk(qkvp, gb), T)
        fb = _fnet(ub, mats_half, twiddles, mats_ctx, mats_ch, T)
        oc = _na(naqkv, bias_tbls[i], T)
        x1, h2, aff = _merge(o_f, o_b, z, fb, oc, gr, xs, modv[i], gdn_norm_w[i][None], norm2_w[i][None],
                             wa16[i], wb16[i], wc16[i], wo16[i], wr16[i], n_lat_blk)
        pos, waff, lo, cnt = _select(aff, T)
        xs = _moe(h2, pos, waff, lo, cnt, w_e_gate, w_e_up, w_e_down, i, x1, modv[i], n_lat_blk,
                  n_lat_blk if last else Tt // TOK_BLK)
    return xs
```
